```python
import math
import jax
import jax.numpy as jnp
from jax import lax
import numpy as np

D_MODEL = 4096
BATCH = 1
SEQ = 8192
DEPTH = 1
DEC_BATCH = 32
DEC_SEQ = 32
PAST_LEN = 2048

CHUNK = 64
Q_BLOCK = 128
ATT_WIDTH = D_MODEL // 2
HEAD_DIM_A = 128
N_HEADS_A = ATT_WIDTH // (2 * HEAD_DIM_A)
T5_BUCKETS = 32
T5_MAX_DISTANCE = 128
RWKV_WIDTH = D_MODEL - ATT_WIDTH
HEAD_B = 64
N_HEADS_B = RWKV_WIDTH // HEAD_B
RANK_W = 96
RANK_A = 96
RANK_G = 256
RWKV_PROJ = 3 * RWKV_WIDTH + RANK_W + RANK_A + RANK_G
IN_COLS = 3 * ATT_WIDTH + RWKV_PROJ
LNX_EPS = 64e-5
N_GROUPS = 8
EXP_PER_GROUP = 8
N_EXPERTS = N_GROUPS * EXP_PER_GROUP
TOP_K = 2
D_EXPERT = 1024
EXPERT_BLOCK = 128
PLE_DIM = 256
RMS_EPS = 1e-6
NEG_INF = -1e30

kernel_name = 'hymba_diffattn_rwkv7_hmoe_stream_step'


def rms_norm(x, g, eps=RMS_EPS):
    xf = x.astype(jnp.float32)
    y = xf * lax.rsqrt(jnp.mean(xf * xf, axis=-1, keepdims=True) + eps)
    return (y * g.astype(jnp.float32)).astype(x.dtype)


def t5_bucket(rel):
    half = T5_BUCKETS // 2
    max_exact = half // 2
    n = jnp.abs(rel)
    nf = jnp.maximum(n, 1).astype(jnp.float32)
    large = max_exact + (jnp.log(nf / max_exact) / math.log(T5_MAX_DISTANCE / max_exact)
                         * (half - max_exact)).astype(jnp.int32)
    large = jnp.minimum(large, half - 1)
    return jnp.where(rel > 0, half, 0) + jnp.where(n < max_exact, n, large)


def diff_attention(q, k, v, q_pos, k_pos, lam, t5_table):
    s = jnp.einsum('bqhjd,bkhjd->bhjqk', q, k).astype(jnp.float32) * (HEAD_DIM_A ** -0.5)
    rel = k_pos[None, :] - q_pos[:, None]
    bias = jnp.transpose(t5_table[t5_bucket(rel)].astype(jnp.float32), (2, 0, 1))
    visible = (k_pos[None, :] // CHUNK) <= (q_pos[:, None] // CHUNK)
    s = jnp.where(visible, s + bias[None, :, None], NEG_INF)
    pr = jax.nn.softmax(s, axis=-1)
    attn = pr[:, :, 0] - lam * pr[:, :, 1]
    return jnp.einsum('bhqk,bkhe->bqhe', attn.astype(v.dtype), v)


def attend(q, k, v, q_pos, k_pos, lam, t5_table):
    B, Tq = q.shape[0], q.shape[1]
    if Tq <= Q_BLOCK:
        return diff_attention(q, k, v, q_pos, k_pos, lam, t5_table)
    nb = Tq // Q_BLOCK
    qb = jnp.swapaxes(q.reshape(B, nb, Q_BLOCK, N_HEADS_A, 2, HEAD_DIM_A), 0, 1)
    pb = q_pos.reshape(nb, Q_BLOCK)
    ob = lax.map(lambda a: diff_attention(a[0], k, v, a[1], k_pos, lam, t5_table), (qb, pb))
    return jnp.swapaxes(ob, 0, 1).reshape(B, Tq, N_HEADS_A, 2 * HEAD_DIM_A)


def rwkv7_mix(feat, shift_prev, wkv0, lp):
    f32 = jnp.float32
    B, T, _ = feat.shape
    c = RWKV_WIDTH
    prev = jnp.concatenate([shift_prev.astype(feat.dtype), feat[:, :-1]], axis=1)
    xm = feat + (prev - feat) * lp['rwkv_mu']
    r, k, v = xm[..., :c], xm[..., c:2 * c], xm[..., 2 * c:3 * c]
    o = 3 * c
    wd = xm[..., o:o + RANK_W]
    ad = xm[..., o + RANK_W:o + RANK_W + RANK_A]
    gd = xm[..., o + RANK_W + RANK_A:]
    w_log = -jax.nn.softplus(-(lp['rwkv_w0'] + jnp.tanh(wd) @ lp['rwkv_w2']).astype(f32)) - 0.5
    decay = jnp.exp(-jnp.exp(w_log))
    a = jax.nn.sigmoid((lp['rwkv_a0'] + ad @ lp['rwkv_a2']).astype(f32))
    g = (jax.nn.sigmoid(gd) @ lp['rwkv_g2']).astype(f32)
    rf, kf, vf = r.astype(f32), k.astype(f32), v.astype(f32)
    heads = lambda t: t.reshape(B, T, N_HEADS_B, HEAD_B)
    kk = heads(kf * lp['rwkv_k_k'].astype(f32))
    kk = kk / jnp.maximum(jnp.sqrt(jnp.sum(kk * kk, axis=-1, keepdims=True)), 1e-12)
    kf = kf * (1.0 + (a - 1.0) * lp['rwkv_k_a'].astype(f32))
    rh, wh, kh, vh, ah = heads(rf), heads(decay), heads(kf), heads(vf), heads(a)

    def step(S, inp):
        r_t, w_t, k_t, v_t, kk_t, a_t = inp
        sa = jnp.einsum('bhvk,bhk->bhv', S, -kk_t)
        S = (S * w_t[:, :, None, :] + sa[..., None] * (kk_t * a_t)[:, :, None, :]
             + v_t[..., None] * k_t[:, :, None, :])
        return S, jnp.einsum('bhvk,bhk->bhv', S, r_t)

    tm = lambda t: jnp.moveaxis(t, 1, 0)
    S_fin, ys = lax.scan(step, wkv0.astype(f32), (tm(rh), tm(wh), tm(kh), tm(vh), tm(kk), tm(ah)))
    y = jnp.moveaxis(ys, 0, 1)
    mean = jnp.mean(y, axis=-1, keepdims=True)
    var = jnp.mean(jnp.square(y - mean), axis=-1, keepdims=True)
    y = ((y - mean) * lax.rsqrt(var + LNX_EPS)).reshape(B, T, c)
    y = y * lp['lnx_g'].astype(f32) + lp['lnx_b'].astype(f32)
    bonus = jnp.sum(rh * kh * lp['rwkv_r_k'].astype(f32), axis=-1, keepdims=True) * vh
    out = (y + bonus.reshape(B, T, c)) * g
    return out.astype(feat.dtype), S_fin, feat[:, -1:]


def swiglu(x, wg, wu, wd):
    return (jax.nn.silu(x @ wg) * (x @ wu)) @ wd


def grouped_experts(h, expert_idx, gates, w_gate, w_up, w_down):
    N, D = h.shape
    A = N * TOP_K
    flat_e = expert_idx.reshape(A)
    flat_tok = jnp.arange(A, dtype=jnp.int32) // TOP_K
    flat_gate = gates.reshape(A)
    order = jnp.argsort(flat_e)
    se, stok, sgate = flat_e[order], flat_tok[order], flat_gate[order]
    counts = jnp.zeros((N_EXPERTS,), jnp.int32).at[flat_e].add(1)
    start = jnp.cumsum(counts) - counts
    pcounts = (counts + EXPERT_BLOCK - 1) // EXPERT_BLOCK * EXPERT_BLOCK
    pend = jnp.cumsum(pcounts)
    pstart = pend - pcounts
    dest = pstart[se] + jnp.arange(A, dtype=jnp.int32) - start[se]
    n_blocks = -(-A // EXPERT_BLOCK) + N_EXPERTS
    rows_tok = jnp.zeros((n_blocks * EXPERT_BLOCK,), jnp.int32).at[dest].set(stok)
    block_e = jnp.searchsorted(pend, jnp.arange(n_blocks, dtype=jnp.int32) * EXPERT_BLOCK, side='right')
    block_e = jnp.minimum(block_e, N_EXPERTS - 1)
    xb = h[rows_tok].reshape(n_blocks, EXPERT_BLOCK, D)
    yb = lax.map(lambda a: swiglu(a[0], w_gate[a[1]], w_up[a[1]], w_down[a[1]]), (xb, block_e))
    contrib = yb.reshape(-1, D)[dest] * sgate[:, None]
    return jnp.zeros_like(h).at[stok].add(contrib)


def hier_moe(h, lp):
    N = h.shape[0]
    g_logits = (h @ lp['rg_w']).astype(jnp.float32) + lp['rg_b'].astype(jnp.float32)
    g_prob = jax.nn.softmax(g_logits, axis=-1)
    g_top, g_idx = lax.top_k(g_prob, 1)
    i_logits = (h @ lp['ri_w']).reshape(N, N_GROUPS, EXP_PER_GROUP).astype(jnp.float32)
    i_logits = i_logits + lp['ri_b'].astype(jnp.float32)
    i_logits = jnp.take_along_axis(i_logits, g_idx[:, :, None], axis=1)[:, 0]
    i_top, i_idx = lax.top_k(jax.nn.softmax(i_logits, axis=-1), TOP_K)
    gates = g_top * i_top / jnp.sum(i_top, axis=-1, keepdims=True)
    expert_idx = g_idx * EXP_PER_GROUP + i_idx
    return grouped_experts(h, expert_idx, gates.astype(h.dtype), lp['e_wg'], lp['e_wu'], lp['e_wd'])


def trunk_layer(x, pe, k_past, v_past, wkv0, shift_prev, t5_table, lam_init, lp):
    B, T, _ = x.shape
    past = k_past.shape[1]
    f32 = jnp.float32
    aw = ATT_WIDTH
    h = rms_norm(x, lp['norm1_g'])
    proj = h @ lp['w_in']
    q = rms_norm(proj[..., :aw].reshape(B, T, N_HEADS_A, 2, HEAD_DIM_A), lp['q_norm_g'])
    k = rms_norm(proj[..., aw:2 * aw].reshape(B, T, N_HEADS_A, 2, HEAD_DIM_A), lp['k_norm_g'])
    v = proj[..., 2 * aw:3 * aw].reshape(B, T, N_HEADS_A, 2 * HEAD_DIM_A)
    k_all = jnp.concatenate([k_past.astype(k.dtype), k], axis=1)
    v_all = jnp.concatenate([v_past.astype(v.dtype), v], axis=1)
    q_pos = past + jnp.arange(T, dtype=jnp.int32)
    k_pos = jnp.arange(past + T, dtype=jnp.int32)
    lam = (jnp.exp(jnp.sum(lp['lambda_q1'].astype(f32) * lp['lambda_k1'].astype(f32)))
           - jnp.exp(jnp.sum(lp['lambda_q2'].astype(f32) * lp['lambda_k2'].astype(f32))) + lam_init)
    o = attend(q, k_all, v_all, q_pos, k_pos, lam, t5_table)
    y_a = (rms_norm(o, lp['subln_g']) * (1.0 - lam_init)).reshape(B, T, aw)
    y_b, wkv_new, shift_new = rwkv7_mix(proj[..., 3 * aw:], shift_prev, wkv0, lp)
    x = x + jnp.concatenate([y_a, y_b], axis=-1) @ lp['w_out']
    h2 = rms_norm(x, lp['norm2_g']).reshape(B * T, D_MODEL)
    x = x + hier_moe(h2, lp).reshape(B, T, D_MODEL)
    gate = jax.nn.sigmoid(rms_norm(x, lp['ple_norm_g']) @ lp['ple_gate_w'])
    x = x + (pe @ lp['ple_proj_w']) * gate
    return x, k, v, wkv_new, shift_new


def setup_inputs(seed: int = 0) -> dict:
    key = jax.random.key(seed)
    ks = iter(jax.random.split(key, 64))
    f32 = jnp.float32
    nrm = lambda shape, scale: jax.random.normal(next(ks), shape, f32) * scale
    gain = lambda shape: 1.0 + 0.05 * jax.random.normal(next(ks), shape, f32)
    L = DEPTH
    return {
        'x_prompt': nrm((BATCH, SEQ, D_MODEL), 1.0),
        'x_sample': nrm((DEC_BATCH, DEC_SEQ, D_MODEL), 1.0),
        'p_prompt': nrm((DEPTH, BATCH, SEQ, PLE_DIM), 1.0),
        'p_sample': nrm((DEPTH, DEC_BATCH, DEC_SEQ, PLE_DIM), 1.0),
        'cache_k': nrm((L, DEC_BATCH, PAST_LEN, N_HEADS_A, 2, HEAD_DIM_A), 1.0),
        'cache_v': nrm((L, DEC_BATCH, PAST_LEN, N_HEADS_A, 2 * HEAD_DIM_A), 1.0),
        'state_wkv': nrm((L, DEC_BATCH, N_HEADS_B, HEAD_B, HEAD_B), 0.5),
        'state_shift': nrm((L, DEC_BATCH, 1, RWKV_PROJ), 1.0),
        't5_table': nrm((T5_BUCKETS, N_HEADS_A), 0.5),
        'norm1_g': gain((L, D_MODEL)),
        'w_in': nrm((L, D_MODEL, IN_COLS), D_MODEL ** -0.5),
        'q_norm_g': gain((L, HEAD_DIM_A)),
        'k_norm_g': gain((L, HEAD_DIM_A)),
        'lambda_q1': nrm((L, HEAD_DIM_A), 0.1),
        'lambda_k1': nrm((L, HEAD_DIM_A), 0.1),
        'lambda_q2': nrm((L, HEAD_DIM_A), 0.1),
        'lambda_k2': nrm((L, HEAD_DIM_A), 0.1),
        'subln_g': gain((L, 2 * HEAD_DIM_A)),
        'rwkv_mu': jax.random.uniform(next(ks), (L, RWKV_PROJ), f32, 0.0, 1.0),
        'rwkv_w0': jax.random.uniform(next(ks), (L, RWKV_WIDTH), f32, -6.5, -1.5),
        'rwkv_w2': nrm((L, RANK_W, RWKV_WIDTH), 0.1 * RANK_W ** -0.5),
        'rwkv_a0': nrm((L, RWKV_WIDTH), 0.1),
        'rwkv_a2': nrm((L, RANK_A, RWKV_WIDTH), RANK_A ** -0.5),
        'rwkv_g2': nrm((L, RANK_G, RWKV_WIDTH), RANK_G ** -0.5),
        'rwkv_k_k': 0.85 + nrm((L, RWKV_WIDTH), 0.02),
        'rwkv_k_a': 1.0 + nrm((L, RWKV_WIDTH), 0.02),
        'rwkv_r_k': nrm((L, N_HEADS_B, HEAD_B), 0.1),
        'lnx_g': gain((L, RWKV_WIDTH)),
        'lnx_b': nrm((L, RWKV_WIDTH), 0.02),
        'w_out': nrm((L, D_MODEL, D_MODEL), D_MODEL ** -0.5),
        'norm2_g': gain((L, D_MODEL)),
        'router_group_w': nrm((L, D_MODEL, N_GROUPS), D_MODEL ** -0.5),
        'router_group_b': nrm((L, N_GROUPS), 0.01),
        'router_inner_w': nrm((L, D_MODEL, N_EXPERTS), D_MODEL ** -0.5),
        'router_inner_b': nrm((L, N_GROUPS, EXP_PER_GROUP), 0.01),
        'expert_w_gate': nrm((L, N_EXPERTS, D_MODEL, D_EXPERT), D_MODEL ** -0.5),
        'expert_w_up': nrm((L, N_EXPERTS, D_MODEL, D_EXPERT), D_MODEL ** -0.5),
        'expert_w_down': nrm((L, N_EXPERTS, D_EXPERT, D_MODEL), D_EXPERT ** -0.5),
        'ple_norm_g': gain((L, D_MODEL)),
        'ple_gate_w': nrm((L, D_MODEL, D_MODEL), D_MODEL ** -0.5),
        'ple_proj_w': nrm((L, PLE_DIM, D_MODEL), PLE_DIM ** -0.5),
    }


def reference(x_prompt, x_sample, p_prompt, p_sample, cache_k, cache_v, state_wkv, state_shift,
              t5_table, norm1_g, w_in, q_norm_g, k_norm_g, lambda_q1, lambda_k1, lambda_q2, lambda_k2,
              subln_g, rwkv_mu, rwkv_w0, rwkv_w2, rwkv_a0, rwkv_a2, rwkv_g2, rwkv_k_k, rwkv_k_a, rwkv_r_k,
              lnx_g, lnx_b, w_out, norm2_g, router_group_w, router_group_b, router_inner_w, router_inner_b,
              expert_w_gate, expert_w_up, expert_w_down, ple_norm_g, ple_gate_w, ple_proj_w):
    xp, xs = x_prompt, x_sample
    B = xp.shape[0]
    kp_l, vp_l, hp_l, sp_l = [], [], [], []
    ks_l, vs_l, hs_l, ss_l = [], [], [], []
    for i in range(DEPTH):
        lp = dict(norm1_g=norm1_g[i], w_in=w_in[i], q_norm_g=q_norm_g[i], k_norm_g=k_norm_g[i],
                  lambda_q1=lambda_q1[i], lambda_k1=lambda_k1[i], lambda_q2=lambda_q2[i],
                  lambda_k2=lambda_k2[i], subln_g=subln_g[i], rwkv_mu=rwkv_mu[i], rwkv_w0=rwkv_w0[i],
                  rwkv_w2=rwkv_w2[i], rwkv_a0=rwkv_a0[i], rwkv_a2=rwkv_a2[i], rwkv_g2=rwkv_g2[i],
                  rwkv_k_k=rwkv_k_k[i], rwkv_k_a=rwkv_k_a[i], rwkv_r_k=rwkv_r_k[i], lnx_g=lnx_g[i],
                  lnx_b=lnx_b[i], w_out=w_out[i], norm2_g=norm2_g[i], rg_w=router_group_w[i],
                  rg_b=router_group_b[i], ri_w=router_inner_w[i], ri_b=router_inner_b[i],
                  e_wg=expert_w_gate[i], e_wu=expert_w_up[i], e_wd=expert_w_down[i],
                  ple_norm_g=ple_norm_g[i], ple_gate_w=ple_gate_w[i], ple_proj_w=ple_proj_w[i])
        lam_init = 0.8 - 0.6 * math.exp(-0.3 * i)
        xp, k_new, v_new, wkv_new, shift_new = trunk_layer(
            xp, p_prompt[i],
            jnp.zeros((B, 0, N_HEADS_A, 2, HEAD_DIM_A), xp.dtype),
            jnp.zeros((B, 0, N_HEADS_A, 2 * HEAD_DIM_A), xp.dtype),
            jnp.zeros((B, N_HEADS_B, HEAD_B, HEAD_B), jnp.float32),
            jnp.zeros((B, 1, RWKV_PROJ), xp.dtype),
            t5_table, lam_init, lp)
        kp_l.append(k_new); vp_l.append(v_new); hp_l.append(wkv_new); sp_l.append(shift_new)
        xs, k_new, v_new, wkv_new, shift_new = trunk_layer(
            xs, p_sample[i], cache_k[i], cache_v[i], state_wkv[i], state_shift[i],
            t5_table, lam_init, lp)
        ks_l.append(k_new); vs_l.append(v_new); hs_l.append(wkv_new); ss_l.append(shift_new)
    return (xp, xs, jnp.stack(kp_l), jnp.stack(vp_l), jnp.stack(hp_l), jnp.stack(sp_l),
            jnp.stack(ks_l), jnp.stack(vs_l), jnp.stack(hs_l), jnp.stack(ss_l))
```

```python
import functools
import math

import jax
import jax.numpy as jnp
from jax import lax
from jax.experimental import pallas as pl
from jax.experimental.pallas import tpu as pltpu

F32 = jnp.float32
BF16 = jnp.bfloat16

LANES = 128
SUBLANES = 8
VMEM_BYTES_V7X = 64 * 1024 * 1024
VMEM_LIMIT = VMEM_BYTES_V7X - 6 * 1024 * 1024

CHUNK = 64
HEAD_DIM_A = 128
N_HEADS_A = 8
ATT_WIDTH = 2 * HEAD_DIM_A * N_HEADS_A
T5_BUCKETS = 32
HEAD_B = 64
N_HEADS_B = 32
RWKV_WIDTH = HEAD_B * N_HEADS_B
RANK_W = 96
RANK_A = 96
RANK_G = 256
LORA_COLS = 512
LNX_EPS = 64e-5
N_GROUPS = 8
EXP_PER_GROUP = 8
N_EXPERTS = N_GROUPS * EXP_PER_GROUP
TOP_K = 2
RMS_EPS = 1e-6
NEG_INF = -1e30
T5_LOG_THRESHOLDS = (12, 16, 23, 32, 46, 64, 91)
T5_FAR = 128

K_LO = LANES // N_HEADS_B
K_HI = HEAD_B // K_LO
V_BLK = HEAD_B // SUBLANES
SCAN_TB = 32

EXPERT_ROWS = 256


def _cparams(sem, vmem=VMEM_LIMIT):
    return pltpu.CompilerParams(dimension_semantics=sem, vmem_limit_bytes=vmem)


def _exact_div(a, b):
    assert a % b == 0, (a, b)
    return a // b


def _sigmoid(x):
    return 1.0 / (1.0 + jnp.exp(-x))


def _rmsnorm_kernel(x_ref, g_ref, o_ref):
    x = x_ref[...]
    ms = jnp.mean(x * x, axis=-1, keepdims=True)
    o_ref[...] = (x * lax.rsqrt(ms + RMS_EPS) * g_ref[...]).astype(o_ref.dtype)


def rmsnorm_cast(x, g, tm=256):
    m, d = x.shape
    return pl.pallas_call(
        _rmsnorm_kernel,
        out_shape=jax.ShapeDtypeStruct((m, d), BF16),
        grid=(_exact_div(m, tm),),
        in_specs=[pl.BlockSpec((tm, d), lambda i: (i, 0)), pl.BlockSpec((1, d), lambda i: (0, 0))],
        out_specs=pl.BlockSpec((tm, d), lambda i: (i, 0)),
        compiler_params=_cparams(("parallel",)),
        name="rmsnorm_cast",
    )(x, g.reshape(1, d))


def _mm_kernel(a_ref, w_ref, *rest, mode):
    if mode == "plain":
        o_ref, wb_ref = rest
    elif mode == "residual":
        res_ref, o_ref, wb_ref = rest
    else:
        res_ref, p_ref, pw_ref, o_ref, wb_ref = rest

    @pl.when(pl.program_id(1) == 0)
    def _():
        wb_ref[...] = w_ref[...].astype(BF16)

    acc = jnp.dot(a_ref[...], wb_ref[...], preferred_element_type=F32)
    if mode == "plain":
        o_ref[...] = acc
    elif mode == "residual":
        o_ref[...] = res_ref[...] + acc
    else:
        pe = jnp.dot(p_ref[...], pw_ref[...].astype(BF16), preferred_element_type=F32)
        o_ref[...] = res_ref[...] + pe * _sigmoid(acc)


def matmul(a, w, n_cols, *, col_block0=0, mode="plain", res=None, p=None, pw=None, tm=1024, tn=512):
    m, k = a.shape
    tm = min(tm, m)
    assert m % tm == 0 and n_cols % tn == 0
    in_specs = [pl.BlockSpec((tm, k), lambda n, i: (i, 0)),
                pl.BlockSpec((k, tn), lambda n, i: (0, n + col_block0))]
    args = [a, w]
    if mode in ("residual", "ple"):
        in_specs.append(pl.BlockSpec((tm, tn), lambda n, i: (i, n)))
        args.append(res)
    if mode == "ple":
        kp = p.shape[1]
        in_specs += [pl.BlockSpec((tm, kp), lambda n, i: (i, 0)), pl.BlockSpec((kp, tn), lambda n, i: (0, n))]
        args += [p, pw]
    return pl.pallas_call(
        functools.partial(_mm_kernel, mode=mode),
        out_shape=jax.ShapeDtypeStruct((m, n_cols), F32),
        grid=(n_cols // tn, m // tm),
        in_specs=in_specs,
        out_specs=pl.BlockSpec((tm, tn), lambda n, i: (i, n)),
        scratch_shapes=[pltpu.VMEM((k, tn), BF16)],
        compiler_params=_cparams(("arbitrary", "arbitrary")),
        name="matmul_" + mode,
    )(*args)


def _qk_norm_kernel(q_ref, k_ref, v_ref, qg_ref, kg_ref, qo_ref, ko_ref, kbo_ref, vo_ref):
    def head_norm(x, g):
        ms = jnp.mean(x * x, axis=-1, keepdims=True)
        return x * lax.rsqrt(ms + RMS_EPS) * g

    qg = qg_ref[...]
    kg = kg_ref[...]
    for c in range(ATT_WIDTH // HEAD_DIM_A):
        sl = slice(c * HEAD_DIM_A, (c + 1) * HEAD_DIM_A)
        qn = head_norm(q_ref[:, sl], qg)
        qo_ref[:, sl] = (qn * (HEAD_DIM_A ** -0.5)).astype(BF16)
        kn = head_norm(k_ref[:, sl], kg)
        ko_ref[:, sl] = kn
        kbo_ref[:, sl] = kn.astype(BF16)
    vo_ref[...] = v_ref[...].astype(BF16)


def qk_norm(proj_qkv, q_g, k_g, tm=256):
    m = proj_qkv.shape[0]
    w = ATT_WIDTH
    blk = lambda c: pl.BlockSpec((tm, w), lambda i, c=c: (i, c))
    vec = pl.BlockSpec((1, HEAD_DIM_A), lambda i: (0, 0))
    out_blk = pl.BlockSpec((tm, w), lambda i: (i, 0))
    return pl.pallas_call(
        _qk_norm_kernel,
        out_shape=(jax.ShapeDtypeStruct((m, w), BF16), jax.ShapeDtypeStruct((m, w), F32),
                   jax.ShapeDtypeStruct((m, w), BF16), jax.ShapeDtypeStruct((m, w), BF16)),
        grid=(_exact_div(m, tm),),
        in_specs=[blk(0), blk(1), blk(2), vec, vec],
        out_specs=(out_blk, out_blk, out_blk, out_blk),
        compiler_params=_cparams(("parallel",)),
        name="qk_norm",
    )(proj_qkv, proj_qkv, proj_qkv, q_g.reshape(1, -1), k_g.reshape(1, -1))


def _bias_kernel(tab_ref, o_ref, *, rel0, q0, k0, masked):
    _, nr, nc = o_ref.shape
    r = lax.broadcasted_iota(jnp.int32, (nr, nc), 0)
    c = lax.broadcasted_iota(jnp.int32, (nr, nc), 1)
    rel = rel0 + c - r
    n = jnp.abs(rel)
    large = jnp.full((nr, nc), T5_BUCKETS // 4, jnp.int32)
    for thr in T5_LOG_THRESHOLDS:
        large = large + jnp.where(n >= thr, 1, 0)
    bucket = jnp.where(n < T5_BUCKETS // 4, n, large) + jnp.where(rel > 0, T5_BUCKETS // 2, 0)
    if masked:
        visible = ((k0 + c) // CHUNK) <= ((q0 + r) // CHUNK)
    for h in range(N_HEADS_A):
        acc = jnp.zeros((nr, nc), F32)
        for b in range(T5_BUCKETS):
            acc = jnp.where(bucket == b, tab_ref[b, h], acc)
        if masked:
            acc = jnp.where(visible, acc, NEG_INF)
        o_ref[h] = acc


def bias_tiles(table, nr, nc, *, rel0, masked=False):
    return pl.pallas_call(
        functools.partial(_bias_kernel, rel0=rel0, q0=0, k0=0, masked=masked),
        out_shape=jax.ShapeDtypeStruct((N_HEADS_A, nr, nc), F32),
        in_specs=[pl.BlockSpec(memory_space=pltpu.SMEM)],
        out_specs=pl.BlockSpec(memory_space=pltpu.VMEM),
        compiler_params=_cparams(None),
        name="t5_bias",
    )(table)


def _lambda_value(lq1, lk1, lq2, lk2, lam_init):
    s1 = jnp.sum(lq1 * lk1, axis=-1, keepdims=True)
    s2 = jnp.sum(lq2 * lk2, axis=-1, keepdims=True)
    return jnp.exp(s1) - jnp.exp(s2) + lam_init


def _online_update(s, v, m_ref, l_ref, acc_ref, idx):
    m_prev = m_ref[idx]
    m_new = jnp.maximum(m_prev, jnp.max(s, axis=-1, keepdims=True))
    alpha = jnp.exp(m_prev - m_new)
    p = jnp.exp(s - m_new)
    l_ref[idx] = alpha * l_ref[idx] + jnp.sum(p, axis=-1, keepdims=True)
    acc_ref[idx] = alpha * acc_ref[idx] + jnp.dot(p.astype(BF16), v, preferred_element_type=F32)
    m_ref[idx] = m_new


def _diff_finish(acc1, l1, acc2, l2, lam, g, lam_init):
    o = acc1 / l1 - lam * (acc2 / l2)
    ms = jnp.mean(o * o, axis=-1, keepdims=True)
    return (o * lax.rsqrt(ms + RMS_EPS) * g) * (1.0 - lam_init)


_NT = (((1,), (1,)), ((), ()))


def _prompt_attn_kernel(far_ref, q_ref, k_ref, v_ref, bd_ref, bl_ref, lq1, lk1, lq2, lk2, g_ref,
                        o_ref, m_ref, l_ref, acc_ref, *, blk, lam_init):
    h = pl.program_id(0)
    i = pl.program_id(1)
    m_ref[...] = jnp.full(m_ref.shape, NEG_INF, F32)
    l_ref[...] = jnp.zeros(l_ref.shape, F32)
    acc_ref[...] = jnp.zeros(acc_ref.shape, F32)
    d = HEAD_DIM_A

    def update(r0, bias):
        kb = k_ref[pl.ds(r0, blk), :]
        vb = v_ref[pl.ds(r0, blk), :]
        for mp in range(2):
            s = lax.dot_general(q_ref[:, mp * d:(mp + 1) * d], kb[:, mp * d:(mp + 1) * d], _NT,
                                preferred_element_type=F32)
            _online_update(s + bias, vb, m_ref, l_ref, acc_ref, mp)

    far_bias = far_ref[h]

    def far_body(j, carry):
        update(pl.multiple_of(j * blk, blk), far_bias)
        return carry

    lax.fori_loop(0, jnp.maximum(i - 1, 0), far_body, 0)

    @pl.when(i >= 1)
    def _():
        update(pl.multiple_of((i - 1) * blk, blk), bl_ref[0])

    update(pl.multiple_of(i * blk, blk), bd_ref[0])
    lam = _lambda_value(lq1[...], lk1[...], lq2[...], lk2[...], lam_init)
    o_ref[...] = _diff_finish(acc_ref[0], l_ref[0], acc_ref[1], l_ref[1], lam, g_ref[...], lam_init).astype(BF16)


def prompt_attention(qn, kn, vn, t, bias_d, bias_l, far_bias, lams, subln_g, lam_init, blk):
    nq = t // blk
    hw = 2 * HEAD_DIM_A
    vec = pl.BlockSpec((1, HEAD_DIM_A), lambda h, i, far: (0, 0))
    return pl.pallas_call(
        functools.partial(_prompt_attn_kernel, blk=blk, lam_init=lam_init),
        out_shape=jax.ShapeDtypeStruct((t, ATT_WIDTH), BF16),
        grid_spec=pltpu.PrefetchScalarGridSpec(
            num_scalar_prefetch=1,
            grid=(N_HEADS_A, nq),
            in_specs=[pl.BlockSpec((blk, hw), lambda h, i, far: (i, h)),
                      pl.BlockSpec((t, hw), lambda h, i, far: (0, h)),
                      pl.BlockSpec((t, hw), lambda h, i, far: (0, h)),
                      pl.BlockSpec((1, blk, blk), lambda h, i, far: (h, 0, 0)),
                      pl.BlockSpec((1, blk, blk), lambda h, i, far: (h, 0, 0)),
                      vec, vec, vec, vec,
                      pl.BlockSpec((1, hw), lambda h, i, far: (0, 0))],
            out_specs=pl.BlockSpec((blk, hw), lambda h, i, far: (i, h)),
            scratch_shapes=[pltpu.VMEM((2, blk, 1), F32), pltpu.VMEM((2, blk, 1), F32),
                            pltpu.VMEM((2, blk, hw), F32)]),
        compiler_params=_cparams(("arbitrary", "arbitrary")),
        name="prompt_attention",
    )(far_bias, qn, kn, vn, bias_d, bias_l, *lams, subln_g.reshape(1, hw))


def _sample_attn_kernel(q_ref, ck_ref, cv_ref, kn_ref, vn_ref, bp_ref, bn_ref, lq1, lk1, lq2, lk2, g_ref,
                        o_ref, m_ref, l_ref, acc_ref, *, lam_init):
    t = pl.program_id(1)
    d = HEAD_DIM_A

    @pl.when(t == 0)
    def _():
        m_ref[...] = jnp.full(m_ref.shape, NEG_INF, F32)
        l_ref[...] = jnp.zeros(l_ref.shape, F32)
        acc_ref[...] = jnp.zeros(acc_ref.shape, F32)

    def update(kb, vb, bias_of_head):
        for h in range(N_HEADS_A):
            bias = bias_of_head(h)
            for mp in range(2):
                c0 = (2 * h + mp) * d
                s = lax.dot_general(q_ref[:, c0:c0 + d], kb[:, c0:c0 + d], _NT, preferred_element_type=F32)
                _online_update(s + bias, vb[:, 2 * h * d:2 * (h + 1) * d], m_ref, l_ref, acc_ref, 2 * h + mp)

    update(ck_ref[0].astype(BF16), cv_ref[0].astype(BF16), lambda h: bp_ref[h])

    @pl.when(t == pl.num_programs(1) - 1)
    def _():
        update(kn_ref[...], vn_ref[...], lambda h: bn_ref[h])
        lam = _lambda_value(lq1[...], lk1[...], lq2[...], lk2[...], lam_init)
        for h in range(N_HEADS_A):
            y = _diff_finish(acc_ref[2 * h], l_ref[2 * h], acc_ref[2 * h + 1], l_ref[2 * h + 1], lam,
                             g_ref[...], lam_init)
            o_ref[:, 2 * h * d:2 * (h + 1) * d] = y.astype(BF16)


def sample_attention(qn, kn, vn, row0, cache_k, cache_v, bias_past, bias_new, lams, subln_g, lam_init, tk=512):
    nb, past, w = cache_k.shape
    ts = bias_new.shape[1]
    tk = min(tk, past)
    blk0 = row0 // ts
    hw = 2 * HEAD_DIM_A
    vec = pl.BlockSpec((1, HEAD_DIM_A), lambda b, t: (0, 0))
    new_rows = pl.BlockSpec((ts, w), lambda b, t: (blk0 + b, 0))
    return pl.pallas_call(
        functools.partial(_sample_attn_kernel, lam_init=lam_init),
        out_shape=jax.ShapeDtypeStruct((nb * ts, w), BF16),
        grid=(nb, past // tk),
        in_specs=[new_rows,
                  pl.BlockSpec((1, tk, w), lambda b, t: (b, t, 0)),
                  pl.BlockSpec((1, tk, w), lambda b, t: (b, t, 0)),
                  new_rows, new_rows,
                  pl.BlockSpec((N_HEADS_A, ts, tk), lambda b, t: (0, 0, t)),
                  pl.BlockSpec((N_HEADS_A, ts, ts), lambda b, t: (0, 0, 0)),
                  vec, vec, vec, vec,
                  pl.BlockSpec((1, hw), lambda b, t: (0, 0))],
        out_specs=pl.BlockSpec((ts, w), lambda b, t: (b, 0)),
        scratch_shapes=[pltpu.VMEM((2 * N_HEADS_A, ts, 1), F32), pltpu.VMEM((2 * N_HEADS_A, ts, 1), F32),
                        pltpu.VMEM((2 * N_HEADS_A, ts, hw), F32)],
        compiler_params=_cparams(("arbitrary", "arbitrary")),
        name="sample_attention",
    )(qn, cache_k, cache_v, kn, vn, bias_past, bias_new, *lams, subln_g.reshape(1, hw))


def _group_allreduce(x):
    x = x + pltpu.roll(x, N_HEADS_B, axis=1)
    return x + pltpu.roll(x, 2 * N_HEADS_B, axis=1)


def _head_sum128(x):
    acc = x[:, 0:LANES]
    for c in range(1, K_HI):
        acc = acc + x[:, c * LANES:(c + 1) * LANES]
    return _group_allreduce(acc)


def _tile16(x128):
    return jnp.concatenate([x128] * K_HI, axis=1)


def _rwkv_prep_kernel(f_ref, pf_ref, lo_ref, plo_ref, mu_ref, mul_ref, w0_ref, a0_ref, kk_ref, ka_ref, rk_ref,
                      w2_ref, a2_ref, g2_ref,
                      ak_o, wr_o, w_o, b_o, km_o, v_o, g_o, br_o, kr_o, bo_o):
    c = RWKV_WIDTH
    f = f_ref[...]
    xm = f + (pf_ref[...] - f) * mu_ref[...]
    lo = lo_ref[...]
    xl = lo + (plo_ref[...] - lo) * mul_ref[...]
    r, k, v = xm[:, :c], xm[:, c:2 * c], xm[:, 2 * c:]
    wd, ad, gd = xl[:, :LANES], xl[:, LANES:2 * LANES], xl[:, 2 * LANES:]
    lw = w0_ref[...] + jnp.dot(jnp.tanh(wd).astype(BF16), w2_ref[...].astype(BF16), preferred_element_type=F32)
    z = -lw
    softplus = jnp.maximum(z, 0.0) + jnp.log(1.0 + jnp.exp(-jnp.abs(z)))
    decay = jnp.exp(-jnp.exp(-softplus - 0.5))
    a = _sigmoid(a0_ref[...] + jnp.dot(ad.astype(BF16), a2_ref[...].astype(BF16), preferred_element_type=F32))
    g_o[...] = jnp.dot(_sigmoid(gd).astype(BF16), g2_ref[...].astype(BF16), preferred_element_type=F32)
    kk = k * kk_ref[...]
    norm = jnp.maximum(jnp.sqrt(_tile16(_head_sum128(kk * kk))), 1e-12)
    kk = kk / norm
    kmod = k * (1.0 + (a - 1.0) * ka_ref[...])
    bvec = kk * a
    ak_o[...] = -kk
    wr_o[...] = decay * r
    w_o[...] = decay
    b_o[...] = bvec
    km_o[...] = kmod
    v_o[...] = v
    br_o[...] = _head_sum128(bvec * r)
    kr_o[...] = _head_sum128(kmod * r)
    bo_o[...] = _head_sum128(r * kmod * rk_ref[...])


def rwkv_prep(feat, prev, lora, prev_lora, prm, tm=128):
    m = feat.shape[0]
    c = RWKV_WIDTH
    row = lambda w: pl.BlockSpec((tm, w), lambda i: (i, 0))
    vec = lambda w: pl.BlockSpec((1, w), lambda i: (0, 0))
    mat = lambda r: pl.BlockSpec((r, c), lambda i: (0, 0))
    big = jax.ShapeDtypeStruct((m, c), F32)
    small = jax.ShapeDtypeStruct((m, LANES), F32)
    return pl.pallas_call(
        _rwkv_prep_kernel,
        out_shape=(big,) * 7 + (small,) * 3,
        grid=(_exact_div(m, tm),),
        in_specs=[row(3 * c), row(3 * c), row(LORA_COLS), row(LORA_COLS), vec(3 * c), vec(LORA_COLS),
                  vec(c), vec(c), vec(c), vec(c), vec(c), mat(LANES), mat(LANES), mat(RANK_G)],
        out_specs=(row(c),) * 7 + (row(LANES),) * 3,
        compiler_params=_cparams(("parallel",)),
        name="rwkv_prep",
    )(feat, prev, lora, prev_lora, prm["mu_rkv"], prm["mu_lora"], prm["w0"], prm["a0"], prm["k_k"], prm["k_a"],
      prm["r_k"], prm["w2"], prm["a2"], prm["g2"])


def _scan_kernel(seq_ref, first_ref, last_ref, ak_ref, wr_ref, w_ref, b_ref, km_ref, v_ref, br_ref, s0_ref,
                 y_ref, sout_ref, s_ref):
    step = pl.program_id(0)

    @pl.when(first_ref[step] == 1)
    def _():
        s_ref[...] = s0_ref[0]

    sub = lax.broadcasted_iota(jnp.int32, (SUBLANES, LANES), 0)
    grp = lax.broadcasted_iota(jnp.int32, (SUBLANES, LANES), 1) // N_HEADS_B
    own_group = grp == (sub % K_LO)
    low_half = sub < K_LO

    def row(ref, t8, s, c):
        tile = ref[pl.ds(t8, SUBLANES), c * LANES:(c + 1) * LANES]
        return jnp.broadcast_to(tile[s:s + 1], (SUBLANES, LANES))

    def time_step(t8, s, y_lo, y_hi):
        vt = []
        for vb in range(V_BLK):
            tile = jnp.where(low_half, row(v_ref, t8, s, 2 * vb), row(v_ref, t8, s, 2 * vb + 1))
            vt.append(_group_allreduce(jnp.where(own_group, tile, 0.0)))
        acc_u = [jnp.zeros((SUBLANES, LANES), F32) for _ in range(V_BLK)]
        acc_y = [jnp.zeros((SUBLANES, LANES), F32) for _ in range(V_BLK)]
        for kh in range(K_HI):
            a_row = row(ak_ref, t8, s, kh)
            wr_row = row(wr_ref, t8, s, kh)
            for vb in range(V_BLK):
                st = s_ref[vb, kh]
                acc_u[vb] = acc_u[vb] + st * a_row
                acc_y[vb] = acc_y[vb] + st * wr_row
        br_row = row(br_ref, t8, s, 0)
        u = [_group_allreduce(x) for x in acc_u]
        for vb in range(V_BLK):
            y = _group_allreduce(acc_y[vb]) + u[vb] * br_row
            ym = jnp.where(own_group, y, 0.0)
            ym = ym + pltpu.roll(ym, 1, axis=0)
            ym = ym + pltpu.roll(ym, 2, axis=0)
            y_lo[vb] = jnp.where(sub == s, pltpu.roll(ym, (s - (K_LO - 1)) % SUBLANES, axis=0), y_lo[vb])
            y_hi[vb] = jnp.where(sub == s, pltpu.roll(ym, (s - (2 * K_LO - 1)) % SUBLANES, axis=0), y_hi[vb])
        for kh in range(K_HI):
            w_row = row(w_ref, t8, s, kh)
            b_row = row(b_ref, t8, s, kh)
            km_row = row(km_ref, t8, s, kh)
            for vb in range(V_BLK):
                s_ref[vb, kh] = s_ref[vb, kh] * w_row + (b_row * u[vb] + km_row * vt[vb])

    def eight_steps(gi, carry):
        t8 = pl.multiple_of(gi * SUBLANES, SUBLANES)
        y_lo = [jnp.zeros((SUBLANES, LANES), F32) for _ in range(V_BLK)]
        y_hi = [jnp.zeros((SUBLANES, LANES), F32) for _ in range(V_BLK)]
        for s in range(SUBLANES):
            time_step(t8, s, y_lo, y_hi)
        for vb in range(V_BLK):
            y_ref[pl.ds(t8, SUBLANES), (2 * vb) * LANES:(2 * vb + 1) * LANES] = y_lo[vb]
            y_ref[pl.ds(t8, SUBLANES), (2 * vb + 1) * LANES:(2 * vb + 2) * LANES] = y_hi[vb]
        return carry

    lax.fori_loop(0, y_ref.shape[0] // SUBLANES, eight_steps, 0)

    @pl.when(last_ref[step] == 1)
    def _():
        sout_ref[0] = s_ref[...]


def rwkv_scan(ops, br, s0, seq_of_step, first, last):
    m, c = ops[0].shape
    nseq = s0.shape[0]
    tb = SCAN_TB
    row = lambda w: pl.BlockSpec((tb, w), lambda i, sq, fi, la: (i, 0))
    st = pl.BlockSpec((1, V_BLK, K_HI, SUBLANES, LANES), lambda i, sq, fi, la: (sq[i], 0, 0, 0, 0))
    return pl.pallas_call(
        _scan_kernel,
        out_shape=(jax.ShapeDtypeStruct((m, c), F32), jax.ShapeDtypeStruct(s0.shape, F32)),
        grid_spec=pltpu.PrefetchScalarGridSpec(
            num_scalar_prefetch=3,
            grid=(m // tb,),
            in_specs=[row(c)] * 6 + [row(LANES), st],
            out_specs=(row(c), st),
            scratch_shapes=[pltpu.VMEM((V_BLK, K_HI, SUBLANES, LANES), F32)]),
        compiler_params=_cparams(("arbitrary",)),
        name="rwkv_scan",
    )(seq_of_step, first, last, *ops, br, s0)


def _rwkv_post_kernel(y_ref, v_ref, g_ref, kr_ref, bo_ref, lg_ref, lb_ref, o_ref):
    v = v_ref[...]
    y = y_ref[...] + v * _tile16(kr_ref[...])
    mean = _tile16(_head_sum128(y)) * (1.0 / HEAD_B)
    yc = y - mean
    var = _tile16(_head_sum128(yc * yc)) * (1.0 / HEAD_B)
    yn = yc * lax.rsqrt(var + LNX_EPS) * lg_ref[...] + lb_ref[...]
    o_ref[...] = ((yn + _tile16(bo_ref[...]) * v) * g_ref[...]).astype(BF16)


def rwkv_post(y, v, g, kr, bo, lnx_g, lnx_b, tm=256):
    m, c = y.shape
    row = lambda w: pl.BlockSpec((tm, w), lambda i: (i, 0))
    vec = pl.BlockSpec((1, c), lambda i: (0, 0))
    return pl.pallas_call(
        _rwkv_post_kernel,
        out_shape=jax.ShapeDtypeStruct((m, c), BF16),
        grid=(_exact_div(m, tm),),
        in_specs=[row(c), row(c), row(c), row(LANES), row(LANES), vec, vec],
        out_specs=row(c),
        compiler_params=_cparams(("parallel",)),
        name="rwkv_post",
    )(y, v, g, kr, bo, lnx_g, lnx_b)


def _router_kernel(x_ref, g_ref, rw_ref, rb_ref, h_ref, route_ref):
    x = x_ref[...]
    ms = jnp.mean(x * x, axis=-1, keepdims=True)
    h = x * lax.rsqrt(ms + RMS_EPS) * g_ref[...]
    h_ref[...] = h
    logits = jnp.dot(h, rw_ref[...], precision=lax.Precision.HIGHEST, preferred_element_type=F32) + rb_ref[...]
    lane = lax.broadcasted_iota(jnp.int32, logits.shape, 1)
    big = jnp.int32(LANES)

    def first_argmax(vals, valid):
        masked = jnp.where(valid, vals, -jnp.inf)
        mx = jnp.max(masked, axis=-1, keepdims=True)
        idx = jnp.min(jnp.where(valid & (masked == mx), lane, big), axis=-1, keepdims=True)
        return mx, idx

    is_group = lane < N_GROUPS
    g_max, g_idx = first_argmax(logits, is_group)
    g_top = 1.0 / jnp.sum(jnp.where(is_group, jnp.exp(logits - g_max), 0.0), axis=-1, keepdims=True)
    in_group = (lane >= N_GROUPS) & (lane < N_GROUPS + N_EXPERTS) & ((lane - N_GROUPS) // EXP_PER_GROUP == g_idx)
    i_max, idx1 = first_argmax(logits, in_group)
    z = jnp.sum(jnp.where(in_group, jnp.exp(logits - i_max), 0.0), axis=-1, keepdims=True)
    i_max2, idx2 = first_argmax(logits, in_group & (lane != idx1))
    p1 = 1.0 / z
    p2 = jnp.exp(i_max2 - i_max) / z
    psum = p1 + p2
    gate1 = g_top * p1 / psum
    gate2 = g_top * p2 / psum
    route = jnp.where(lane == 0, (idx1 - N_GROUPS).astype(F32),
                      jnp.where(lane == 1, (idx2 - N_GROUPS).astype(F32),
                                jnp.where(lane == 2, gate1, jnp.where(lane == 3, gate2, 0.0))))
    route_ref[...] = route


def norm_router(x, g, rw, rb, tm=256):
    m, d = x.shape
    return pl.pallas_call(
        _router_kernel,
        out_shape=(jax.ShapeDtypeStruct((m, d), F32), jax.ShapeDtypeStruct((m, LANES), F32)),
        grid=(_exact_div(m, tm),),
        in_specs=[pl.BlockSpec((tm, d), lambda i: (i, 0)), pl.BlockSpec((1, d), lambda i: (0, 0)),
                  pl.BlockSpec((d, LANES), lambda i: (0, 0)), pl.BlockSpec((1, LANES), lambda i: (0, 0))],
        out_specs=(pl.BlockSpec((tm, d), lambda i: (i, 0)), pl.BlockSpec((tm, LANES), lambda i: (i, 0))),
        compiler_params=_cparams(("parallel",)),
        name="norm_router",
    )(x, g.reshape(1, d), rw, rb)


def _row_copy(src_hbm, buf, sem, slot, src_row, dst_row):
    return pltpu.make_async_copy(src_hbm.at[pl.ds(src_row, 1)], buf.at[slot, pl.ds(dst_row, 1)], sem.at[slot])


def _gather_rows_kernel(idx_ref, nused_ref, src_hbm, o_ref, buf, sem, *, rows):
    b = pl.program_id(0)
    nb = pl.num_programs(0)
    used = nused_ref[0]

    def issue(blk, slot):
        def body(r, c):
            _row_copy(src_hbm, buf, sem, slot, idx_ref[blk * rows + r], r).start()
            return c
        lax.fori_loop(0, rows, body, 0)

    def wait(slot):
        def body(r, c):
            _row_copy(src_hbm, buf, sem, slot, 0, r).wait()
            return c
        lax.fori_loop(0, rows, body, 0)

    @pl.when((b == 0) & (used > 0))
    def _():
        issue(0, 0)

    @pl.when((b + 1 < nb) & (b + 1 < used))
    def _():
        issue(b + 1, (b + 1) % 2)

    @pl.when(b < used)
    def _():
        wait(b % 2)
        o_ref[...] = buf[b % 2].astype(o_ref.dtype)

    @pl.when(b >= used)
    def _():
        o_ref[...] = jnp.zeros(o_ref.shape, o_ref.dtype)


def gather_rows(src, idx, n_used, rows, out_dtype):
    n = idx.shape[0]
    d = src.shape[1]
    return pl.pallas_call(
        functools.partial(_gather_rows_kernel, rows=rows),
        out_shape=jax.ShapeDtypeStruct((n, d), out_dtype),
        grid_spec=pltpu.PrefetchScalarGridSpec(
            num_scalar_prefetch=2,
            grid=(n // rows,),
            in_specs=[pl.BlockSpec(memory_space=pl.ANY)],
            out_specs=pl.BlockSpec((rows, d), lambda b, idx, nu: (b, 0)),
            scratch_shapes=[pltpu.VMEM((2, rows, d), src.dtype), pltpu.SemaphoreType.DMA((2,))]),
        compiler_params=_cparams(("arbitrary",)),
        name="gather_rows",
    )(idx, n_used, src)


def _combine_kernel(idx_ref, src_hbm, x_ref, route_ref, o_ref, buf, sem, *, rows):
    b = pl.program_id(0)
    nb = pl.num_programs(0)

    def issue(blk, slot):
        def body(r, c):
            base = (blk * rows + r) * TOP_K
            _row_copy(src_hbm, buf, sem, slot, idx_ref[base], r).start()
            _row_copy(src_hbm, buf, sem, slot, idx_ref[base + 1], rows + r).start()
            return c
        lax.fori_loop(0, rows, body, 0)

    def wait(slot):
        def body(r, c):
            _row_copy(src_hbm, buf, sem, slot, 0, r).wait()
            return c
        lax.fori_loop(0, TOP_K * rows, body, 0)

    @pl.when(b == 0)
    def _():
        issue(0, 0)

    @pl.when(b + 1 < nb)
    def _():
        issue(b + 1, (b + 1) % 2)

    slot = b % 2
    wait(slot)
    route = route_ref[...]
    g1 = route[:, 2:3]
    g2 = route[:, 3:4]
    o_ref[...] = x_ref[...] + (g1 * buf[slot, pl.ds(0, rows), :] + g2 * buf[slot, pl.ds(rows, rows), :])


def moe_combine(yb, dest, x, route, rows=128):
    m, d = x.shape
    return pl.pallas_call(
        functools.partial(_combine_kernel, rows=rows),
        out_shape=jax.ShapeDtypeStruct((m, d), F32),
        grid_spec=pltpu.PrefetchScalarGridSpec(
            num_scalar_prefetch=1,
            grid=(_exact_div(m, rows),),
            in_specs=[pl.BlockSpec(memory_space=pl.ANY),
                      pl.BlockSpec((rows, d), lambda b, idx: (b, 0)),
                      pl.BlockSpec((rows, LANES), lambda b, idx: (b, 0))],
            out_specs=pl.BlockSpec((rows, d), lambda b, idx: (b, 0)),
            scratch_shapes=[pltpu.VMEM((2, TOP_K * rows, d), F32), pltpu.SemaphoreType.DMA((2,))]),
        compiler_params=_cparams(("arbitrary",)),
        name="moe_combine",
    )(dest, yb, x, route)


def _expert_changed(be_ref, b):
    return (b == 0) | (be_ref[b] != be_ref[jnp.maximum(b - 1, 0)])


def _expert_up_kernel(be_ref, nused_ref, x_ref, wg_ref, wu_ref, o_ref, wgb_ref, wub_ref):
    b = pl.program_id(1)

    @pl.when(b < nused_ref[0])
    def _():
        @pl.when(_expert_changed(be_ref, b))
        def _():
            wgb_ref[...] = wg_ref[0].astype(BF16)
            wub_ref[...] = wu_ref[0].astype(BF16)

        x = x_ref[...]
        g = jnp.dot(x, wgb_ref[...], preferred_element_type=F32)
        u = jnp.dot(x, wub_ref[...], preferred_element_type=F32)
        o_ref[...] = (g * _sigmoid(g) * u).astype(BF16)

    @pl.when(b >= nused_ref[0])
    def _():
        o_ref[...] = jnp.zeros(o_ref.shape, o_ref.dtype)


def expert_up(xg, block_e, n_used, wg, wu, tf=512):
    n, d = xg.shape
    de = wg.shape[2]
    bm = EXPERT_ROWS
    live = lambda b, nu: jnp.minimum(b, jnp.maximum(nu[0] - 1, 0))
    wspec = pl.BlockSpec((1, d, tf), lambda f, b, be, nu: (be[live(b, nu)], 0, f))
    return pl.pallas_call(
        _expert_up_kernel,
        out_shape=jax.ShapeDtypeStruct((n, de), BF16),
        grid_spec=pltpu.PrefetchScalarGridSpec(
            num_scalar_prefetch=2,
            grid=(de // tf, n // bm),
            in_specs=[pl.BlockSpec((bm, d), lambda f, b, be, nu: (live(b, nu), 0)), wspec, wspec],
            out_specs=pl.BlockSpec((bm, tf), lambda f, b, be, nu: (b, f)),
            scratch_shapes=[pltpu.VMEM((d, tf), BF16), pltpu.VMEM((d, tf), BF16)]),
        compiler_params=_cparams(("arbitrary", "arbitrary")),
        name="expert_up",
    )(block_e, n_used, xg, wg, wu)


def _expert_down_kernel(be_ref, nused_ref, h_ref, wd_ref, o_ref, wdb_ref):
    b = pl.program_id(1)

    @pl.when(b < nused_ref[0])
    def _():
        @pl.when(_expert_changed(be_ref, b))
        def _():
            wdb_ref[...] = wd_ref[0].astype(BF16)

        o_ref[...] = jnp.dot(h_ref[...], wdb_ref[...], preferred_element_type=F32)

    @pl.when(b >= nused_ref[0])
    def _():
        o_ref[...] = jnp.zeros(o_ref.shape, o_ref.dtype)


def expert_down(hmid, block_e, n_used, wd, tn=1024):
    n, de = hmid.shape
    d = wd.shape[2]
    bm = EXPERT_ROWS
    live = lambda b, nu: jnp.minimum(b, jnp.maximum(nu[0] - 1, 0))
    return pl.pallas_call(
        _expert_down_kernel,
        out_shape=jax.ShapeDtypeStruct((n, d), F32),
        grid_spec=pltpu.PrefetchScalarGridSpec(
            num_scalar_prefetch=2,
            grid=(d // tn, n // bm),
            in_specs=[pl.BlockSpec((bm, de), lambda c, b, be, nu: (live(b, nu), 0)),
                      pl.BlockSpec((1, de, tn), lambda c, b, be, nu: (be[live(b, nu)], 0, c))],
            out_specs=pl.BlockSpec((bm, tn), lambda c, b, be, nu: (b, c)),
            scratch_shapes=[pltpu.VMEM((de, tn), BF16)]),
        compiler_params=_cparams(("arbitrary", "arbitrary")),
        name="expert_down",
    )(block_e, n_used, hmid, wd)


def _perm_cols(x):
    pre = x.shape[:-1]
    return jnp.moveaxis(x.reshape(pre + (N_HEADS_B, K_HI, K_LO)), -3, -1).reshape(pre + (RWKV_WIDTH,))


def _unperm_cols(x):
    pre = x.shape[:-1]
    return jnp.moveaxis(x.reshape(pre + (K_HI, K_LO, N_HEADS_B)), -1, -3).reshape(pre + (RWKV_WIDTH,))


def _perm_rows(x):
    return _perm_cols(x.T).T


def _pack_lora_cols(x):
    pad = jnp.zeros(x.shape[:-1] + (LANES - RANK_W,), x.dtype)
    return jnp.concatenate([x[..., :RANK_W], pad, x[..., RANK_W:RANK_W + RANK_A], pad, x[..., RANK_W + RANK_A:]], -1)


def _unpack_lora_cols(x):
    return jnp.concatenate([x[..., :RANK_W], x[..., LANES:LANES + RANK_A], x[..., 2 * LANES:]], -1)


def _state_to_tiles(s):
    n = s.shape[0]
    s = s.reshape(n, N_HEADS_B, V_BLK, SUBLANES, K_HI, K_LO)
    return s.transpose(0, 2, 4, 3, 5, 1).reshape(n, V_BLK, K_HI, SUBLANES, LANES)


def _tiles_to_state(s):
    n = s.shape[0]
    s = s.reshape(n, V_BLK, K_HI, SUBLANES, K_LO, N_HEADS_B)
    return s.transpose(0, 5, 1, 3, 2, 4).reshape(n, N_HEADS_B, HEAD_B, HEAD_B)


def _pad_rows(w, rows):
    return jnp.concatenate([w, jnp.zeros((rows - w.shape[0],) + w.shape[1:], w.dtype)], 0)


def _trunk_layer(xp, xs, pp, ps, cache_k, cache_v, state_wkv, state_shift, t5_table, lam_init, lp, attn_blk=512):
    t, d = xp.shape
    nb, ts, _ = xs.shape
    past = cache_k.shape[1]
    ms = nb * ts
    m = t + ms
    aw = ATT_WIDTH
    c = RWKV_WIDTH
    assert ts == SCAN_TB and t % SCAN_TB == 0 and past % CHUNK == 0 and ts <= CHUNK

    x = jnp.concatenate([xp, xs.reshape(ms, d)], 0)
    pe = jnp.concatenate([pp, ps.reshape(ms, -1)], 0).astype(BF16)

    w_in = lp["w_in"]
    rkv0 = 3 * aw
    w_rkv = jnp.concatenate([_perm_cols(w_in[:, rkv0 + i * c:rkv0 + (i + 1) * c]) for i in range(3)], 1)
    w_lora = _pack_lora_cols(w_in[:, rkv0 + 3 * c:])
    h1 = rmsnorm_cast(x, lp["norm1_g"])
    proj_qkv = matmul(h1, w_in, 3 * aw)
    feat = matmul(h1, w_rkv, 3 * c)
    lora = matmul(h1, w_lora, LORA_COLS)

    qn, k_new, kn, vn = qk_norm(proj_qkv, lp["q_norm_g"], lp["k_norm_g"])
    v_new = proj_qkv[:, 2 * aw:]
    lams = [lp[n].reshape(1, HEAD_DIM_A) for n in ("lambda_q1", "lambda_k1", "lambda_q2", "lambda_k2")]
    blk = min(attn_blk, t)
    assert t % blk == 0 and blk >= T5_FAR
    bias_d = bias_tiles(t5_table, blk, blk, rel0=0, masked=True)
    bias_l = bias_tiles(t5_table, blk, blk, rel0=-blk)
    far_bias = t5_table[T5_BUCKETS // 2 - 1]
    ya_p = prompt_attention(qn, kn, vn, t, bias_d, bias_l, far_bias, lams, lp["subln_g"], lam_init, blk)
    bias_past = bias_tiles(t5_table, ts, past, rel0=-past)
    bias_new = bias_tiles(t5_table, ts, ts, rel0=0)
    ya_s = sample_attention(qn, kn, vn, t, cache_k.reshape(nb, past, aw), cache_v.reshape(nb, past, aw),
                            bias_past, bias_new, lams, lp["subln_g"], lam_init)

    shift_rkv = jnp.concatenate([_perm_cols(state_shift[:, 0, i * c:(i + 1) * c]) for i in range(3)], 1)
    shift_lora = _pack_lora_cols(state_shift[:, 0, 3 * c:])
    seq_rows = t + ts * jnp.arange(nb)
    prev = jnp.roll(feat, 1, axis=0).at[0].set(0.0).at[seq_rows].set(shift_rkv)
    prev_lora = jnp.roll(lora, 1, axis=0).at[0].set(0.0).at[seq_rows].set(shift_lora)
    vec = lambda v: v.reshape(1, -1)
    mu = lp["rwkv_mu"]
    prm = dict(
        mu_rkv=vec(jnp.concatenate([_perm_cols(mu[i * c:(i + 1) * c]) for i in range(3)])),
        mu_lora=vec(_pack_lora_cols(mu[3 * c:])),
        w0=vec(_perm_cols(lp["rwkv_w0"])), a0=vec(_perm_cols(lp["rwkv_a0"])),
        k_k=vec(_perm_cols(lp["rwkv_k_k"])), k_a=vec(_perm_cols(lp["rwkv_k_a"])),
        r_k=vec(_perm_cols(lp["rwkv_r_k"].reshape(-1))),
        w2=_pad_rows(_perm_cols(lp["rwkv_w2"]), LANES), a2=_pad_rows(_perm_cols(lp["rwkv_a2"]), LANES),
        g2=_perm_cols(lp["rwkv_g2"]))
    ak, wr, wdec, bvec, km, vv, gate, br, kr, bonus = rwkv_prep(feat, prev, lora, prev_lora, prm)
    n_pstep = t // SCAN_TB
    seq_of_step = jnp.concatenate([jnp.zeros((n_pstep,), jnp.int32), 1 + jnp.arange(nb, dtype=jnp.int32)])
    first = jnp.concatenate([jnp.zeros((n_pstep,), jnp.int32).at[0].set(1), jnp.ones((nb,), jnp.int32)])
    last = jnp.concatenate([jnp.zeros((n_pstep,), jnp.int32).at[-1].set(1), jnp.ones((nb,), jnp.int32)])
    s0 = jnp.concatenate([jnp.zeros((1, V_BLK, K_HI, SUBLANES, LANES), F32),
                          _state_to_tiles(state_wkv.astype(F32))], 0)
    y_scan, s_fin = rwkv_scan((ak, wr, wdec, bvec, km, vv), br, s0, seq_of_step, first, last)
    yb = rwkv_post(y_scan, vv, gate, kr, bonus, vec(_perm_cols(lp["lnx_g"])), vec(_perm_cols(lp["lnx_b"])))
    wkv_fin = _tiles_to_state(s_fin)

    def shift_out(rows):
        return jnp.concatenate([_unperm_cols(feat[rows, i * c:(i + 1) * c]) for i in range(3)]
                               + [_unpack_lora_cols(lora[rows])], -1)

    shift_p = shift_out(jnp.array([t - 1]))
    shift_s = shift_out(seq_rows + ts - 1)

    y_mix = jnp.concatenate([jnp.concatenate([ya_p, ya_s], 0), yb], 1)
    w_out = jnp.concatenate([lp["w_out"][:aw], _perm_rows(lp["w_out"][aw:])], 0)
    x1 = matmul(y_mix, w_out, d, mode="residual", res=x)

    rw = jnp.concatenate([lp["rg_w"], lp["ri_w"], jnp.zeros((d, LANES - N_GROUPS - N_EXPERTS), F32)], 1)
    rb = jnp.concatenate([lp["rg_b"], lp["ri_b"].reshape(-1), jnp.zeros((LANES - N_GROUPS - N_EXPERTS,), F32)])
    h2, route = norm_router(x1, lp["norm2_g"], rw, rb.reshape(1, LANES))
    n_assign = m * TOP_K
    flat_e = route[:, :TOP_K].astype(jnp.int32).reshape(n_assign)
    bm = EXPERT_ROWS
    onehot = (flat_e[:, None] == jnp.arange(N_EXPERTS, dtype=jnp.int32)[None, :]).astype(jnp.int32)
    running = jnp.cumsum(onehot, axis=0)
    counts = running[-1]
    rank = jnp.take_along_axis(running, flat_e[:, None], axis=1)[:, 0] - 1
    pcounts = (counts + bm - 1) // bm * bm
    pend = jnp.cumsum(pcounts)
    dest = (pend - pcounts)[flat_e] + rank
    n_blocks = n_assign // bm + N_EXPERTS
    rows_tok = jnp.zeros((n_blocks * bm,), jnp.int32).at[dest].set(jnp.arange(n_assign, dtype=jnp.int32) // TOP_K)
    block_e = jnp.minimum(jnp.searchsorted(pend, jnp.arange(n_blocks, dtype=jnp.int32) * bm, side="right"),
                          N_EXPERTS - 1).astype(jnp.int32)
    n_used = (pend[-1] // bm).astype(jnp.int32).reshape(1)
    xg = gather_rows(h2, rows_tok, n_used, bm, BF16)
    hmid = expert_up(xg, block_e, n_used, lp["e_wg"], lp["e_wu"])
    yexp = expert_down(hmid, block_e, n_used, lp["e_wd"])
    x2 = moe_combine(yexp, dest.astype(jnp.int32), x1, route)

    h3 = rmsnorm_cast(x2, lp["ple_norm_g"])
    x3 = matmul(h3, lp["ple_gate_w"], d, mode="ple", res=x2, p=pe, pw=lp["ple_proj_w"])

    return (x3[:t], x3[t:].reshape(nb, ts, d), k_new, v_new, wkv_fin, shift_p, shift_s)


def kernel(x_prompt, x_sample, p_prompt, p_sample, cache_k, cache_v, state_wkv, state_shift, t5_table, norm1_g, w_in, q_norm_g, k_norm_g, lambda_q1, lambda_k1, lambda_q2, lambda_k2, subln_g, rwkv_mu, rwkv_w0, rwkv_w2, rwkv_a0, rwkv_a2, rwkv_g2, rwkv_k_k, rwkv_k_a, rwkv_r_k, lnx_g, lnx_b, w_out, norm2_g, router_group_w, router_group_b, router_inner_w, router_inner_b, expert_w_gate, expert_w_up, expert_w_down, ple_norm_g, ple_gate_w, ple_proj_w):
    depth = w_in.shape[0]
    bp, t, d = x_prompt.shape
    nb, ts, _ = x_sample.shape
    assert depth == 1 and bp == 1, "one layer and one prompt stream are fused with the sample batch"
    i = 0
    lp = dict(norm1_g=norm1_g[i], w_in=w_in[i], q_norm_g=q_norm_g[i], k_norm_g=k_norm_g[i],
              lambda_q1=lambda_q1[i], lambda_k1=lambda_k1[i], lambda_q2=lambda_q2[i], lambda_k2=lambda_k2[i],
              subln_g=subln_g[i], rwkv_mu=rwkv_mu[i], rwkv_w0=rwkv_w0[i], rwkv_w2=rwkv_w2[i],
              rwkv_a0=rwkv_a0[i], rwkv_a2=rwkv_a2[i], rwkv_g2=rwkv_g2[i], rwkv_k_k=rwkv_k_k[i],
              rwkv_k_a=rwkv_k_a[i], rwkv_r_k=rwkv_r_k[i], lnx_g=lnx_g[i], lnx_b=lnx_b[i], w_out=w_out[i],
              norm2_g=norm2_g[i], rg_w=router_group_w[i], rg_b=router_group_b[i], ri_w=router_inner_w[i],
              ri_b=router_inner_b[i], e_wg=expert_w_gate[i], e_wu=expert_w_up[i], e_wd=expert_w_down[i],
              ple_norm_g=ple_norm_g[i], ple_gate_w=ple_gate_w[i], ple_proj_w=ple_proj_w[i])
    lam_init = 0.8 - 0.6 * math.exp(-0.3 * i)
    yp, ys, k_new, v_new, wkv_fin, shift_p, shift_s = _trunk_layer(
        x_prompt[0], x_sample, p_prompt[i, 0], p_sample[i], cache_k[i], cache_v[i], state_wkv[i],
        state_shift[i], t5_table, lam_init, lp)
    hk = (N_HEADS_A, 2, HEAD_DIM_A)
    hv = (N_HEADS_A, 2 * HEAD_DIM_A)
    return (yp[None], ys,
            k_new[:t].reshape((1, 1, t) + hk), v_new[:t].reshape((1, 1, t) + hv),
            wkv_fin[:1][None], shift_p.reshape(1, 1, 1, -1),
            k_new[t:].reshape((1, nb, ts) + hk), v_new[t:].reshape((1, nb, ts) + hv),
            wkv_fin[1:][None], shift_s.reshape(1, nb, 1, -1))
```

```python
import functools
import math

import jax
import jax.numpy as jnp
from jax import lax
from jax.experimental import pallas as pl
from jax.experimental.pallas import tpu as pltpu

F32 = jnp.float32
BF16 = jnp.bfloat16

LANES = 128
SUBLANES = 8
VMEM_BYTES_V7X = 64 * 1024 * 1024
VMEM_LIMIT = VMEM_BYTES_V7X - 6 * 1024 * 1024

CHUNK = 64
HEAD_DIM_A = 128
N_HEADS_A = 8
ATT_WIDTH = 2 * HEAD_DIM_A * N_HEADS_A
T5_BUCKETS = 32
HEAD_B = 64
N_HEADS_B = 32
RWKV_WIDTH = HEAD_B * N_HEADS_B
RANK_W = 96
RANK_A = 96
RANK_G = 256
LORA_COLS = 512
LNX_EPS = 64e-5
N_GROUPS = 8
EXP_PER_GROUP = 8
N_EXPERTS = N_GROUPS * EXP_PER_GROUP
TOP_K = 2
RMS_EPS = 1e-6
NEG_INF = -1e30
T5_LOG_THRESHOLDS = (12, 16, 23, 32, 46, 64, 91)
T5_FAR = 128

K_LO = LANES // N_HEADS_B
K_HI = HEAD_B // K_LO
V_BLK = HEAD_B // SUBLANES
SCAN_TB = 32

EXPERT_ROWS = 256


def _cparams(sem, vmem=VMEM_LIMIT):
    return pltpu.CompilerParams(dimension_semantics=sem, vmem_limit_bytes=vmem)


def _exact_div(a, b):
    assert a % b == 0, (a, b)
    return a // b


def _sigmoid(x):
    return 1.0 / (1.0 + jnp.exp(-x))


def _rmsnorm_kernel(x_ref, g_ref, o_ref):
    x = x_ref[...]
    ms = jnp.mean(x * x, axis=-1, keepdims=True)
    o_ref[...] = (x * lax.rsqrt(ms + RMS_EPS) * g_ref[...]).astype(o_ref.dtype)


def rmsnorm_cast(x, g, tm=256):
    m, d = x.shape
    return pl.pallas_call(
        _rmsnorm_kernel,
        out_shape=jax.ShapeDtypeStruct((m, d), BF16),
        grid=(_exact_div(m, tm),),
        in_specs=[pl.BlockSpec((tm, d), lambda i: (i, 0)), pl.BlockSpec((1, d), lambda i: (0, 0))],
        out_specs=pl.BlockSpec((tm, d), lambda i: (i, 0)),
        compiler_params=_cparams(("parallel",)),
        name="rmsnorm_cast",
    )(x, g.reshape(1, d))


def _mm_kernel(a_ref, w_ref, *rest, mode):
    if mode == "plain":
        o_ref, wb_ref = rest
    elif mode == "residual":
        res_ref, o_ref, wb_ref = rest
    else:
        res_ref, p_ref, pw_ref, o_ref, wb_ref = rest

    @pl.when(pl.program_id(1) == 0)
    def _():
        wb_ref[...] = w_ref[...].astype(BF16)

    acc = jnp.dot(a_ref[...], wb_ref[...], preferred_element_type=F32)
    if mode == "plain":
        o_ref[...] = acc
    elif mode == "residual":
        o_ref[...] = res_ref[...] + acc
    else:
        pe = jnp.dot(p_ref[...], pw_ref[...].astype(BF16), preferred_element_type=F32)
        o_ref[...] = res_ref[...] + pe * _sigmoid(acc)


def matmul(a, w, n_cols, *, col_block0=0, mode="plain", res=None, p=None, pw=None, tm=1024, tn=512):
    m, k = a.shape
    tm = min(tm, m)
    assert m % tm == 0 and n_cols % tn == 0
    in_specs = [pl.BlockSpec((tm, k), lambda n, i: (i, 0)),
                pl.BlockSpec((k, tn), lambda n, i: (0, n + col_block0))]
    args = [a, w]
    if mode in ("residual", "ple"):
        in_specs.append(pl.BlockSpec((tm, tn), lambda n, i: (i, n)))
        args.append(res)
    if mode == "ple":
        kp = p.shape[1]
        in_specs += [pl.BlockSpec((tm, kp), lambda n, i: (i, 0)), pl.BlockSpec((kp, tn), lambda n, i: (0, n))]
        args += [p, pw]
    return pl.pallas_call(
        functools.partial(_mm_kernel, mode=mode),
        out_shape=jax.ShapeDtypeStruct((m, n_cols), F32),
        grid=(n_cols // tn, m // tm),
        in_specs=in_specs,
        out_specs=pl.BlockSpec((tm, tn), lambda n, i: (i, n)),
        scratch_shapes=[pltpu.VMEM((k, tn), BF16)],
        compiler_params=_cparams(("arbitrary", "arbitrary")),
        name="matmul_" + mode,
    )(*args)


def _qk_norm_kernel(q_ref, k_ref, v_ref, qg_ref, kg_ref, qo_ref, ko_ref, kbo_ref, vo_ref):
    def head_norm(x, g):
        ms = jnp.mean(x * x, axis=-1, keepdims=True)
        return x * lax.rsqrt(ms + RMS_EPS) * g

    qg = qg_ref[...]
    kg = kg_ref[...]
    for c in range(ATT_WIDTH // HEAD_DIM_A):
        sl = slice(c * HEAD_DIM_A, (c + 1) * HEAD_DIM_A)
        qn = head_norm(q_ref[:, sl], qg)
        qo_ref[:, sl] = (qn * (LOG2E * HEAD_DIM_A ** -0.5)).astype(BF16)
        kn = head_norm(k_ref[:, sl], kg)
        ko_ref[:, sl] = kn
        kbo_ref[:, sl] = kn.astype(BF16)
    vo_ref[...] = v_ref[...].astype(BF16)


def qk_norm(proj_qkv, q_g, k_g, tm=256):
    m = proj_qkv.shape[0]
    w = ATT_WIDTH
    blk = lambda c: pl.BlockSpec((tm, w), lambda i, c=c: (i, c))
    vec = pl.BlockSpec((1, HEAD_DIM_A), lambda i: (0, 0))
    out_blk = pl.BlockSpec((tm, w), lambda i: (i, 0))
    return pl.pallas_call(
        _qk_norm_kernel,
        out_shape=(jax.ShapeDtypeStruct((m, w), BF16), jax.ShapeDtypeStruct((m, w), F32),
                   jax.ShapeDtypeStruct((m, w), BF16), jax.ShapeDtypeStruct((m, w), BF16)),
        grid=(_exact_div(m, tm),),
        in_specs=[blk(0), blk(1), blk(2), vec, vec],
        out_specs=(out_blk, out_blk, out_blk, out_blk),
        compiler_params=_cparams(("parallel",)),
        name="qk_norm",
    )(proj_qkv, proj_qkv, proj_qkv, q_g.reshape(1, -1), k_g.reshape(1, -1))


def _bias_kernel(tab_ref, o_ref, *, rel0, masked, key_major):
    _, nr, nc = o_ref.shape
    r = lax.broadcasted_iota(jnp.int32, (nr, nc), 0)
    c = lax.broadcasted_iota(jnp.int32, (nr, nc), 1)
    kpos, qpos = (r, c) if key_major else (c, r)
    rel = rel0 + kpos - qpos
    n = jnp.abs(rel)
    large = jnp.full((nr, nc), T5_BUCKETS // 4, jnp.int32)
    for thr in T5_LOG_THRESHOLDS:
        large = large + jnp.where(n >= thr, 1, 0)
    bucket = jnp.where(n < T5_BUCKETS // 4, n, large) + jnp.where(rel > 0, T5_BUCKETS // 2, 0)
    if masked:
        visible = (kpos // CHUNK) <= (qpos // CHUNK)
    for h in range(N_HEADS_A):
        acc = jnp.zeros((nr, nc), F32)
        for b in range(T5_BUCKETS):
            acc = jnp.where(bucket == b, tab_ref[b, h] * LOG2E, acc)
        if masked:
            acc = jnp.where(visible, acc, NEG_INF)
        o_ref[h] = acc


def bias_tiles(table, nr, nc, *, rel0, masked=False, key_major=False):
    return pl.pallas_call(
        functools.partial(_bias_kernel, rel0=rel0, masked=masked, key_major=key_major),
        out_shape=jax.ShapeDtypeStruct((N_HEADS_A, nr, nc), F32),
        in_specs=[pl.BlockSpec(memory_space=pltpu.SMEM)],
        out_specs=pl.BlockSpec(memory_space=pltpu.VMEM),
        compiler_params=_cparams(None),
        name="t5_bias",
    )(table)


def _lambda_value(lq1, lk1, lq2, lk2, lam_init):
    s1 = jnp.sum(lq1 * lk1, axis=-1, keepdims=True)
    s2 = jnp.sum(lq2 * lk2, axis=-1, keepdims=True)
    return jnp.exp(s1) - jnp.exp(s2) + lam_init


def _online_update(s, bias, v, m_ref, l_ref, acc_ref, idx):
    nchunk = max(s.shape[1] // LANES, 1)
    width = s.shape[1] // nchunk
    sc = [s[:, c * width:(c + 1) * width] for c in range(nchunk)]
    shift = None
    if callable(bias):
        sc = [x + bias(c * width, width) for c, x in enumerate(sc)]
    else:
        shift = bias
    row_max = jnp.max(functools.reduce(jnp.maximum, sc), axis=-1, keepdims=True)
    if shift is not None:
        row_max = row_max + shift
    m_prev = m_ref[idx]
    m_new = jnp.maximum(m_prev, row_max)
    alpha = jnp.exp2(m_prev - m_new)
    offset = m_new if shift is None else m_new - shift
    p = [jnp.exp2(x - offset) for x in sc]
    row_sum = jnp.sum(functools.reduce(lambda a, b: a + b, p), axis=-1, keepdims=True)
    l_ref[idx] = alpha * l_ref[idx] + row_sum
    pb = p[0].astype(BF16) if nchunk == 1 else jnp.concatenate([x.astype(BF16) for x in p], axis=1)
    acc_ref[idx] = alpha * acc_ref[idx] + jnp.dot(pb, v, preferred_element_type=F32)
    m_ref[idx] = m_new


def _diff_finish(acc1, l1, acc2, l2, lam, g, lam_init):
    o = acc1 / l1 - lam * (acc2 / l2)
    ms = jnp.mean(o * o, axis=-1, keepdims=True)
    return (o * lax.rsqrt(ms + RMS_EPS) * g) * (1.0 - lam_init)


_NT = (((1,), (1,)), ((), ()))
BF16_SUBLANES = 16
VT_ROWS = 2 * HEAD_DIM_A + BF16_SUBLANES
LOG2E = math.log2(math.e)


def _prompt_attn_kernel(far_ref, q_ref, k_ref, vt_ref, bd_ref, bl_ref, lq1, lk1, lq2, lk2, g_ref,
                        o_ref, m_ref, acc_ref, *, blk, nsub, lam_init):
    h = pl.program_id(0)
    i = pl.program_id(1)
    m_ref[...] = jnp.full(m_ref.shape, NEG_INF, F32)
    acc_ref[...] = jnp.zeros(acc_ref.shape, F32)
    d = HEAD_DIM_A
    hw = 2 * d
    far_bias = far_ref[h]

    def chain(st, bias_tile, vt, idx):
        shift = far_bias if bias_tile is None else None
        if bias_tile is not None:
            st = st + bias_tile[0]
        col_max = jnp.max(st, axis=0, keepdims=True)
        if shift is not None:
            col_max = col_max + shift
        m_prev = m_ref[idx]
        m_new = jnp.maximum(m_prev, col_max)
        alpha = jnp.exp2(m_prev - m_new)
        p = jnp.exp2(st - (m_new if shift is None else m_new - shift))
        acc_ref[idx] = alpha * acc_ref[idx] + jnp.dot(vt, p.astype(BF16), preferred_element_type=F32)
        m_ref[idx] = m_new

    def update(j, ahead):
        kb = k_ref[pl.ds(pl.multiple_of(j * blk, blk), blk), :]
        vt = vt_ref[j]
        todo = []
        for sa in range(nsub):
            if ahead[sa] < 0:
                continue
            bias_tile = None if ahead[sa] >= 2 else (bl_ref if ahead[sa] == 1 else bd_ref)
            for mp in range(2):
                st = lax.dot_general(kb[:, mp * d:(mp + 1) * d], q_ref[sa * blk:(sa + 1) * blk, mp * d:(mp + 1) * d],
                                     _NT, preferred_element_type=F32)
                todo.append((st, bias_tile, 2 * sa + mp))
        for st, bias_tile, idx in todo:
            chain(st, bias_tile, vt, idx)

    def far_body(j, carry):
        update(j, [2] * nsub)
        return carry

    lax.fori_loop(0, jnp.maximum(nsub * i - 1, 0), far_body, 0)

    @pl.when(i >= 1)
    def _():
        update(nsub * i - 1, [sa + 1 for sa in range(nsub)])

    for o in range(nsub):
        update(nsub * i + o, [sa - o for sa in range(nsub)])
    lam = _lambda_value(lq1[...], lk1[...], lq2[...], lk2[...], lam_init)
    for sa in range(nsub):
        a1 = acc_ref[2 * sa]
        a2 = acc_ref[2 * sa + 1]
        ot = a1[:hw] / a1[hw:hw + 1] - lam * (a2[:hw] / a2[hw:hw + 1])
        ms = jnp.mean(ot * ot, axis=0, keepdims=True)
        yt = ot * lax.rsqrt(ms + RMS_EPS)
        o_ref[sa * blk:(sa + 1) * blk, :] = ((yt.T * g_ref[...]) * (1.0 - lam_init)).astype(BF16)


def prompt_attention(qn, kn, vt, t, bias_d, bias_l, far_bias, lams, subln_g, lam_init, blk, nsub=2):
    nsub = min(nsub, t // blk)
    bq = nsub * blk
    hw = 2 * HEAD_DIM_A
    vec = pl.BlockSpec((1, HEAD_DIM_A), lambda h, i, far: (0, 0))
    return pl.pallas_call(
        functools.partial(_prompt_attn_kernel, blk=blk, nsub=nsub, lam_init=lam_init),
        out_shape=jax.ShapeDtypeStruct((t, ATT_WIDTH), BF16),
        grid_spec=pltpu.PrefetchScalarGridSpec(
            num_scalar_prefetch=1,
            grid=(N_HEADS_A, _exact_div(t, bq)),
            in_specs=[pl.BlockSpec((bq, hw), lambda h, i, far: (i, h)),
                      pl.BlockSpec((t, hw), lambda h, i, far: (0, h)),
                      pl.BlockSpec((t // blk, VT_ROWS, blk), lambda h, i, far: (0, h, 0)),
                      pl.BlockSpec((1, blk, blk), lambda h, i, far: (h, 0, 0)),
                      pl.BlockSpec((1, blk, blk), lambda h, i, far: (h, 0, 0)),
                      vec, vec, vec, vec,
                      pl.BlockSpec((1, hw), lambda h, i, far: (0, 0))],
            out_specs=pl.BlockSpec((bq, hw), lambda h, i, far: (i, h)),
            scratch_shapes=[pltpu.VMEM((2 * nsub, 1, blk), F32), pltpu.VMEM((2 * nsub, VT_ROWS, blk), F32)]),
        compiler_params=_cparams(("arbitrary", "arbitrary")),
        name="prompt_attention",
    )(far_bias, qn, kn, vt, bias_d, bias_l, *lams, subln_g.reshape(1, hw))


def _sample_attn_kernel(q_ref, ck_ref, cv_ref, kn_ref, vn_ref, bp_ref, bn_ref, lq1, lk1, lq2, lk2, g_ref,
                        o_ref, m_ref, l_ref, acc_ref, *, lam_init):
    t = pl.program_id(1)
    d = HEAD_DIM_A

    @pl.when(t == 0)
    def _():
        m_ref[...] = jnp.full(m_ref.shape, NEG_INF, F32)
        l_ref[...] = jnp.zeros(l_ref.shape, F32)
        acc_ref[...] = jnp.zeros(acc_ref.shape, F32)

    def update(keys, values, bias_ref):
        for h in range(N_HEADS_A):
            vb = values(h)
            for mp in range(2):
                c0 = (2 * h + mp) * d
                s = lax.dot_general(q_ref[:, c0:c0 + d], keys(2 * h + mp), _NT, preferred_element_type=F32)
                _online_update(s, lambda k0, w, h=h: bias_ref[h, :, k0:k0 + w], vb, m_ref, l_ref, acc_ref,
                               2 * h + mp)

    update(lambda c: ck_ref[0, :, c, :].astype(BF16), lambda h: cv_ref[0, :, h, :].astype(BF16), bp_ref)

    @pl.when(t == pl.num_programs(1) - 1)
    def _():
        update(lambda c: kn_ref[:, c * d:(c + 1) * d], lambda h: vn_ref[:, 2 * h * d:2 * (h + 1) * d], bn_ref)
        lam = _lambda_value(lq1[...], lk1[...], lq2[...], lk2[...], lam_init)
        for h in range(N_HEADS_A):
            y = _diff_finish(acc_ref[2 * h], l_ref[2 * h], acc_ref[2 * h + 1], l_ref[2 * h + 1], lam,
                             g_ref[...], lam_init)
            o_ref[:, 2 * h * d:2 * (h + 1) * d] = y.astype(BF16)


def sample_attention(qn, kn, vn, row0, cache_k, cache_v, bias_past, bias_new, lams, subln_g, lam_init, tk=512):
    nb, past = cache_k.shape[:2]
    w = ATT_WIDTH
    ts = bias_new.shape[1]
    tk = min(tk, past)
    blk0 = row0 // ts
    hw = 2 * HEAD_DIM_A
    vec = pl.BlockSpec((1, HEAD_DIM_A), lambda b, t: (0, 0))
    new_rows = pl.BlockSpec((ts, w), lambda b, t: (blk0 + b, 0))
    return pl.pallas_call(
        functools.partial(_sample_attn_kernel, lam_init=lam_init),
        out_shape=jax.ShapeDtypeStruct((nb * ts, w), BF16),
        grid=(nb, past // tk),
        in_specs=[new_rows,
                  pl.BlockSpec((1, tk, 2 * N_HEADS_A, HEAD_DIM_A), lambda b, t: (b, t, 0, 0)),
                  pl.BlockSpec((1, tk, N_HEADS_A, hw), lambda b, t: (b, t, 0, 0)),
                  new_rows, new_rows,
                  pl.BlockSpec((N_HEADS_A, ts, tk), lambda b, t: (0, 0, t)),
                  pl.BlockSpec((N_HEADS_A, ts, ts), lambda b, t: (0, 0, 0)),
                  vec, vec, vec, vec,
                  pl.BlockSpec((1, hw), lambda b, t: (0, 0))],
        out_specs=pl.BlockSpec((ts, w), lambda b, t: (b, 0)),
        scratch_shapes=[pltpu.VMEM((2 * N_HEADS_A, ts, 1), F32), pltpu.VMEM((2 * N_HEADS_A, ts, 1), F32),
                        pltpu.VMEM((2 * N_HEADS_A, ts, hw), F32)],
        compiler_params=_cparams(("arbitrary", "arbitrary")),
        name="sample_attention",
    )(qn, cache_k, cache_v, kn, vn, bias_past, bias_new, *lams, subln_g.reshape(1, hw))


def _group_allreduce(x):
    x = x + pltpu.roll(x, N_HEADS_B, axis=1)
    return x + pltpu.roll(x, 2 * N_HEADS_B, axis=1)


def _head_sum128(x):
    acc = x[:, 0:LANES]
    for c in range(1, K_HI):
        acc = acc + x[:, c * LANES:(c + 1) * LANES]
    return _group_allreduce(acc)


def _tile16(x128):
    return jnp.concatenate([x128] * K_HI, axis=1)


def _rwkv_prep_kernel(f_ref, pf_ref, lo_ref, plo_ref, mu_ref, mul_ref, w0_ref, a0_ref, kk_ref, ka_ref, rk_ref,
                      w2_ref, a2_ref, g2_ref,
                      ak_o, wr_o, w_o, b_o, km_o, v_o, g_o, br_o, kr_o, bo_o):
    c = RWKV_WIDTH
    f = f_ref[...]
    xm = f + (pf_ref[...] - f) * mu_ref[...]
    lo = lo_ref[...]
    xl = lo + (plo_ref[...] - lo) * mul_ref[...]
    r, k, v = xm[:, :c], xm[:, c:2 * c], xm[:, 2 * c:]
    wd, ad, gd = xl[:, :LANES], xl[:, LANES:2 * LANES], xl[:, 2 * LANES:]
    lw = w0_ref[...] + jnp.dot(jnp.tanh(wd).astype(BF16), w2_ref[...].astype(BF16), preferred_element_type=F32)
    z = -lw
    softplus = jnp.maximum(z, 0.0) + jnp.log(1.0 + jnp.exp(-jnp.abs(z)))
    decay = jnp.exp(-jnp.exp(-softplus - 0.5))
    a = _sigmoid(a0_ref[...] + jnp.dot(ad.astype(BF16), a2_ref[...].astype(BF16), preferred_element_type=F32))
    g_o[...] = jnp.dot(_sigmoid(gd).astype(BF16), g2_ref[...].astype(BF16), preferred_element_type=F32)
    kk = k * kk_ref[...]
    norm = jnp.maximum(jnp.sqrt(_tile16(_head_sum128(kk * kk))), 1e-12)
    kk = kk / norm
    kmod = k * (1.0 + (a - 1.0) * ka_ref[...])
    bvec = kk * a
    ak_o[...] = -kk
    wr_o[...] = decay * r
    w_o[...] = decay
    b_o[...] = bvec
    km_o[...] = kmod
    v_o[...] = v
    br_o[...] = _head_sum128(bvec * r)
    kr_o[...] = _head_sum128(kmod * r)
    bo_o[...] = _head_sum128(r * kmod * rk_ref[...])


def rwkv_prep(feat, prev, lora, prev_lora, prm, tm=128):
    m = feat.shape[0]
    c = RWKV_WIDTH
    row = lambda w: pl.BlockSpec((tm, w), lambda i: (i, 0))
    vec = lambda w: pl.BlockSpec((1, w), lambda i: (0, 0))
    mat = lambda r: pl.BlockSpec((r, c), lambda i: (0, 0))
    big = jax.ShapeDtypeStruct((m, c), F32)
    small = jax.ShapeDtypeStruct((m, LANES), F32)
    return pl.pallas_call(
        _rwkv_prep_kernel,
        out_shape=(big,) * 7 + (small,) * 3,
        grid=(_exact_div(m, tm),),
        in_specs=[row(3 * c), row(3 * c), row(LORA_COLS), row(LORA_COLS), vec(3 * c), vec(LORA_COLS),
                  vec(c), vec(c), vec(c), vec(c), vec(c), mat(LANES), mat(LANES), mat(RANK_G)],
        out_specs=(row(c),) * 7 + (row(LANES),) * 3,
        compiler_params=_cparams(("parallel",)),
        name="rwkv_prep",
    )(feat, prev, lora, prev_lora, prm["mu_rkv"], prm["mu_lora"], prm["w0"], prm["a0"], prm["k_k"], prm["k_a"],
      prm["r_k"], prm["w2"], prm["a2"], prm["g2"])


def _scan_kernel(seq_ref, first_ref, last_ref, ak_ref, wr_ref, w_ref, b_ref, km_ref, v_ref, br_ref, s0_ref,
                 y_ref, sout_ref, s_ref):
    step = pl.program_id(0)

    @pl.when(first_ref[step] == 1)
    def _():
        s_ref[...] = s0_ref[0]

    sub = lax.broadcasted_iota(jnp.int32, (SUBLANES, LANES), 0)
    grp = lax.broadcasted_iota(jnp.int32, (SUBLANES, LANES), 1) // N_HEADS_B
    own_group = grp == (sub % K_LO)
    low_half = sub < K_LO

    def row(ref, t8, s, c):
        tile = ref[pl.ds(t8, SUBLANES), c * LANES:(c + 1) * LANES]
        return jnp.broadcast_to(tile[s:s + 1], (SUBLANES, LANES))

    def time_step(t8, s, y_lo, y_hi):
        vt = []
        for vb in range(V_BLK):
            tile = jnp.where(low_half, row(v_ref, t8, s, 2 * vb), row(v_ref, t8, s, 2 * vb + 1))
            vt.append(_group_allreduce(jnp.where(own_group, tile, 0.0)))
        acc_u = [jnp.zeros((SUBLANES, LANES), F32) for _ in range(V_BLK)]
        acc_y = [jnp.zeros((SUBLANES, LANES), F32) for _ in range(V_BLK)]
        for kh in range(K_HI):
            a_row = row(ak_ref, t8, s, kh)
            wr_row = row(wr_ref, t8, s, kh)
            for vb in range(V_BLK):
                st = s_ref[vb, kh]
                acc_u[vb] = acc_u[vb] + st * a_row
                acc_y[vb] = acc_y[vb] + st * wr_row
        br_row = row(br_ref, t8, s, 0)
        u = [_group_allreduce(x) for x in acc_u]
        for vb in range(V_BLK):
            y = _group_allreduce(acc_y[vb]) + u[vb] * br_row
            ym = jnp.where(own_group, y, 0.0)
            ym = ym + pltpu.roll(ym, 1, axis=0)
            ym = ym + pltpu.roll(ym, 2, axis=0)
            y_lo[vb] = jnp.where(sub == s, pltpu.roll(ym, (s - (K_LO - 1)) % SUBLANES, axis=0), y_lo[vb])
            y_hi[vb] = jnp.where(sub == s, pltpu.roll(ym, (s - (2 * K_LO - 1)) % SUBLANES, axis=0), y_hi[vb])
        for kh in range(K_HI):
            w_row = row(w_ref, t8, s, kh)
            b_row = row(b_ref, t8, s, kh)
            km_row = row(km_ref, t8, s, kh)
            for vb in range(V_BLK):
                s_ref[vb, kh] = s_ref[vb, kh] * w_row + (b_row * u[vb] + km_row * vt[vb])

    def eight_steps(gi, carry):
        t8 = pl.multiple_of(gi * SUBLANES, SUBLANES)
        y_lo = [jnp.zeros((SUBLANES, LANES), F32) for _ in range(V_BLK)]
        y_hi = [jnp.zeros((SUBLANES, LANES), F32) for _ in range(V_BLK)]
        for s in range(SUBLANES):
            time_step(t8, s, y_lo, y_hi)
        for vb in range(V_BLK):
            y_ref[pl.ds(t8, SUBLANES), (2 * vb) * LANES:(2 * vb + 1) * LANES] = y_lo[vb]
            y_ref[pl.ds(t8, SUBLANES), (2 * vb + 1) * LANES:(2 * vb + 2) * LANES] = y_hi[vb]
        return carry

    lax.fori_loop(0, y_ref.shape[0] // SUBLANES, eight_steps, 0)

    @pl.when(last_ref[step] == 1)
    def _():
        sout_ref[0] = s_ref[...]


def rwkv_scan(ops, br, s0, seq_of_step, first, last):
    m, c = ops[0].shape
    nseq = s0.shape[0]
    tb = SCAN_TB
    row = lambda w: pl.BlockSpec((tb, w), lambda i, sq, fi, la: (i, 0))
    st = pl.BlockSpec((1, V_BLK, K_HI, SUBLANES, LANES), lambda i, sq, fi, la: (sq[i], 0, 0, 0, 0))
    return pl.pallas_call(
        _scan_kernel,
        out_shape=(jax.ShapeDtypeStruct((m, c), F32), jax.ShapeDtypeStruct(s0.shape, F32)),
        grid_spec=pltpu.PrefetchScalarGridSpec(
            num_scalar_prefetch=3,
            grid=(m // tb,),
            in_specs=[row(c)] * 6 + [row(LANES), st],
            out_specs=(row(c), st),
            scratch_shapes=[pltpu.VMEM((V_BLK, K_HI, SUBLANES, LANES), F32)]),
        compiler_params=_cparams(("arbitrary",)),
        name="rwkv_scan",
    )(seq_of_step, first, last, *ops, br, s0)


def _rwkv_post_kernel(y_ref, v_ref, g_ref, kr_ref, bo_ref, lg_ref, lb_ref, o_ref):
    v = v_ref[...]
    y = y_ref[...] + v * _tile16(kr_ref[...])
    mean = _tile16(_head_sum128(y)) * (1.0 / HEAD_B)
    yc = y - mean
    var = _tile16(_head_sum128(yc * yc)) * (1.0 / HEAD_B)
    yn = yc * lax.rsqrt(var + LNX_EPS) * lg_ref[...] + lb_ref[...]
    o_ref[...] = ((yn + _tile16(bo_ref[...]) * v) * g_ref[...]).astype(BF16)


def rwkv_post(y, v, g, kr, bo, lnx_g, lnx_b, tm=256):
    m, c = y.shape
    row = lambda w: pl.BlockSpec((tm, w), lambda i: (i, 0))
    vec = pl.BlockSpec((1, c), lambda i: (0, 0))
    return pl.pallas_call(
        _rwkv_post_kernel,
        out_shape=jax.ShapeDtypeStruct((m, c), BF16),
        grid=(_exact_div(m, tm),),
        in_specs=[row(c), row(c), row(c), row(LANES), row(LANES), vec, vec],
        out_specs=row(c),
        compiler_params=_cparams(("parallel",)),
        name="rwkv_post",
    )(y, v, g, kr, bo, lnx_g, lnx_b)


def _router_kernel(x_ref, g_ref, rw_ref, rb_ref, h_ref, route_ref):
    x = x_ref[...]
    ms = jnp.mean(x * x, axis=-1, keepdims=True)
    h = x * lax.rsqrt(ms + RMS_EPS) * g_ref[...]
    h_ref[...] = h
    logits = jnp.dot(h, rw_ref[...], precision=lax.Precision.HIGHEST, preferred_element_type=F32) + rb_ref[...]
    lane = lax.broadcasted_iota(jnp.int32, logits.shape, 1)
    big = jnp.int32(LANES)

    def first_argmax(vals, valid):
        masked = jnp.where(valid, vals, -jnp.inf)
        mx = jnp.max(masked, axis=-1, keepdims=True)
        idx = jnp.min(jnp.where(valid & (masked == mx), lane, big), axis=-1, keepdims=True)
        return mx, idx

    is_group = lane < N_GROUPS
    g_max, g_idx = first_argmax(logits, is_group)
    g_top = 1.0 / jnp.sum(jnp.where(is_group, jnp.exp(logits - g_max), 0.0), axis=-1, keepdims=True)
    in_group = (lane >= N_GROUPS) & (lane < N_GROUPS + N_EXPERTS) & ((lane - N_GROUPS) // EXP_PER_GROUP == g_idx)
    i_max, idx1 = first_argmax(logits, in_group)
    z = jnp.sum(jnp.where(in_group, jnp.exp(logits - i_max), 0.0), axis=-1, keepdims=True)
    i_max2, idx2 = first_argmax(logits, in_group & (lane != idx1))
    p1 = 1.0 / z
    p2 = jnp.exp(i_max2 - i_max) / z
    psum = p1 + p2
    gate1 = g_top * p1 / psum
    gate2 = g_top * p2 / psum
    route = jnp.where(lane == 0, (idx1 - N_GROUPS).astype(F32),
                      jnp.where(lane == 1, (idx2 - N_GROUPS).astype(F32),
                                jnp.where(lane == 2, gate1, jnp.where(lane == 3, gate2, 0.0))))
    route_ref[...] = route


def norm_router(x, g, rw, rb, tm=256):
    m, d = x.shape
    return pl.pallas_call(
        _router_kernel,
        out_shape=(jax.ShapeDtypeStruct((m, d), F32), jax.ShapeDtypeStruct((m, LANES), F32)),
        grid=(_exact_div(m, tm),),
        in_specs=[pl.BlockSpec((tm, d), lambda i: (i, 0)), pl.BlockSpec((1, d), lambda i: (0, 0)),
                  pl.BlockSpec((d, LANES), lambda i: (0, 0)), pl.BlockSpec((1, LANES), lambda i: (0, 0))],
        out_specs=(pl.BlockSpec((tm, d), lambda i: (i, 0)), pl.BlockSpec((tm, LANES), lambda i: (i, 0))),
        compiler_params=_cparams(("parallel",)),
        name="norm_router",
    )(x, g.reshape(1, d), rw, rb)


def _row_copy(src_hbm, buf, sem, slot, src_row, dst_row):
    return pltpu.make_async_copy(src_hbm.at[pl.ds(src_row, 1)], buf.at[slot, pl.ds(dst_row, 1)], sem.at[slot])


def _wait_slot(src_hbm, buf, sem, slot):
    pltpu.make_async_copy(src_hbm.at[pl.ds(0, buf.shape[1])], buf.at[slot], sem.at[slot]).wait()


ROW_ISSUE_UNROLL = 8


def _gather_rows_kernel(idx_ref, nused_ref, src_hbm, o_ref, buf, sem, *, rows):
    b = pl.program_id(0)
    nb = pl.num_programs(0)
    used = nused_ref[0]

    def issue(blk, slot):
        def body(r, c):
            _row_copy(src_hbm, buf, sem, slot, idx_ref[blk * rows + r], r).start()
            return c
        lax.fori_loop(0, rows, body, 0, unroll=ROW_ISSUE_UNROLL)

    def wait(slot):
        _wait_slot(src_hbm, buf, sem, slot)

    @pl.when((b == 0) & (used > 0))
    def _():
        issue(0, 0)

    @pl.when((b + 1 < nb) & (b + 1 < used))
    def _():
        issue(b + 1, (b + 1) % 2)

    @pl.when(b < used)
    def _():
        wait(b % 2)
        o_ref[...] = buf[b % 2].astype(o_ref.dtype)

    @pl.when(b >= used)
    def _():
        o_ref[...] = jnp.zeros(o_ref.shape, o_ref.dtype)


def gather_rows(src, idx, n_used, rows, out_dtype):
    n = idx.shape[0]
    d = src.shape[1]
    return pl.pallas_call(
        functools.partial(_gather_rows_kernel, rows=rows),
        out_shape=jax.ShapeDtypeStruct((n, d), out_dtype),
        grid_spec=pltpu.PrefetchScalarGridSpec(
            num_scalar_prefetch=2,
            grid=(n // rows,),
            in_specs=[pl.BlockSpec(memory_space=pl.ANY)],
            out_specs=pl.BlockSpec((rows, d), lambda b, idx, nu: (b, 0)),
            scratch_shapes=[pltpu.VMEM((2, rows, d), src.dtype), pltpu.SemaphoreType.DMA((2,))]),
        compiler_params=_cparams(("arbitrary",)),
        name="gather_rows",
    )(idx, n_used, src)


def _combine_kernel(idx_ref, src_hbm, x_ref, route_ref, o_ref, buf, sem, *, rows):
    b = pl.program_id(0)
    nb = pl.num_programs(0)

    def issue(blk, slot):
        def body(r, c):
            base = (blk * rows + r) * TOP_K
            _row_copy(src_hbm, buf, sem, slot, idx_ref[base], r).start()
            _row_copy(src_hbm, buf, sem, slot, idx_ref[base + 1], rows + r).start()
            return c
        lax.fori_loop(0, rows, body, 0, unroll=ROW_ISSUE_UNROLL)

    def wait(slot):
        _wait_slot(src_hbm, buf, sem, slot)

    @pl.when(b == 0)
    def _():
        issue(0, 0)

    @pl.when(b + 1 < nb)
    def _():
        issue(b + 1, (b + 1) % 2)

    slot = b % 2
    wait(slot)
    route = route_ref[...]
    g1 = route[:, 2:3]
    g2 = route[:, 3:4]
    o_ref[...] = x_ref[...] + (g1 * buf[slot, pl.ds(0, rows), :] + g2 * buf[slot, pl.ds(rows, rows), :])


def moe_combine(yb, dest, x, route, rows=128):
    m, d = x.shape
    return pl.pallas_call(
        functools.partial(_combine_kernel, rows=rows),
        out_shape=jax.ShapeDtypeStruct((m, d), F32),
        grid_spec=pltpu.PrefetchScalarGridSpec(
            num_scalar_prefetch=1,
            grid=(_exact_div(m, rows),),
            in_specs=[pl.BlockSpec(memory_space=pl.ANY),
                      pl.BlockSpec((rows, d), lambda b, idx: (b, 0)),
                      pl.BlockSpec((rows, LANES), lambda b, idx: (b, 0))],
            out_specs=pl.BlockSpec((rows, d), lambda b, idx: (b, 0)),
            scratch_shapes=[pltpu.VMEM((2, TOP_K * rows, d), F32), pltpu.SemaphoreType.DMA((2,))]),
        compiler_params=_cparams(("arbitrary",)),
        name="moe_combine",
    )(dest, yb, x, route)


def _expert_runs(block_e, n_used):
    nb = block_e.shape[0]
    idx = jnp.arange(nb, dtype=jnp.int32)
    valid = idx < n_used[0]
    first = valid & ((idx == 0) | (block_e != jnp.roll(block_e, 1)))
    slot = (jnp.cumsum(first.astype(jnp.int32)) - 1) % 2
    nxt = lax.cummin(jnp.where(first, idx, nb)[::-1])[::-1]
    nxt = jnp.concatenate([nxt[1:], jnp.full((1,), nb, jnp.int32)])
    next_e = jnp.where(nxt < nb, block_e[jnp.minimum(nxt, nb - 1)], -1)
    return first.astype(jnp.int32), slot.astype(jnp.int32), next_e.astype(jnp.int32)


def _stream_expert_weights(b, be_ref, first_ref, slot_ref, next_ref, copies, cast):
    @pl.when(first_ref[b] == 1)
    def _():
        slot = slot_ref[b]

        @pl.when(b == 0)
        def _():
            for cp in copies(be_ref[0], 0):
                cp.start()

        for cp in copies(be_ref[b], slot):
            cp.wait()

        @pl.when(next_ref[b] >= 0)
        def _():
            for cp in copies(next_ref[b], 1 - slot):
                cp.start()

        cast(slot)


def _expert_up_kernel(be_ref, nused_ref, first_ref, slot_ref, next_ref, x_ref, wg_hbm, wu_hbm, o_ref,
                      wbuf, wgb_ref, wub_ref, sem, *, tf):
    f = pl.program_id(0)
    b = pl.program_id(1)

    def copies(e, slot):
        cols = pl.ds(pl.multiple_of(f * tf, tf), tf)
        return [pltpu.make_async_copy(w.at[e, :, cols], wbuf.at[slot, k], sem.at[slot, k])
                for k, w in enumerate((wg_hbm, wu_hbm))]

    def cast(slot):
        wgb_ref[...] = wbuf[slot, 0].astype(BF16)
        wub_ref[...] = wbuf[slot, 1].astype(BF16)

    @pl.when(b < nused_ref[0])
    def _():
        _stream_expert_weights(b, be_ref, first_ref, slot_ref, next_ref, copies, cast)
        x = x_ref[...]
        g = jnp.dot(x, wgb_ref[...], preferred_element_type=F32)
        u = jnp.dot(x, wub_ref[...], preferred_element_type=F32)
        o_ref[...] = (g * _sigmoid(g) * u).astype(BF16)

    @pl.when(b >= nused_ref[0])
    def _():
        o_ref[...] = jnp.zeros(o_ref.shape, o_ref.dtype)


def expert_up(xg, block_e, n_used, runs, wg, wu, tf=512):
    n, d = xg.shape
    de = wg.shape[2]
    bm = EXPERT_ROWS
    live = lambda b, nu: jnp.minimum(b, jnp.maximum(nu[0] - 1, 0))
    return pl.pallas_call(
        functools.partial(_expert_up_kernel, tf=tf),
        out_shape=jax.ShapeDtypeStruct((n, de), BF16),
        grid_spec=pltpu.PrefetchScalarGridSpec(
            num_scalar_prefetch=5,
            grid=(de // tf, n // bm),
            in_specs=[pl.BlockSpec((bm, d), lambda f, b, be, nu, fi, sl, ne: (live(b, nu), 0)),
                      pl.BlockSpec(memory_space=pl.ANY), pl.BlockSpec(memory_space=pl.ANY)],
            out_specs=pl.BlockSpec((bm, tf), lambda f, b, be, nu, fi, sl, ne: (b, f)),
            scratch_shapes=[pltpu.VMEM((2, 2, d, tf), F32), pltpu.VMEM((d, tf), BF16), pltpu.VMEM((d, tf), BF16),
                            pltpu.SemaphoreType.DMA((2, 2))]),
        compiler_params=_cparams(("arbitrary", "arbitrary")),
        name="expert_up",
    )(block_e, n_used, *runs, xg, wg, wu)


def _expert_down_kernel(be_ref, nused_ref, first_ref, slot_ref, next_ref, h_ref, wd_hbm, o_ref, wbuf, wdb_ref, sem,
                        *, tn):
    c = pl.program_id(0)
    b = pl.program_id(1)

    def copies(e, slot):
        cols = pl.ds(pl.multiple_of(c * tn, tn), tn)
        return [pltpu.make_async_copy(wd_hbm.at[e, :, cols], wbuf.at[slot], sem.at[slot])]

    def cast(slot):
        wdb_ref[...] = wbuf[slot].astype(BF16)

    @pl.when(b < nused_ref[0])
    def _():
        _stream_expert_weights(b, be_ref, first_ref, slot_ref, next_ref, copies, cast)
        o_ref[...] = jnp.dot(h_ref[...], wdb_ref[...], preferred_element_type=F32)

    @pl.when(b >= nused_ref[0])
    def _():
        o_ref[...] = jnp.zeros(o_ref.shape, o_ref.dtype)


def expert_down(hmid, block_e, n_used, runs, wd, tn=1024):
    n, de = hmid.shape
    d = wd.shape[2]
    bm = EXPERT_ROWS
    live = lambda b, nu: jnp.minimum(b, jnp.maximum(nu[0] - 1, 0))
    return pl.pallas_call(
        functools.partial(_expert_down_kernel, tn=tn),
        out_shape=jax.ShapeDtypeStruct((n, d), F32),
        grid_spec=pltpu.PrefetchScalarGridSpec(
            num_scalar_prefetch=5,
            grid=(d // tn, n // bm),
            in_specs=[pl.BlockSpec((bm, de), lambda c, b, be, nu, fi, sl, ne: (live(b, nu), 0)),
                      pl.BlockSpec(memory_space=pl.ANY)],
            out_specs=pl.BlockSpec((bm, tn), lambda c, b, be, nu, fi, sl, ne: (b, c)),
            scratch_shapes=[pltpu.VMEM((2, de, tn), F32), pltpu.VMEM((de, tn), BF16), pltpu.SemaphoreType.DMA((2,))]),
        compiler_params=_cparams(("arbitrary", "arbitrary")),
        name="expert_down",
    )(block_e, n_used, *runs, hmid, wd)


def _perm_cols(x):
    pre = x.shape[:-1]
    return jnp.moveaxis(x.reshape(pre + (N_HEADS_B, K_HI, K_LO)), -3, -1).reshape(pre + (RWKV_WIDTH,))


def _unperm_cols(x):
    pre = x.shape[:-1]
    return jnp.moveaxis(x.reshape(pre + (K_HI, K_LO, N_HEADS_B)), -1, -3).reshape(pre + (RWKV_WIDTH,))


def _perm_rows(x):
    return _perm_cols(x.T).T


def _pack_lora_cols(x):
    pad = jnp.zeros(x.shape[:-1] + (LANES - RANK_W,), x.dtype)
    return jnp.concatenate([x[..., :RANK_W], pad, x[..., RANK_W:RANK_W + RANK_A], pad, x[..., RANK_W + RANK_A:]], -1)


def _unpack_lora_cols(x):
    return jnp.concatenate([x[..., :RANK_W], x[..., LANES:LANES + RANK_A], x[..., 2 * LANES:]], -1)


def _state_to_tiles(s):
    n = s.shape[0]
    s = s.reshape(n, N_HEADS_B, V_BLK, SUBLANES, K_HI, K_LO)
    return s.transpose(0, 2, 4, 3, 5, 1).reshape(n, V_BLK, K_HI, SUBLANES, LANES)


def _tiles_to_state(s):
    n = s.shape[0]
    s = s.reshape(n, V_BLK, K_HI, SUBLANES, K_LO, N_HEADS_B)
    return s.transpose(0, 5, 1, 3, 2, 4).reshape(n, N_HEADS_B, HEAD_B, HEAD_B)


def _pad_rows(w, rows):
    return jnp.concatenate([w, jnp.zeros((rows - w.shape[0],) + w.shape[1:], w.dtype)], 0)


def _trunk_layer(xp, xs, pp, ps, cache_k, cache_v, state_wkv, state_shift, t5_table, lam_init, lp, attn_blk=512):
    t, d = xp.shape
    nb, ts, _ = xs.shape
    past = cache_k.shape[1]
    ms = nb * ts
    m = t + ms
    aw = ATT_WIDTH
    c = RWKV_WIDTH
    assert ts == SCAN_TB and t % SCAN_TB == 0 and past % CHUNK == 0 and ts <= CHUNK

    x = jnp.concatenate([xp, xs.reshape(ms, d)], 0)
    pe = jnp.concatenate([pp, ps.reshape(ms, -1)], 0).astype(BF16)

    w_in = lp["w_in"]
    rkv0 = 3 * aw
    w_rkv = jnp.concatenate([_perm_cols(w_in[:, rkv0 + i * c:rkv0 + (i + 1) * c]) for i in range(3)], 1)
    w_lora = _pack_lora_cols(w_in[:, rkv0 + 3 * c:])
    h1 = rmsnorm_cast(x, lp["norm1_g"])
    proj_qkv = matmul(h1, w_in, 3 * aw)
    feat = matmul(h1, w_rkv, 3 * c)
    lora = matmul(h1, w_lora, LORA_COLS)

    qn, k_new, kn, vn = qk_norm(proj_qkv, lp["q_norm_g"], lp["k_norm_g"])
    v_new = proj_qkv[:, 2 * aw:]
    lams = [lp[n].reshape(1, HEAD_DIM_A) for n in ("lambda_q1", "lambda_k1", "lambda_q2", "lambda_k2")]
    blk = min(attn_blk, t)
    assert t % blk == 0 and blk >= T5_FAR
    bias_d = bias_tiles(t5_table, blk, blk, rel0=0, masked=True, key_major=True)
    bias_l = bias_tiles(t5_table, blk, blk, rel0=-blk, key_major=True)
    far_bias = t5_table[T5_BUCKETS // 2 - 1] * LOG2E
    nkb = t // blk
    ones_tile = jnp.zeros((nkb, N_HEADS_A, BF16_SUBLANES, blk), BF16).at[:, :, 0, :].set(1.0)
    vt = jnp.transpose(vn[:t].reshape(nkb, blk, N_HEADS_A, 2 * HEAD_DIM_A), (0, 2, 3, 1))
    vt = jnp.concatenate([vt, ones_tile], 2).reshape(nkb, N_HEADS_A * VT_ROWS, blk)
    ya_p = prompt_attention(qn, kn, vt, t, bias_d, bias_l, far_bias, lams, lp["subln_g"], lam_init, blk)
    bias_past = bias_tiles(t5_table, ts, past, rel0=-past)
    bias_new = bias_tiles(t5_table, ts, ts, rel0=0)
    ya_s = sample_attention(qn, kn, vn, t, cache_k.reshape(nb, past, 2 * N_HEADS_A, HEAD_DIM_A), cache_v,
                            bias_past, bias_new, lams, lp["subln_g"], lam_init)

    shift_rkv = jnp.concatenate([_perm_cols(state_shift[:, 0, i * c:(i + 1) * c]) for i in range(3)], 1)
    shift_lora = _pack_lora_cols(state_shift[:, 0, 3 * c:])
    seq_rows = t + ts * jnp.arange(nb)
    prev = jnp.roll(feat, 1, axis=0).at[0].set(0.0).at[seq_rows].set(shift_rkv)
    prev_lora = jnp.roll(lora, 1, axis=0).at[0].set(0.0).at[seq_rows].set(shift_lora)
    vec = lambda v: v.reshape(1, -1)
    mu = lp["rwkv_mu"]
    prm = dict(
        mu_rkv=vec(jnp.concatenate([_perm_cols(mu[i * c:(i + 1) * c]) for i in range(3)])),
        mu_lora=vec(_pack_lora_cols(mu[3 * c:])),
        w0=vec(_perm_cols(lp["rwkv_w0"])), a0=vec(_perm_cols(lp["rwkv_a0"])),
        k_k=vec(_perm_cols(lp["rwkv_k_k"])), k_a=vec(_perm_cols(lp["rwkv_k_a"])),
        r_k=vec(_perm_cols(lp["rwkv_r_k"].reshape(-1))),
        w2=_pad_rows(_perm_cols(lp["rwkv_w2"]), LANES), a2=_pad_rows(_perm_cols(lp["rwkv_a2"]), LANES),
        g2=_perm_cols(lp["rwkv_g2"]))
    ak, wr, wdec, bvec, km, vv, gate, br, kr, bonus = rwkv_prep(feat, prev, lora, prev_lora, prm)
    n_pstep = t // SCAN_TB
    seq_of_step = jnp.concatenate([jnp.zeros((n_pstep,), jnp.int32), 1 + jnp.arange(nb, dtype=jnp.int32)])
    first = jnp.concatenate([jnp.zeros((n_pstep,), jnp.int32).at[0].set(1), jnp.ones((nb,), jnp.int32)])
    last = jnp.concatenate([jnp.zeros((n_pstep,), jnp.int32).at[-1].set(1), jnp.ones((nb,), jnp.int32)])
    s0 = jnp.concatenate([jnp.zeros((1, V_BLK, K_HI, SUBLANES, LANES), F32),
                          _state_to_tiles(state_wkv.astype(F32))], 0)
    y_scan, s_fin = rwkv_scan((ak, wr, wdec, bvec, km, vv), br, s0, seq_of_step, first, last)
    yb = rwkv_post(y_scan, vv, gate, kr, bonus, vec(_perm_cols(lp["lnx_g"])), vec(_perm_cols(lp["lnx_b"])))
    wkv_fin = _tiles_to_state(s_fin)

    def shift_out(rows):
        return jnp.concatenate([_unperm_cols(feat[rows, i * c:(i + 1) * c]) for i in range(3)]
                               + [_unpack_lora_cols(lora[rows])], -1)

    shift_p = shift_out(jnp.array([t - 1]))
    shift_s = shift_out(seq_rows + ts - 1)

    y_mix = jnp.concatenate([jnp.concatenate([ya_p, ya_s], 0), yb], 1)
    w_out = jnp.concatenate([lp["w_out"][:aw], _perm_rows(lp["w_out"][aw:])], 0)
    x1 = matmul(y_mix, w_out, d, mode="residual", res=x)

    rw = jnp.concatenate([lp["rg_w"], lp["ri_w"], jnp.zeros((d, LANES - N_GROUPS - N_EXPERTS), F32)], 1)
    rb = jnp.concatenate([lp["rg_b"], lp["ri_b"].reshape(-1), jnp.zeros((LANES - N_GROUPS - N_EXPERTS,), F32)])
    h2, route = norm_router(x1, lp["norm2_g"], rw, rb.reshape(1, LANES))
    n_assign = m * TOP_K
    flat_e = route[:, :TOP_K].astype(jnp.int32).reshape(n_assign)
    bm = EXPERT_ROWS
    seg = LANES
    onehot = (flat_e[:, None] == jnp.arange(N_EXPERTS, dtype=jnp.int32)[None, :])
    oh = onehot.astype(BF16).reshape(_exact_div(n_assign, seg), seg, N_EXPERTS)
    tri = (jnp.arange(seg)[:, None] >= jnp.arange(seg)[None, :]).astype(BF16)
    within = jnp.einsum("ij,bje->bie", tri, oh, preferred_element_type=F32)
    seg_tot = within[:, -1, :]
    seg_off = jnp.cumsum(seg_tot, axis=0) - seg_tot
    running = (within + seg_off[:, None, :]).reshape(n_assign, N_EXPERTS)
    counts = (seg_off[-1] + seg_tot[-1]).astype(jnp.int32)
    rank = jnp.sum(jnp.where(onehot, running, 0.0), axis=1).astype(jnp.int32) - 1
    pcounts = (counts + bm - 1) // bm * bm
    pend = jnp.cumsum(pcounts)
    dest = (pend - pcounts)[flat_e] + rank
    n_blocks = n_assign // bm + N_EXPERTS
    rows_tok = jnp.zeros((n_blocks * bm,), jnp.int32).at[dest].set(jnp.arange(n_assign, dtype=jnp.int32) // TOP_K)
    block_e = jnp.minimum(jnp.searchsorted(pend, jnp.arange(n_blocks, dtype=jnp.int32) * bm, side="right"),
                          N_EXPERTS - 1).astype(jnp.int32)
    n_used = (pend[-1] // bm).astype(jnp.int32).reshape(1)
    xg = gather_rows(h2, rows_tok, n_used, bm, BF16)
    runs = _expert_runs(block_e, n_used)
    hmid = expert_up(xg, block_e, n_used, runs, lp["e_wg"], lp["e_wu"])
    yexp = expert_down(hmid, block_e, n_used, runs, lp["e_wd"])
    x2 = moe_combine(yexp, dest.astype(jnp.int32), x1, route)

    h3 = rmsnorm_cast(x2, lp["ple_norm_g"])
    x3 = matmul(h3, lp["ple_gate_w"], d, mode="ple", res=x2, p=pe, pw=lp["ple_proj_w"])

    return (x3[:t], x3[t:].reshape(nb, ts, d), k_new, v_new, wkv_fin, shift_p, shift_s)


def kernel(x_prompt, x_sample, p_prompt, p_sample, cache_k, cache_v, state_wkv, state_shift, t5_table, norm1_g, w_in, q_norm_g, k_norm_g, lambda_q1, lambda_k1, lambda_q2, lambda_k2, subln_g, rwkv_mu, rwkv_w0, rwkv_w2, rwkv_a0, rwkv_a2, rwkv_g2, rwkv_k_k, rwkv_k_a, rwkv_r_k, lnx_g, lnx_b, w_out, norm2_g, router_group_w, router_group_b, router_inner_w, router_inner_b, expert_w_gate, expert_w_up, expert_w_down, ple_norm_g, ple_gate_w, ple_proj_w):
    depth = w_in.shape[0]
    bp, t, d = x_prompt.shape
    nb, ts, _ = x_sample.shape
    assert depth == 1 and bp == 1, "one layer and one prompt stream are fused with the sample batch"
    i = 0
    lp = dict(norm1_g=norm1_g[i], w_in=w_in[i], q_norm_g=q_norm_g[i], k_norm_g=k_norm_g[i],
              lambda_q1=lambda_q1[i], lambda_k1=lambda_k1[i], lambda_q2=lambda_q2[i], lambda_k2=lambda_k2[i],
              subln_g=subln_g[i], rwkv_mu=rwkv_mu[i], rwkv_w0=rwkv_w0[i], rwkv_w2=rwkv_w2[i],
              rwkv_a0=rwkv_a0[i], rwkv_a2=rwkv_a2[i], rwkv_g2=rwkv_g2[i], rwkv_k_k=rwkv_k_k[i],
              rwkv_k_a=rwkv_k_a[i], rwkv_r_k=rwkv_r_k[i], lnx_g=lnx_g[i], lnx_b=lnx_b[i], w_out=w_out[i],
              norm2_g=norm2_g[i], rg_w=router_group_w[i], rg_b=router_group_b[i], ri_w=router_inner_w[i],
              ri_b=router_inner_b[i], e_wg=expert_w_gate[i], e_wu=expert_w_up[i], e_wd=expert_w_down[i],
              ple_norm_g=ple_norm_g[i], ple_gate_w=ple_gate_w[i], ple_proj_w=ple_proj_w[i])
    lam_init = 0.8 - 0.6 * math.exp(-0.3 * i)
    yp, ys, k_new, v_new, wkv_fin, shift_p, shift_s = _trunk_layer(
        x_prompt[0], x_sample, p_prompt[i, 0], p_sample[i], cache_k[i], cache_v[i], state_wkv[i],
        state_shift[i], t5_table, lam_init, lp)
    hk = (N_HEADS_A, 2, HEAD_DIM_A)
    hv = (N_HEADS_A, 2 * HEAD_DIM_A)
    return (yp[None], ys,
            k_new[:t].reshape((1, 1, t) + hk), v_new[:t].reshape((1, 1, t) + hv),
            wkv_fin[:1][None], shift_p.reshape(1, 1, 1, -1),
            k_new[t:].reshape((1, nb, ts) + hk), v_new[t:].reshape((1, nb, ts) + hv),
            wkv_fin[1:][None], shift_s.reshape(1, nb, 1, -1))
```

```python
import functools
import math

import jax
import jax.numpy as jnp
from jax import lax
from jax.experimental import pallas as pl
from jax.experimental.pallas import tpu as pltpu

F32 = jnp.float32
BF16 = jnp.bfloat16

LANES = 128
SUBLANES = 8
VMEM_BYTES_V7X = 64 * 1024 * 1024
VMEM_LIMIT = VMEM_BYTES_V7X - 6 * 1024 * 1024

CHUNK = 64
HEAD_DIM_A = 128
N_HEADS_A = 8
ATT_WIDTH = 2 * HEAD_DIM_A * N_HEADS_A
T5_BUCKETS = 32
HEAD_B = 64
N_HEADS_B = 32
RWKV_WIDTH = HEAD_B * N_HEADS_B
RANK_W = 96
RANK_A = 96
RANK_G = 256
LORA_COLS = 512
LNX_EPS = 64e-5
N_GROUPS = 8
EXP_PER_GROUP = 8
N_EXPERTS = N_GROUPS * EXP_PER_GROUP
TOP_K = 2
RMS_EPS = 1e-6
NEG_INF = -1e30
T5_LOG_THRESHOLDS = (12, 16, 23, 32, 46, 64, 91)
T5_FAR = 128

K_LO = LANES // N_HEADS_B
K_HI = HEAD_B // K_LO
V_BLK = HEAD_B // SUBLANES
SCAN_TB = 32

EXPERT_ROWS = 256


def _cparams(sem, vmem=VMEM_LIMIT):
    return pltpu.CompilerParams(dimension_semantics=sem, vmem_limit_bytes=vmem)


def _exact_div(a, b):
    assert a % b == 0, (a, b)
    return a // b


def _sigmoid(x):
    return 1.0 / (1.0 + jnp.exp(-x))


def _rmsnorm_kernel(x_ref, g_ref, o_ref):
    x = x_ref[...]
    ms = jnp.mean(x * x, axis=-1, keepdims=True)
    o_ref[...] = (x * lax.rsqrt(ms + RMS_EPS) * g_ref[...]).astype(o_ref.dtype)


def rmsnorm_cast(x, g, tm=256):
    m, d = x.shape
    return pl.pallas_call(
        _rmsnorm_kernel,
        out_shape=jax.ShapeDtypeStruct((m, d), BF16),
        grid=(_exact_div(m, tm),),
        in_specs=[pl.BlockSpec((tm, d), lambda i: (i, 0)), pl.BlockSpec((1, d), lambda i: (0, 0))],
        out_specs=pl.BlockSpec((tm, d), lambda i: (i, 0)),
        compiler_params=_cparams(("parallel",)),
        name="rmsnorm_cast",
    )(x, g.reshape(1, d))


def _mm_kernel(a_ref, w_ref, *rest, mode):
    if mode == "plain":
        o_ref, wb_ref = rest
    elif mode == "residual":
        res_ref, o_ref, wb_ref = rest
    else:
        res_ref, p_ref, pw_ref, o_ref, wb_ref = rest

    @pl.when(pl.program_id(1) == 0)
    def _():
        wb_ref[...] = w_ref[...].astype(BF16)

    acc = jnp.dot(a_ref[...], wb_ref[...], preferred_element_type=F32)
    if mode == "plain":
        o_ref[...] = acc
    elif mode == "residual":
        o_ref[...] = res_ref[...] + acc
    else:
        pe = jnp.dot(p_ref[...], pw_ref[...].astype(BF16), preferred_element_type=F32)
        o_ref[...] = res_ref[...] + pe * _sigmoid(acc)


def matmul(a, w, n_cols, *, col_block0=0, mode="plain", res=None, p=None, pw=None, tm=1024, tn=512):
    m, k = a.shape
    tm = min(tm, m)
    assert m % tm == 0 and n_cols % tn == 0
    in_specs = [pl.BlockSpec((tm, k), lambda n, i: (i, 0)),
                pl.BlockSpec((k, tn), lambda n, i: (0, n + col_block0))]
    args = [a, w]
    if mode in ("residual", "ple"):
        in_specs.append(pl.BlockSpec((tm, tn), lambda n, i: (i, n)))
        args.append(res)
    if mode == "ple":
        kp = p.shape[1]
        in_specs += [pl.BlockSpec((tm, kp), lambda n, i: (i, 0)), pl.BlockSpec((kp, tn), lambda n, i: (0, n))]
        args += [p, pw]
    return pl.pallas_call(
        functools.partial(_mm_kernel, mode=mode),
        out_shape=jax.ShapeDtypeStruct((m, n_cols), F32),
        grid=(n_cols // tn, m // tm),
        in_specs=in_specs,
        out_specs=pl.BlockSpec((tm, tn), lambda n, i: (i, n)),
        scratch_shapes=[pltpu.VMEM((k, tn), BF16)],
        compiler_params=_cparams(("arbitrary", "arbitrary")),
        name="matmul_" + mode,
    )(*args)


def _qk_norm_kernel(q_ref, k_ref, v_ref, qg_ref, kg_ref, qo_ref, ko_ref, kbo_ref, vo_ref):
    def head_norm(x, g):
        ms = jnp.mean(x * x, axis=-1, keepdims=True)
        return x * lax.rsqrt(ms + RMS_EPS) * g

    qg = qg_ref[...]
    kg = kg_ref[...]
    for c in range(ATT_WIDTH // HEAD_DIM_A):
        sl = slice(c * HEAD_DIM_A, (c + 1) * HEAD_DIM_A)
        qn = head_norm(q_ref[:, sl], qg)
        qo_ref[:, sl] = (qn * (LOG2E * HEAD_DIM_A ** -0.5)).astype(BF16)
        kn = head_norm(k_ref[:, sl], kg)
        ko_ref[:, sl] = kn
        kbo_ref[:, sl] = kn.astype(BF16)
    vo_ref[...] = v_ref[...].astype(BF16)


def qk_norm(proj_qkv, q_g, k_g, tm=256):
    m = proj_qkv.shape[0]
    w = ATT_WIDTH
    blk = lambda c: pl.BlockSpec((tm, w), lambda i, c=c: (i, c))
    vec = pl.BlockSpec((1, HEAD_DIM_A), lambda i: (0, 0))
    out_blk = pl.BlockSpec((tm, w), lambda i: (i, 0))
    return pl.pallas_call(
        _qk_norm_kernel,
        out_shape=(jax.ShapeDtypeStruct((m, w), BF16), jax.ShapeDtypeStruct((m, w), F32),
                   jax.ShapeDtypeStruct((m, w), BF16), jax.ShapeDtypeStruct((m, w), BF16)),
        grid=(_exact_div(m, tm),),
        in_specs=[blk(0), blk(1), blk(2), vec, vec],
        out_specs=(out_blk, out_blk, out_blk, out_blk),
        compiler_params=_cparams(("parallel",)),
        name="qk_norm",
    )(proj_qkv, proj_qkv, proj_qkv, q_g.reshape(1, -1), k_g.reshape(1, -1))


def _bias_kernel(tab_ref, o_ref, *, rel0, masked, key_major, n_valid):
    _, nr, nc = o_ref.shape
    r = lax.broadcasted_iota(jnp.int32, (nr, nc), 0)
    c = lax.broadcasted_iota(jnp.int32, (nr, nc), 1)
    kpos, qpos = (r, c) if key_major else (c, r)
    rel = rel0 + kpos - qpos
    n = jnp.abs(rel)
    large = jnp.full((nr, nc), T5_BUCKETS // 4, jnp.int32)
    for thr in T5_LOG_THRESHOLDS:
        large = large + jnp.where(n >= thr, 1, 0)
    bucket = jnp.where(n < T5_BUCKETS // 4, n, large) + jnp.where(rel > 0, T5_BUCKETS // 2, 0)
    if masked:
        visible = (kpos // CHUNK) <= (qpos // CHUNK)
    for h in range(N_HEADS_A):
        acc = jnp.zeros((nr, nc), F32)
        for b in range(T5_BUCKETS):
            acc = jnp.where(bucket == b, tab_ref[b, h] * LOG2E, acc)
        if masked:
            acc = jnp.where(visible, acc, NEG_INF)
        if n_valid is not None:
            acc = jnp.where(c < n_valid, acc, NEG_INF)
        o_ref[h] = acc


def bias_tiles(table, nr, nc, *, rel0, masked=False, key_major=False, n_valid=None):
    return pl.pallas_call(
        functools.partial(_bias_kernel, rel0=rel0, masked=masked, key_major=key_major, n_valid=n_valid),
        out_shape=jax.ShapeDtypeStruct((N_HEADS_A, nr, nc), F32),
        in_specs=[pl.BlockSpec(memory_space=pltpu.SMEM)],
        out_specs=pl.BlockSpec(memory_space=pltpu.VMEM),
        compiler_params=_cparams(None),
        name="t5_bias",
    )(table)


def _lambda_value(lq1, lk1, lq2, lk2, lam_init):
    s1 = jnp.sum(lq1 * lk1, axis=-1, keepdims=True)
    s2 = jnp.sum(lq2 * lk2, axis=-1, keepdims=True)
    return jnp.exp(s1) - jnp.exp(s2) + lam_init


def _online_step(s, bias, v, m_prev, l_prev, acc_prev):
    nchunk = max(s.shape[1] // LANES, 1)
    width = s.shape[1] // nchunk
    sc = [s[:, c * width:(c + 1) * width] + bias(c * width, width) for c in range(nchunk)]
    m_new = jnp.maximum(m_prev, jnp.max(functools.reduce(jnp.maximum, sc), axis=-1, keepdims=True))
    alpha = jnp.exp2(m_prev - m_new)
    p = [jnp.exp2(x - m_new) for x in sc]
    l_new = alpha * l_prev + jnp.sum(functools.reduce(lambda a, b: a + b, p), axis=-1, keepdims=True)
    pb = p[0].astype(BF16) if nchunk == 1 else jnp.concatenate([x.astype(BF16) for x in p], axis=1)
    return m_new, l_new, alpha * acc_prev + jnp.dot(pb, v, preferred_element_type=F32)


def _diff_finish(acc1, l1, acc2, l2, lam, g, lam_init):
    o = acc1 / l1 - lam * (acc2 / l2)
    ms = jnp.mean(o * o, axis=-1, keepdims=True)
    return (o * lax.rsqrt(ms + RMS_EPS) * g) * (1.0 - lam_init)


_NT = (((1,), (1,)), ((), ()))
BF16_SUBLANES = 16
VT_ROWS = 2 * HEAD_DIM_A + BF16_SUBLANES
LOG2E = math.log2(math.e)


def _prompt_attn_kernel(far_ref, q_ref, k_ref, vt_ref, bd_ref, bl_ref, lq1, lk1, lq2, lk2, g_ref,
                        o_ref, m_ref, acc_ref, *, blk, nsub, lam_init):
    h = pl.program_id(0)
    i = pl.program_id(1)
    m_ref[...] = jnp.full(m_ref.shape, NEG_INF, F32)
    acc_ref[...] = jnp.zeros(acc_ref.shape, F32)
    d = HEAD_DIM_A
    hw = 2 * d
    far_bias = far_ref[h]

    def chain(st, bias_tile, vt, idx):
        shift = far_bias if bias_tile is None else None
        if bias_tile is not None:
            st = st + bias_tile[0]
        col_max = jnp.max(st, axis=0, keepdims=True)
        if shift is not None:
            col_max = col_max + shift
        m_prev = m_ref[idx]
        m_new = jnp.maximum(m_prev, col_max)
        alpha = jnp.exp2(m_prev - m_new)
        p = jnp.exp2(st - (m_new if shift is None else m_new - shift))
        acc_ref[idx] = alpha * acc_ref[idx] + jnp.dot(vt, p.astype(BF16), preferred_element_type=F32)
        m_ref[idx] = m_new

    def update(j, ahead):
        kb = k_ref[pl.ds(pl.multiple_of(j * blk, blk), blk), :]
        vt = vt_ref[j]
        todo = []
        for sa in range(nsub):
            if ahead[sa] < 0:
                continue
            bias_tile = None if ahead[sa] >= 2 else (bl_ref if ahead[sa] == 1 else bd_ref)
            for mp in range(2):
                st = lax.dot_general(kb[:, mp * d:(mp + 1) * d], q_ref[sa * blk:(sa + 1) * blk, mp * d:(mp + 1) * d],
                                     _NT, preferred_element_type=F32)
                todo.append((st, bias_tile, 2 * sa + mp))
        for st, bias_tile, idx in todo:
            chain(st, bias_tile, vt, idx)

    def far_body(j, carry):
        update(j, [2] * nsub)
        return carry

    lax.fori_loop(0, jnp.maximum(nsub * i - 1, 0), far_body, 0)

    @pl.when(i >= 1)
    def _():
        update(nsub * i - 1, [sa + 1 for sa in range(nsub)])

    for o in range(nsub):
        update(nsub * i + o, [sa - o for sa in range(nsub)])
    lam = _lambda_value(lq1[...], lk1[...], lq2[...], lk2[...], lam_init)
    for sa in range(nsub):
        a1 = acc_ref[2 * sa]
        a2 = acc_ref[2 * sa + 1]
        ot = a1[:hw] / a1[hw:hw + 1] - lam * (a2[:hw] / a2[hw:hw + 1])
        ms = jnp.mean(ot * ot, axis=0, keepdims=True)
        yt = ot * lax.rsqrt(ms + RMS_EPS)
        o_ref[sa * blk:(sa + 1) * blk, :] = ((yt.T * g_ref[...]) * (1.0 - lam_init)).astype(BF16)


def prompt_attention(qn, kn, vt, t, bias_d, bias_l, far_bias, lams, subln_g, lam_init, blk, nsub=2):
    nsub = min(nsub, t // blk)
    bq = nsub * blk
    hw = 2 * HEAD_DIM_A
    vec = pl.BlockSpec((1, HEAD_DIM_A), lambda h, i, far: (0, 0))
    return pl.pallas_call(
        functools.partial(_prompt_attn_kernel, blk=blk, nsub=nsub, lam_init=lam_init),
        out_shape=jax.ShapeDtypeStruct((t, ATT_WIDTH), BF16),
        grid_spec=pltpu.PrefetchScalarGridSpec(
            num_scalar_prefetch=1,
            grid=(N_HEADS_A, _exact_div(t, bq)),
            in_specs=[pl.BlockSpec((bq, hw), lambda h, i, far: (i, h)),
                      pl.BlockSpec((t, hw), lambda h, i, far: (0, h)),
                      pl.BlockSpec((t // blk, VT_ROWS, blk), lambda h, i, far: (0, h, 0)),
                      pl.BlockSpec((1, blk, blk), lambda h, i, far: (h, 0, 0)),
                      pl.BlockSpec((1, blk, blk), lambda h, i, far: (h, 0, 0)),
                      vec, vec, vec, vec,
                      pl.BlockSpec((1, hw), lambda h, i, far: (0, 0))],
            out_specs=pl.BlockSpec((bq, hw), lambda h, i, far: (i, h)),
            scratch_shapes=[pltpu.VMEM((2 * nsub, 1, blk), F32), pltpu.VMEM((2 * nsub, VT_ROWS, blk), F32)]),
        compiler_params=_cparams(("arbitrary", "arbitrary")),
        name="prompt_attention",
    )(far_bias, qn, kn, vt, bias_d, bias_l, *lams, subln_g.reshape(1, hw))


def _sublane_transpose8(tiles, sub):
    a = list(tiles)
    for dist in (4, 2, 1):
        keep = (sub % (2 * dist)) < dist
        nxt = list(a)
        for i in range(SUBLANES):
            if i % (2 * dist) < dist:
                x, y = a[i], a[i + dist]
                nxt[i] = jnp.where(keep, x, pltpu.roll(y, dist, axis=0))
                nxt[i + dist] = jnp.where(keep, pltpu.roll(x, SUBLANES - dist, axis=0), y)
        a = nxt
    return a


def _sample_attn_kernel(q_ref, ck_ref, cv_ref, kn_ref, vn_ref, bp_ref, bn_ref, lq1, lk1, lq2, lk2, g_ref,
                        o_ref, m_ref, l_ref, acc_ref, ks_ref, vs_ref, *, lam_init):
    t = pl.program_id(1)
    last = t == pl.num_programs(1) - 1
    d = HEAD_DIM_A
    tk = ck_ref.shape[1]
    ts = q_ref.shape[0]
    n = 2 * N_HEADS_A
    sub = lax.broadcasted_iota(jnp.int32, (SUBLANES, LANES), 0)

    @pl.when(t == 0)
    def _():
        m_ref[...] = jnp.full(m_ref.shape, NEG_INF, F32)
        l_ref[...] = jnp.zeros(l_ref.shape, F32)
        acc_ref[...] = jnp.zeros(acc_ref.shape, F32)
        ks_ref[:, pl.ds(tk, LANES), :] = jnp.zeros((n, LANES, d), BF16)
        vs_ref[:, pl.ds(tk, LANES), :] = jnp.zeros((N_HEADS_A, LANES, 2 * d), BF16)
        for c in range(n):
            ks_ref[c, pl.ds(tk, ts), :] = kn_ref[:, c * d:(c + 1) * d]
        for h in range(N_HEADS_A):
            vs_ref[h, pl.ds(tk, ts), :] = vn_ref[:, 2 * h * d:2 * (h + 1) * d]

    def to_head_major(src, dst, p0, lanes):
        halves = [_sublane_transpose8(src[SUBLANES * a:SUBLANES * (a + 1)], sub) for a in range(2)]
        for r in range(SUBLANES):
            dst(r)[pl.ds(p0, BF16_SUBLANES), lanes] = jnp.concatenate([halves[0][r], halves[1][r]], 0).astype(BF16)

    def relayout(g, carry):
        p0 = pl.multiple_of(g * BF16_SUBLANES, BF16_SUBLANES)
        kt = ck_ref[0, pl.ds(p0, BF16_SUBLANES), :, :]
        vt = cv_ref[0, pl.ds(p0, BF16_SUBLANES), :, :]
        for a in range(2):
            rows = slice(a * SUBLANES, (a + 1) * SUBLANES)
            to_head_major([kt[p, rows, :] for p in range(BF16_SUBLANES)], lambda r, a=a: ks_ref.at[a * SUBLANES + r],
                          p0, slice(None))
            lanes = slice(a * LANES, (a + 1) * LANES)
            to_head_major([vt[p, :, lanes] for p in range(BF16_SUBLANES)], lambda r: vs_ref.at[r], p0, lanes)
        return carry

    lax.fori_loop(0, tk // BF16_SUBLANES, relayout, 0)

    state = [(m_ref[c], l_ref[c], acc_ref[c]) for c in range(n)]
    logits = [lax.dot_general(q_ref[:, c * d:(c + 1) * d], ks_ref[c], _NT, preferred_element_type=F32)
              for c in range(n)]
    for c in range(n):
        h = c // 2

        def bias(k0, w, h=h):
            if k0 < tk:
                return bp_ref[h, :, k0:k0 + w]
            return jnp.where(last, bn_ref[h], NEG_INF)

        state[c] = _online_step(logits[c], bias, vs_ref[h], *state[c])
    for c in range(n):
        m_ref[c], l_ref[c], acc_ref[c] = state[c]

    @pl.when(last)
    def _():
        lam = _lambda_value(lq1[...], lk1[...], lq2[...], lk2[...], lam_init)
        for h in range(N_HEADS_A):
            y = _diff_finish(acc_ref[2 * h], l_ref[2 * h], acc_ref[2 * h + 1], l_ref[2 * h + 1], lam,
                             g_ref[...], lam_init)
            o_ref[:, 2 * h * d:2 * (h + 1) * d] = y.astype(BF16)


def sample_attention(qn, kn, vn, row0, cache_k, cache_v, bias_past, bias_new, lams, subln_g, lam_init, tk=1024):
    nb, past = cache_k.shape[:2]
    w = ATT_WIDTH
    ts = bias_new.shape[1]
    tk = min(tk, past)
    blk0 = row0 // ts
    hw = 2 * HEAD_DIM_A
    vec = pl.BlockSpec((1, HEAD_DIM_A), lambda b, t: (0, 0))
    new_rows = pl.BlockSpec((ts, w), lambda b, t: (blk0 + b, 0))
    return pl.pallas_call(
        functools.partial(_sample_attn_kernel, lam_init=lam_init),
        out_shape=jax.ShapeDtypeStruct((nb * ts, w), BF16),
        grid=(nb, _exact_div(past, tk)),
        in_specs=[new_rows,
                  pl.BlockSpec((1, tk, 2 * N_HEADS_A, HEAD_DIM_A), lambda b, t: (b, t, 0, 0)),
                  pl.BlockSpec((1, tk, N_HEADS_A, hw), lambda b, t: (b, t, 0, 0)),
                  new_rows, new_rows,
                  pl.BlockSpec((N_HEADS_A, ts, tk), lambda b, t: (0, 0, t)),
                  pl.BlockSpec((N_HEADS_A, ts, LANES), lambda b, t: (0, 0, 0)),
                  vec, vec, vec, vec,
                  pl.BlockSpec((1, hw), lambda b, t: (0, 0))],
        out_specs=pl.BlockSpec((ts, w), lambda b, t: (b, 0)),
        scratch_shapes=[pltpu.VMEM((2 * N_HEADS_A, ts, 1), F32), pltpu.VMEM((2 * N_HEADS_A, ts, 1), F32),
                        pltpu.VMEM((2 * N_HEADS_A, ts, hw), F32),
                        pltpu.VMEM((2 * N_HEADS_A, tk + LANES, HEAD_DIM_A), BF16),
                        pltpu.VMEM((N_HEADS_A, tk + LANES, hw), BF16)],
        compiler_params=_cparams(("arbitrary", "arbitrary")),
        name="sample_attention",
    )(qn, cache_k, cache_v, kn, vn, bias_past, bias_new, *lams, subln_g.reshape(1, hw))


def _group_allreduce(x):
    x = x + pltpu.roll(x, N_HEADS_B, axis=1)
    return x + pltpu.roll(x, 2 * N_HEADS_B, axis=1)


def _head_sum128(x):
    acc = x[:, 0:LANES]
    for c in range(1, K_HI):
        acc = acc + x[:, c * LANES:(c + 1) * LANES]
    return _group_allreduce(acc)


def _tile16(x128):
    return jnp.concatenate([x128] * K_HI, axis=1)


def _token_shift(x_ref, start_ref, buf_ref, mu_ref):
    tm = x_ref.shape[0]
    x = x_ref[...]
    buf_ref[pl.ds(0, SUBLANES), :] = jnp.zeros((SUBLANES, x.shape[1]), F32)
    buf_ref[pl.ds(SUBLANES, tm), :] = x
    shifted = buf_ref[pl.ds(SUBLANES - 1, tm), :]
    first = lax.broadcasted_iota(jnp.int32, (SCAN_TB, 1), 0) == 0
    prev = jnp.concatenate([jnp.where(first, start_ref[0, g:g + 1, :], shifted[g * SCAN_TB:(g + 1) * SCAN_TB])
                            for g in range(tm // SCAN_TB)], 0)
    return x + (prev - x) * mu_ref[...]


def _rwkv_prep_kernel(f_ref, sf_ref, lo_ref, slo_ref, mu_ref, mul_ref, w0_ref, a0_ref, kk_ref, ka_ref, rk_ref,
                      w2_ref, a2_ref, g2_ref,
                      ak_o, wr_o, w_o, b_o, km_o, v_o, g_o, br_o, kr_o, bo_o, fbuf, lbuf):
    c = RWKV_WIDTH
    xm = _token_shift(f_ref, sf_ref, fbuf, mu_ref)
    xl = _token_shift(lo_ref, slo_ref, lbuf, mul_ref)
    r, k, v = xm[:, :c], xm[:, c:2 * c], xm[:, 2 * c:]
    wd, ad, gd = xl[:, :LANES], xl[:, LANES:2 * LANES], xl[:, 2 * LANES:]
    lw = w0_ref[...] + jnp.dot(jnp.tanh(wd).astype(BF16), w2_ref[...].astype(BF16), preferred_element_type=F32)
    z = -lw
    softplus = jnp.maximum(z, 0.0) + jnp.log(1.0 + jnp.exp(-jnp.abs(z)))
    decay = jnp.exp(-jnp.exp(-softplus - 0.5))
    a = _sigmoid(a0_ref[...] + jnp.dot(ad.astype(BF16), a2_ref[...].astype(BF16), preferred_element_type=F32))
    g_o[...] = jnp.dot(_sigmoid(gd).astype(BF16), g2_ref[...].astype(BF16), preferred_element_type=F32)
    kk = k * kk_ref[...]
    norm = jnp.maximum(jnp.sqrt(_tile16(_head_sum128(kk * kk))), 1e-12)
    kk = kk / norm
    kmod = k * (1.0 + (a - 1.0) * ka_ref[...])
    bvec = kk * a
    ak_o[...] = -kk
    wr_o[...] = decay * r
    w_o[...] = decay
    b_o[...] = bvec
    km_o[...] = kmod
    v_o[...] = v
    br_o[...] = _head_sum128(bvec * r)
    kr_o[...] = _head_sum128(kmod * r)
    bo_o[...] = _head_sum128(r * kmod * rk_ref[...])


def rwkv_prep(feat, start, lora, start_lora, prm, tm=128):
    m = feat.shape[0]
    c = RWKV_WIDTH
    row = lambda w: pl.BlockSpec((tm, w), lambda i: (i, 0))
    st = lambda w: pl.BlockSpec((1, SUBLANES, w), lambda i: (i, 0, 0))
    vec = lambda w: pl.BlockSpec((1, w), lambda i: (0, 0))
    mat = lambda r: pl.BlockSpec((r, c), lambda i: (0, 0))
    big = jax.ShapeDtypeStruct((m, c), F32)
    small = jax.ShapeDtypeStruct((m, LANES), F32)
    assert tm // SCAN_TB <= SUBLANES
    return pl.pallas_call(
        _rwkv_prep_kernel,
        out_shape=(big,) * 7 + (small,) * 3,
        grid=(_exact_div(m, tm),),
        in_specs=[row(3 * c), st(3 * c), row(LORA_COLS), st(LORA_COLS), vec(3 * c), vec(LORA_COLS),
                  vec(c), vec(c), vec(c), vec(c), vec(c), mat(LANES), mat(LANES), mat(RANK_G)],
        out_specs=(row(c),) * 7 + (row(LANES),) * 3,
        scratch_shapes=[pltpu.VMEM((tm + SUBLANES, 3 * c), F32), pltpu.VMEM((tm + SUBLANES, LORA_COLS), F32)],
        compiler_params=_cparams(("arbitrary",)),
        name="rwkv_prep",
    )(feat, start, lora, start_lora, prm["mu_rkv"], prm["mu_lora"], prm["w0"], prm["a0"], prm["k_k"], prm["k_a"],
      prm["r_k"], prm["w2"], prm["a2"], prm["g2"])


def _scan_kernel(seq_ref, first_ref, last_ref, ak_ref, wr_ref, w_ref, b_ref, km_ref, v_ref, br_ref, s0_ref,
                 y_ref, sout_ref, s_ref):
    step = pl.program_id(0)

    @pl.when(first_ref[step] == 1)
    def _():
        s_ref[...] = s0_ref[0]

    sub = lax.broadcasted_iota(jnp.int32, (SUBLANES, LANES), 0)
    grp = lax.broadcasted_iota(jnp.int32, (SUBLANES, LANES), 1) // N_HEADS_B
    own_group = grp == (sub % K_LO)
    low_half = sub < K_LO

    def row(ref, t8, s, c):
        tile = ref[pl.ds(t8, SUBLANES), c * LANES:(c + 1) * LANES]
        return jnp.broadcast_to(tile[s:s + 1], (SUBLANES, LANES))

    def time_step(t8, s, y_lo, y_hi):
        vt = []
        for vb in range(V_BLK):
            tile = jnp.where(low_half, row(v_ref, t8, s, 2 * vb), row(v_ref, t8, s, 2 * vb + 1))
            vt.append(_group_allreduce(jnp.where(own_group, tile, 0.0)))
        acc_u = [jnp.zeros((SUBLANES, LANES), F32) for _ in range(V_BLK)]
        acc_y = [jnp.zeros((SUBLANES, LANES), F32) for _ in range(V_BLK)]
        for kh in range(K_HI):
            a_row = row(ak_ref, t8, s, kh)
            wr_row = row(wr_ref, t8, s, kh)
            for vb in range(V_BLK):
                st = s_ref[vb, kh]
                acc_u[vb] = acc_u[vb] + st * a_row
                acc_y[vb] = acc_y[vb] + st * wr_row
        br_row = row(br_ref, t8, s, 0)
        u = [_group_allreduce(x) for x in acc_u]
        for vb in range(V_BLK):
            y = _group_allreduce(acc_y[vb]) + u[vb] * br_row
            ym = jnp.where(own_group, y, 0.0)
            ym = ym + pltpu.roll(ym, 1, axis=0)
            ym = ym + pltpu.roll(ym, 2, axis=0)
            y_lo[vb] = jnp.where(sub == s, pltpu.roll(ym, (s - (K_LO - 1)) % SUBLANES, axis=0), y_lo[vb])
            y_hi[vb] = jnp.where(sub == s, pltpu.roll(ym, (s - (2 * K_LO - 1)) % SUBLANES, axis=0), y_hi[vb])
        for kh in range(K_HI):
            w_row = row(w_ref, t8, s, kh)
            b_row = row(b_ref, t8, s, kh)
            km_row = row(km_ref, t8, s, kh)
            for vb in range(V_BLK):
                s_ref[vb, kh] = s_ref[vb, kh] * w_row + (b_row * u[vb] + km_row * vt[vb])

    def eight_steps(gi, carry):
        t8 = pl.multiple_of(gi * SUBLANES, SUBLANES)
        y_lo = [jnp.zeros((SUBLANES, LANES), F32) for _ in range(V_BLK)]
        y_hi = [jnp.zeros((SUBLANES, LANES), F32) for _ in range(V_BLK)]
        for s in range(SUBLANES):
            time_step(t8, s, y_lo, y_hi)
        for vb in range(V_BLK):
            y_ref[pl.ds(t8, SUBLANES), (2 * vb) * LANES:(2 * vb + 1) * LANES] = y_lo[vb]
            y_ref[pl.ds(t8, SUBLANES), (2 * vb + 1) * LANES:(2 * vb + 2) * LANES] = y_hi[vb]
        return carry

    lax.fori_loop(0, y_ref.shape[0] // SUBLANES, eight_steps, 0)

    @pl.when(last_ref[step] == 1)
    def _():
        sout_ref[0] = s_ref[...]


def rwkv_scan(ops, br, s0, seq_of_step, first, last):
    m, c = ops[0].shape
    nseq = s0.shape[0]
    tb = SCAN_TB
    row = lambda w: pl.BlockSpec((tb, w), lambda i, sq, fi, la: (i, 0))
    st = pl.BlockSpec((1, V_BLK, K_HI, SUBLANES, LANES), lambda i, sq, fi, la: (sq[i], 0, 0, 0, 0))
    return pl.pallas_call(
        _scan_kernel,
        out_shape=(jax.ShapeDtypeStruct((m, c), F32), jax.ShapeDtypeStruct(s0.shape, F32)),
        grid_spec=pltpu.PrefetchScalarGridSpec(
            num_scalar_prefetch=3,
            grid=(m // tb,),
            in_specs=[row(c)] * 6 + [row(LANES), st],
            out_specs=(row(c), st),
            scratch_shapes=[pltpu.VMEM((V_BLK, K_HI, SUBLANES, LANES), F32)]),
        compiler_params=_cparams(("arbitrary",)),
        name="rwkv_scan",
    )(seq_of_step, first, last, *ops, br, s0)


def _rwkv_post_kernel(y_ref, v_ref, g_ref, kr_ref, bo_ref, lg_ref, lb_ref, o_ref):
    v = v_ref[...]
    y = y_ref[...] + v * _tile16(kr_ref[...])
    mean = _tile16(_head_sum128(y)) * (1.0 / HEAD_B)
    yc = y - mean
    var = _tile16(_head_sum128(yc * yc)) * (1.0 / HEAD_B)
    yn = yc * lax.rsqrt(var + LNX_EPS) * lg_ref[...] + lb_ref[...]
    o_ref[...] = ((yn + _tile16(bo_ref[...]) * v) * g_ref[...]).astype(BF16)


def rwkv_post(y, v, g, kr, bo, lnx_g, lnx_b, tm=256):
    m, c = y.shape
    row = lambda w: pl.BlockSpec((tm, w), lambda i: (i, 0))
    vec = pl.BlockSpec((1, c), lambda i: (0, 0))
    return pl.pallas_call(
        _rwkv_post_kernel,
        out_shape=jax.ShapeDtypeStruct((m, c), BF16),
        grid=(_exact_div(m, tm),),
        in_specs=[row(c), row(c), row(c), row(LANES), row(LANES), vec, vec],
        out_specs=row(c),
        compiler_params=_cparams(("parallel",)),
        name="rwkv_post",
    )(y, v, g, kr, bo, lnx_g, lnx_b)


def _router_kernel(x_ref, g_ref, rw_ref, rb_ref, h_ref, route_ref):
    x = x_ref[...]
    ms = jnp.mean(x * x, axis=-1, keepdims=True)
    h = x * lax.rsqrt(ms + RMS_EPS) * g_ref[...]
    h_ref[...] = h
    logits = jnp.dot(h, rw_ref[...], precision=lax.Precision.HIGHEST, preferred_element_type=F32) + rb_ref[...]
    lane = lax.broadcasted_iota(jnp.int32, logits.shape, 1)
    big = jnp.int32(LANES)

    def first_argmax(vals, valid):
        masked = jnp.where(valid, vals, -jnp.inf)
        mx = jnp.max(masked, axis=-1, keepdims=True)
        idx = jnp.min(jnp.where(valid & (masked == mx), lane, big), axis=-1, keepdims=True)
        return mx, idx

    is_group = lane < N_GROUPS
    g_max, g_idx = first_argmax(logits, is_group)
    g_top = 1.0 / jnp.sum(jnp.where(is_group, jnp.exp(logits - g_max), 0.0), axis=-1, keepdims=True)
    in_group = (lane >= N_GROUPS) & (lane < N_GROUPS + N_EXPERTS) & ((lane - N_GROUPS) // EXP_PER_GROUP == g_idx)
    i_max, idx1 = first_argmax(logits, in_group)
    z = jnp.sum(jnp.where(in_group, jnp.exp(logits - i_max), 0.0), axis=-1, keepdims=True)
    i_max2, idx2 = first_argmax(logits, in_group & (lane != idx1))
    p1 = 1.0 / z
    p2 = jnp.exp(i_max2 - i_max) / z
    psum = p1 + p2
    gate1 = g_top * p1 / psum
    gate2 = g_top * p2 / psum
    route = jnp.where(lane == 0, (idx1 - N_GROUPS).astype(F32),
                      jnp.where(lane == 1, (idx2 - N_GROUPS).astype(F32),
                                jnp.where(lane == 2, gate1, jnp.where(lane == 3, gate2, 0.0))))
    route_ref[...] = route


def norm_router(x, g, rw, rb, tm=256):
    m, d = x.shape
    return pl.pallas_call(
        _router_kernel,
        out_shape=(jax.ShapeDtypeStruct((m, d), F32), jax.ShapeDtypeStruct((m, LANES), F32)),
        grid=(_exact_div(m, tm),),
        in_specs=[pl.BlockSpec((tm, d), lambda i: (i, 0)), pl.BlockSpec((1, d), lambda i: (0, 0)),
                  pl.BlockSpec((d, LANES), lambda i: (0, 0)), pl.BlockSpec((1, LANES), lambda i: (0, 0))],
        out_specs=(pl.BlockSpec((tm, d), lambda i: (i, 0)), pl.BlockSpec((tm, LANES), lambda i: (i, 0))),
        compiler_params=_cparams(("parallel",)),
        name="norm_router",
    )(x, g.reshape(1, d), rw, rb)


def _row_copy(src_hbm, buf, sem, slot, src_row, dst_row):
    return pltpu.make_async_copy(src_hbm.at[pl.ds(src_row, 1)], buf.at[slot, pl.ds(dst_row, 1)], sem.at[slot])


def _wait_slot(src_hbm, buf, sem, slot):
    pltpu.make_async_copy(src_hbm.at[pl.ds(0, buf.shape[1])], buf.at[slot], sem.at[slot]).wait()


ROW_ISSUE_UNROLL = 8


def _gather_rows_kernel(idx_ref, nused_ref, src_hbm, o_ref, buf, sem, *, rows):
    b = pl.program_id(0)
    nb = pl.num_programs(0)
    used = nused_ref[0]

    def issue(blk, slot):
        def body(r, c):
            _row_copy(src_hbm, buf, sem, slot, idx_ref[blk * rows + r], r).start()
            return c
        lax.fori_loop(0, rows, body, 0, unroll=ROW_ISSUE_UNROLL)

    def wait(slot):
        _wait_slot(src_hbm, buf, sem, slot)

    @pl.when((b == 0) & (used > 0))
    def _():
        issue(0, 0)

    @pl.when((b + 1 < nb) & (b + 1 < used))
    def _():
        issue(b + 1, (b + 1) % 2)

    @pl.when(b < used)
    def _():
        wait(b % 2)
        o_ref[...] = buf[b % 2].astype(o_ref.dtype)

    @pl.when(b >= used)
    def _():
        o_ref[...] = jnp.zeros(o_ref.shape, o_ref.dtype)


def gather_rows(src, idx, n_used, rows, out_dtype):
    n = idx.shape[0]
    d = src.shape[1]
    return pl.pallas_call(
        functools.partial(_gather_rows_kernel, rows=rows),
        out_shape=jax.ShapeDtypeStruct((n, d), out_dtype),
        grid_spec=pltpu.PrefetchScalarGridSpec(
            num_scalar_prefetch=2,
            grid=(n // rows,),
            in_specs=[pl.BlockSpec(memory_space=pl.ANY)],
            out_specs=pl.BlockSpec((rows, d), lambda b, idx, nu: (b, 0)),
            scratch_shapes=[pltpu.VMEM((2, rows, d), src.dtype), pltpu.SemaphoreType.DMA((2,))]),
        compiler_params=_cparams(("arbitrary",)),
        name="gather_rows",
    )(idx, n_used, src)


def _combine_kernel(idx_ref, src_hbm, x_ref, route_ref, o_ref, buf, sem, *, rows):
    b = pl.program_id(0)
    nb = pl.num_programs(0)

    def issue(blk, slot):
        def body(r, c):
            base = (blk * rows + r) * TOP_K
            _row_copy(src_hbm, buf, sem, slot, idx_ref[base], r).start()
            _row_copy(src_hbm, buf, sem, slot, idx_ref[base + 1], rows + r).start()
            return c
        lax.fori_loop(0, rows, body, 0, unroll=ROW_ISSUE_UNROLL)

    def wait(slot):
        _wait_slot(src_hbm, buf, sem, slot)

    @pl.when(b == 0)
    def _():
        issue(0, 0)

    @pl.when(b + 1 < nb)
    def _():
        issue(b + 1, (b + 1) % 2)

    slot = b % 2
    wait(slot)
    route = route_ref[...]
    g1 = route[:, 2:3]
    g2 = route[:, 3:4]
    o_ref[...] = x_ref[...] + (g1 * buf[slot, pl.ds(0, rows), :] + g2 * buf[slot, pl.ds(rows, rows), :])


def moe_combine(yb, dest, x, route, rows=128):
    m, d = x.shape
    return pl.pallas_call(
        functools.partial(_combine_kernel, rows=rows),
        out_shape=jax.ShapeDtypeStruct((m, d), F32),
        grid_spec=pltpu.PrefetchScalarGridSpec(
            num_scalar_prefetch=1,
            grid=(_exact_div(m, rows),),
            in_specs=[pl.BlockSpec(memory_space=pl.ANY),
                      pl.BlockSpec((rows, d), lambda b, idx: (b, 0)),
                      pl.BlockSpec((rows, LANES), lambda b, idx: (b, 0))],
            out_specs=pl.BlockSpec((rows, d), lambda b, idx: (b, 0)),
            scratch_shapes=[pltpu.VMEM((2, TOP_K * rows, d), F32), pltpu.SemaphoreType.DMA((2,))]),
        compiler_params=_cparams(("arbitrary",)),
        name="moe_combine",
    )(dest, yb, x, route)


def _expert_runs(block_e, n_used):
    nb = block_e.shape[0]
    idx = jnp.arange(nb, dtype=jnp.int32)
    valid = idx < n_used[0]
    first = valid & ((idx == 0) | (block_e != jnp.roll(block_e, 1)))
    upto = idx[None, :] <= idx[:, None]
    slot = (jnp.sum(jnp.where(upto & first[None, :], 1, 0), axis=1) - 1) % 2
    nxt = jnp.min(jnp.where(first[None, :] & ~upto, idx[None, :], nb), axis=1)
    next_e = jnp.where(nxt < nb, block_e[jnp.minimum(nxt, nb - 1)], -1)
    return first.astype(jnp.int32), slot.astype(jnp.int32), next_e.astype(jnp.int32)


def _stream_expert_weights(b, be_ref, first_ref, slot_ref, next_ref, copies, cast):
    @pl.when(first_ref[b] == 1)
    def _():
        slot = slot_ref[b]

        @pl.when(b == 0)
        def _():
            for cp in copies(be_ref[0], 0):
                cp.start()

        for cp in copies(be_ref[b], slot):
            cp.wait()

        @pl.when(next_ref[b] >= 0)
        def _():
            for cp in copies(next_ref[b], 1 - slot):
                cp.start()

        cast(slot)


def _expert_up_kernel(be_ref, nused_ref, first_ref, slot_ref, next_ref, x_ref, wg_hbm, wu_hbm, o_ref,
                      wbuf, wgb_ref, wub_ref, sem, *, tf):
    f = pl.program_id(0)
    b = pl.program_id(1)

    def copies(e, slot):
        cols = pl.ds(pl.multiple_of(f * tf, tf), tf)
        return [pltpu.make_async_copy(w.at[e, :, cols], wbuf.at[slot, k], sem.at[slot, k])
                for k, w in enumerate((wg_hbm, wu_hbm))]

    def cast(slot):
        wgb_ref[...] = wbuf[slot, 0].astype(BF16)
        wub_ref[...] = wbuf[slot, 1].astype(BF16)

    @pl.when(b < nused_ref[0])
    def _():
        _stream_expert_weights(b, be_ref, first_ref, slot_ref, next_ref, copies, cast)
        x = x_ref[...]
        g = jnp.dot(x, wgb_ref[...], preferred_element_type=F32)
        u = jnp.dot(x, wub_ref[...], preferred_element_type=F32)
        o_ref[...] = (g * _sigmoid(g) * u).astype(BF16)

    @pl.when(b >= nused_ref[0])
    def _():
        o_ref[...] = jnp.zeros(o_ref.shape, o_ref.dtype)


def expert_up(xg, block_e, n_used, runs, wg, wu, tf=512):
    n, d = xg.shape
    de = wg.shape[2]
    bm = EXPERT_ROWS
    live = lambda b, nu: jnp.minimum(b, jnp.maximum(nu[0] - 1, 0))
    return pl.pallas_call(
        functools.partial(_expert_up_kernel, tf=tf),
        out_shape=jax.ShapeDtypeStruct((n, de), BF16),
        grid_spec=pltpu.PrefetchScalarGridSpec(
            num_scalar_prefetch=5,
            grid=(de // tf, n // bm),
            in_specs=[pl.BlockSpec((bm, d), lambda f, b, be, nu, fi, sl, ne: (live(b, nu), 0)),
                      pl.BlockSpec(memory_space=pl.ANY), pl.BlockSpec(memory_space=pl.ANY)],
            out_specs=pl.BlockSpec((bm, tf), lambda f, b, be, nu, fi, sl, ne: (b, f)),
            scratch_shapes=[pltpu.VMEM((2, 2, d, tf), F32), pltpu.VMEM((d, tf), BF16), pltpu.VMEM((d, tf), BF16),
                            pltpu.SemaphoreType.DMA((2, 2))]),
        compiler_params=_cparams(("arbitrary", "arbitrary")),
        name="expert_up",
    )(block_e, n_used, *runs, xg, wg, wu)


def _expert_down_kernel(be_ref, nused_ref, first_ref, slot_ref, next_ref, h_ref, wd_hbm, o_ref, wbuf, wdb_ref, sem,
                        *, tn):
    c = pl.program_id(0)
    b = pl.program_id(1)

    def copies(e, slot):
        cols = pl.ds(pl.multiple_of(c * tn, tn), tn)
        return [pltpu.make_async_copy(wd_hbm.at[e, :, cols], wbuf.at[slot], sem.at[slot])]

    def cast(slot):
        wdb_ref[...] = wbuf[slot].astype(BF16)

    @pl.when(b < nused_ref[0])
    def _():
        _stream_expert_weights(b, be_ref, first_ref, slot_ref, next_ref, copies, cast)
        o_ref[...] = jnp.dot(h_ref[...], wdb_ref[...], preferred_element_type=F32)

    @pl.when(b >= nused_ref[0])
    def _():
        o_ref[...] = jnp.zeros(o_ref.shape, o_ref.dtype)


def expert_down(hmid, block_e, n_used, runs, wd, tn=1024):
    n, de = hmid.shape
    d = wd.shape[2]
    bm = EXPERT_ROWS
    live = lambda b, nu: jnp.minimum(b, jnp.maximum(nu[0] - 1, 0))
    return pl.pallas_call(
        functools.partial(_expert_down_kernel, tn=tn),
        out_shape=jax.ShapeDtypeStruct((n, d), F32),
        grid_spec=pltpu.PrefetchScalarGridSpec(
            num_scalar_prefetch=5,
            grid=(d // tn, n // bm),
            in_specs=[pl.BlockSpec((bm, de), lambda c, b, be, nu, fi, sl, ne: (live(b, nu), 0)),
                      pl.BlockSpec(memory_space=pl.ANY)],
            out_specs=pl.BlockSpec((bm, tn), lambda c, b, be, nu, fi, sl, ne: (b, c)),
            scratch_shapes=[pltpu.VMEM((2, de, tn), F32), pltpu.VMEM((de, tn), BF16), pltpu.SemaphoreType.DMA((2,))]),
        compiler_params=_cparams(("arbitrary", "arbitrary")),
        name="expert_down",
    )(block_e, n_used, *runs, hmid, wd)


def _perm_cols(x):
    pre = x.shape[:-1]
    return jnp.moveaxis(x.reshape(pre + (N_HEADS_B, K_HI, K_LO)), -3, -1).reshape(pre + (RWKV_WIDTH,))


def _unperm_cols(x):
    pre = x.shape[:-1]
    return jnp.moveaxis(x.reshape(pre + (K_HI, K_LO, N_HEADS_B)), -1, -3).reshape(pre + (RWKV_WIDTH,))


def _perm_rows(x):
    return _perm_cols(x.T).T


def _pack_lora_cols(x):
    pad = jnp.zeros(x.shape[:-1] + (LANES - RANK_W,), x.dtype)
    return jnp.concatenate([x[..., :RANK_W], pad, x[..., RANK_W:RANK_W + RANK_A], pad, x[..., RANK_W + RANK_A:]], -1)


def _unpack_lora_cols(x):
    return jnp.concatenate([x[..., :RANK_W], x[..., LANES:LANES + RANK_A], x[..., 2 * LANES:]], -1)


def _state_to_tiles(s):
    n = s.shape[0]
    s = s.reshape(n, N_HEADS_B, V_BLK, SUBLANES, K_HI, K_LO)
    return s.transpose(0, 2, 4, 3, 5, 1).reshape(n, V_BLK, K_HI, SUBLANES, LANES)


def _tiles_to_state(s):
    n = s.shape[0]
    s = s.reshape(n, V_BLK, K_HI, SUBLANES, K_LO, N_HEADS_B)
    return s.transpose(0, 5, 1, 3, 2, 4).reshape(n, N_HEADS_B, HEAD_B, HEAD_B)


def _pad_rows(w, rows):
    return jnp.concatenate([w, jnp.zeros((rows - w.shape[0],) + w.shape[1:], w.dtype)], 0)


def _trunk_layer(xp, xs, pp, ps, cache_k, cache_v, state_wkv, state_shift, t5_table, lam_init, lp, attn_blk=512):
    t, d = xp.shape
    nb, ts, _ = xs.shape
    past = cache_k.shape[1]
    ms = nb * ts
    m = t + ms
    aw = ATT_WIDTH
    c = RWKV_WIDTH
    assert ts == SCAN_TB and t % SCAN_TB == 0 and past % CHUNK == 0 and ts <= CHUNK

    x = jnp.concatenate([xp, xs.reshape(ms, d)], 0)
    pe = jnp.concatenate([pp, ps.reshape(ms, -1)], 0).astype(BF16)

    w_in = lp["w_in"]
    rkv0 = 3 * aw
    w_rkv = jnp.concatenate([_perm_cols(w_in[:, rkv0 + i * c:rkv0 + (i + 1) * c]) for i in range(3)], 1)
    w_lora = _pack_lora_cols(w_in[:, rkv0 + 3 * c:])
    h1 = rmsnorm_cast(x, lp["norm1_g"])
    proj_qkv = matmul(h1, w_in, 3 * aw)
    feat = matmul(h1, w_rkv, 3 * c)
    lora = matmul(h1, w_lora, LORA_COLS)

    qn, k_new, kn, vn = qk_norm(proj_qkv, lp["q_norm_g"], lp["k_norm_g"])
    v_new = proj_qkv[:, 2 * aw:]
    lams = [lp[n].reshape(1, HEAD_DIM_A) for n in ("lambda_q1", "lambda_k1", "lambda_q2", "lambda_k2")]
    blk = min(attn_blk, t)
    assert t % blk == 0 and blk >= T5_FAR
    bias_d = bias_tiles(t5_table, blk, blk, rel0=0, masked=True, key_major=True)
    bias_l = bias_tiles(t5_table, blk, blk, rel0=-blk, key_major=True)
    far_bias = t5_table[T5_BUCKETS // 2 - 1] * LOG2E
    nkb = t // blk
    ones_tile = jnp.zeros((nkb, N_HEADS_A, BF16_SUBLANES, blk), BF16).at[:, :, 0, :].set(1.0)
    vt = jnp.transpose(vn[:t].reshape(nkb, blk, N_HEADS_A, 2 * HEAD_DIM_A), (0, 2, 3, 1))
    vt = jnp.concatenate([vt, ones_tile], 2).reshape(nkb, N_HEADS_A * VT_ROWS, blk)
    ya_p = prompt_attention(qn, kn, vt, t, bias_d, bias_l, far_bias, lams, lp["subln_g"], lam_init, blk)
    bias_past = bias_tiles(t5_table, ts, past, rel0=-past)
    bias_new = bias_tiles(t5_table, ts, LANES, rel0=0, n_valid=ts)
    ya_s = sample_attention(qn, kn, vn, t, cache_k.reshape(nb, past, 2 * N_HEADS_A, HEAD_DIM_A), cache_v,
                            bias_past, bias_new, lams, lp["subln_g"], lam_init)

    shift_rkv = jnp.concatenate([_perm_cols(state_shift[:, 0, i * c:(i + 1) * c]) for i in range(3)], 1)
    shift_lora = _pack_lora_cols(state_shift[:, 0, 3 * c:])
    seq_rows = t + ts * jnp.arange(nb)
    prep_tm = 4 * SCAN_TB

    def run_starts(x, state_rows):
        own = x[SCAN_TB - 1:t - 1:SCAN_TB]
        rows = jnp.concatenate([jnp.zeros((1, x.shape[1]), F32), own, state_rows], 0)
        rows = rows.reshape(m // prep_tm, prep_tm // SCAN_TB, x.shape[1])
        return jnp.concatenate([rows, jnp.zeros((m // prep_tm, SUBLANES - prep_tm // SCAN_TB, x.shape[1]), F32)], 1)

    vec = lambda v: v.reshape(1, -1)
    mu = lp["rwkv_mu"]
    prm = dict(
        mu_rkv=vec(jnp.concatenate([_perm_cols(mu[i * c:(i + 1) * c]) for i in range(3)])),
        mu_lora=vec(_pack_lora_cols(mu[3 * c:])),
        w0=vec(_perm_cols(lp["rwkv_w0"])), a0=vec(_perm_cols(lp["rwkv_a0"])),
        k_k=vec(_perm_cols(lp["rwkv_k_k"])), k_a=vec(_perm_cols(lp["rwkv_k_a"])),
        r_k=vec(_perm_cols(lp["rwkv_r_k"].reshape(-1))),
        w2=_pad_rows(_perm_cols(lp["rwkv_w2"]), LANES), a2=_pad_rows(_perm_cols(lp["rwkv_a2"]), LANES),
        g2=_perm_cols(lp["rwkv_g2"]))
    ak, wr, wdec, bvec, km, vv, gate, br, kr, bonus = rwkv_prep(
        feat, run_starts(feat, shift_rkv), lora, run_starts(lora, shift_lora), prm, tm=prep_tm)
    n_pstep = t // SCAN_TB
    seq_of_step = jnp.concatenate([jnp.zeros((n_pstep,), jnp.int32), 1 + jnp.arange(nb, dtype=jnp.int32)])
    first = jnp.concatenate([jnp.zeros((n_pstep,), jnp.int32).at[0].set(1), jnp.ones((nb,), jnp.int32)])
    last = jnp.concatenate([jnp.zeros((n_pstep,), jnp.int32).at[-1].set(1), jnp.ones((nb,), jnp.int32)])
    s0 = jnp.concatenate([jnp.zeros((1, V_BLK, K_HI, SUBLANES, LANES), F32),
                          _state_to_tiles(state_wkv.astype(F32))], 0)
    y_scan, s_fin = rwkv_scan((ak, wr, wdec, bvec, km, vv), br, s0, seq_of_step, first, last)
    yb = rwkv_post(y_scan, vv, gate, kr, bonus, vec(_perm_cols(lp["lnx_g"])), vec(_perm_cols(lp["lnx_b"])))
    wkv_fin = _tiles_to_state(s_fin)

    def shift_out(rows):
        return jnp.concatenate([_unperm_cols(feat[rows, i * c:(i + 1) * c]) for i in range(3)]
                               + [_unpack_lora_cols(lora[rows])], -1)

    shift_p = shift_out(jnp.array([t - 1]))
    shift_s = shift_out(seq_rows + ts - 1)

    y_mix = jnp.concatenate([jnp.concatenate([ya_p, ya_s], 0), yb], 1)
    w_out = jnp.concatenate([lp["w_out"][:aw], _perm_rows(lp["w_out"][aw:])], 0)
    x1 = matmul(y_mix, w_out, d, mode="residual", res=x)

    rw = jnp.concatenate([lp["rg_w"], lp["ri_w"], jnp.zeros((d, LANES - N_GROUPS - N_EXPERTS), F32)], 1)
    rb = jnp.concatenate([lp["rg_b"], lp["ri_b"].reshape(-1), jnp.zeros((LANES - N_GROUPS - N_EXPERTS,), F32)])
    h2, route = norm_router(x1, lp["norm2_g"], rw, rb.reshape(1, LANES))
    n_assign = m * TOP_K
    flat_e = route[:, :TOP_K].astype(jnp.int32).reshape(n_assign)
    bm = EXPERT_ROWS
    seg = LANES
    onehot = (flat_e[:, None] == jnp.arange(N_EXPERTS, dtype=jnp.int32)[None, :])
    oh = onehot.astype(BF16).reshape(_exact_div(n_assign, seg), seg, N_EXPERTS)
    tri = (jnp.arange(seg)[:, None] >= jnp.arange(seg)[None, :]).astype(BF16)
    within = jnp.einsum("ij,bje->bie", tri, oh, preferred_element_type=F32)
    seg_tot = within[:, -1, :]
    seg_off = jnp.cumsum(seg_tot, axis=0) - seg_tot
    running = (within + seg_off[:, None, :]).reshape(n_assign, N_EXPERTS)
    counts = (seg_off[-1] + seg_tot[-1]).astype(jnp.int32)
    rank = jnp.sum(jnp.where(onehot, running, 0.0), axis=1).astype(jnp.int32) - 1
    pcounts = (counts + bm - 1) // bm * bm
    eid = jnp.arange(N_EXPERTS, dtype=jnp.int32)
    pend = jnp.sum(jnp.where(eid[None, :] <= eid[:, None], pcounts[None, :], 0), axis=1)
    dest = (pend - pcounts)[flat_e] + rank
    n_blocks = n_assign // bm + N_EXPERTS
    rows_tok = jnp.zeros((n_blocks * bm,), jnp.int32).at[dest].set(jnp.arange(n_assign, dtype=jnp.int32) // TOP_K)
    block_row0 = jnp.arange(n_blocks, dtype=jnp.int32) * bm
    block_e = jnp.minimum(jnp.sum((pend[None, :] <= block_row0[:, None]).astype(jnp.int32), axis=1), N_EXPERTS - 1)
    n_used = (pend[-1] // bm).astype(jnp.int32).reshape(1)
    xg = gather_rows(h2, rows_tok, n_used, bm, BF16)
    runs = _expert_runs(block_e, n_used)
    hmid = expert_up(xg, block_e, n_used, runs, lp["e_wg"], lp["e_wu"])
    yexp = expert_down(hmid, block_e, n_used, runs, lp["e_wd"])
    x2 = moe_combine(yexp, dest.astype(jnp.int32), x1, route)

    h3 = rmsnorm_cast(x2, lp["ple_norm_g"])
    x3 = matmul(h3, lp["ple_gate_w"], d, mode="ple", res=x2, p=pe, pw=lp["ple_proj_w"])

    return (x3[:t], x3[t:].reshape(nb, ts, d), k_new, v_new, wkv_fin, shift_p, shift_s)


def kernel(x_prompt, x_sample, p_prompt, p_sample, cache_k, cache_v, state_wkv, state_shift, t5_table, norm1_g, w_in, q_norm_g, k_norm_g, lambda_q1, lambda_k1, lambda_q2, lambda_k2, subln_g, rwkv_mu, rwkv_w0, rwkv_w2, rwkv_a0, rwkv_a2, rwkv_g2, rwkv_k_k, rwkv_k_a, rwkv_r_k, lnx_g, lnx_b, w_out, norm2_g, router_group_w, router_group_b, router_inner_w, router_inner_b, expert_w_gate, expert_w_up, expert_w_down, ple_norm_g, ple_gate_w, ple_proj_w):
    depth = w_in.shape[0]
    bp, t, d = x_prompt.shape
    nb, ts, _ = x_sample.shape
    assert depth == 1 and bp == 1, "one layer and one prompt stream are fused with the sample batch"
    i = 0
    lp = dict(norm1_g=norm1_g[i], w_in=w_in[i], q_norm_g=q_norm_g[i], k_norm_g=k_norm_g[i],
              lambda_q1=lambda_q1[i], lambda_k1=lambda_k1[i], lambda_q2=lambda_q2[i], lambda_k2=lambda_k2[i],
              subln_g=subln_g[i], rwkv_mu=rwkv_mu[i], rwkv_w0=rwkv_w0[i], rwkv_w2=rwkv_w2[i],
              rwkv_a0=rwkv_a0[i], rwkv_a2=rwkv_a2[i], rwkv_g2=rwkv_g2[i], rwkv_k_k=rwkv_k_k[i],
              rwkv_k_a=rwkv_k_a[i], rwkv_r_k=rwkv_r_k[i], lnx_g=lnx_g[i], lnx_b=lnx_b[i], w_out=w_out[i],
              norm2_g=norm2_g[i], rg_w=router_group_w[i], rg_b=router_group_b[i], ri_w=router_inner_w[i],
              ri_b=router_inner_b[i], e_wg=expert_w_gate[i], e_wu=expert_w_up[i], e_wd=expert_w_down[i],
              ple_norm_g=ple_norm_g[i], ple_gate_w=ple_gate_w[i], ple_proj_w=ple_proj_w[i])
    lam_init = 0.8 - 0.6 * math.exp(-0.3 * i)
    yp, ys, k_new, v_new, wkv_fin, shift_p, shift_s = _trunk_layer(
        x_prompt[0], x_sample, p_prompt[i, 0], p_sample[i], cache_k[i], cache_v[i], state_wkv[i],
        state_shift[i], t5_table, lam_init, lp)
    hk = (N_HEADS_A, 2, HEAD_DIM_A)
    hv = (N_HEADS_A, 2 * HEAD_DIM_A)
    return (yp[None], ys,
            k_new[:t].reshape((1, 1, t) + hk), v_new[:t].reshape((1, 1, t) + hv),
            wkv_fin[:1][None], shift_p.reshape(1, 1, 1, -1),
            k_new[t:].reshape((1, nb, ts) + hk), v_new[t:].reshape((1, nb, ts) + hv),
            wkv_fin[1:][None], shift_s.reshape(1, nb, 1, -1))
```

```python
import functools
import math

import jax
import jax.numpy as jnp
from jax import lax
from jax.experimental import pallas as pl
from jax.experimental.pallas import tpu as pltpu

F32 = jnp.float32
BF16 = jnp.bfloat16

LANES = 128
SUBLANES = 8
VMEM_BYTES_V7X = 64 * 1024 * 1024
VMEM_LIMIT = VMEM_BYTES_V7X - 6 * 1024 * 1024

CHUNK = 64
HEAD_DIM_A = 128
N_HEADS_A = 8
ATT_WIDTH = 2 * HEAD_DIM_A * N_HEADS_A
T5_BUCKETS = 32
HEAD_B = 64
N_HEADS_B = 32
RWKV_WIDTH = HEAD_B * N_HEADS_B
RANK_W = 96
RANK_A = 96
RANK_G = 256
LORA_COLS = 512
LNX_EPS = 64e-5
N_GROUPS = 8
EXP_PER_GROUP = 8
N_EXPERTS = N_GROUPS * EXP_PER_GROUP
TOP_K = 2
RMS_EPS = 1e-6
NEG_INF = -1e30
T5_LOG_THRESHOLDS = (12, 16, 23, 32, 46, 64, 91)
T5_FAR = 128

K_LO = LANES // N_HEADS_B
K_HI = HEAD_B // K_LO
V_BLK = HEAD_B // SUBLANES
SCAN_TB = 32

EXPERT_ROWS = 256


def _cparams(sem, vmem=VMEM_LIMIT):
    return pltpu.CompilerParams(dimension_semantics=sem, vmem_limit_bytes=vmem)


def _exact_div(a, b):
    assert a % b == 0, (a, b)
    return a // b


def _sigmoid(x):
    return 1.0 / (1.0 + jnp.exp(-x))


def _part_blocks(parts, tm):
    edges = [0]
    for p in parts:
        edges.append(edges[-1] + _exact_div(p.shape[0], tm))
    return list(zip(edges[:-1], edges[1:]))


def _part_spec(block, lo, hi, row_axis_arg, const_index):
    def index_map(*grid):
        return (jnp.clip(grid[row_axis_arg] - lo, 0, hi - lo - 1),) + const_index(*grid)
    return pl.BlockSpec(block, index_map)


def _rmsnorm_kernel(*refs, ranges):
    x_refs, g_ref, o_ref = refs[:len(ranges)], refs[len(ranges)], refs[len(ranges) + 1]
    i = pl.program_id(0)
    for x_ref, (lo, hi) in zip(x_refs, ranges):
        @pl.when((i >= lo) & (i < hi))
        def _(x_ref=x_ref):
            x = x_ref[...]
            ms = jnp.mean(x * x, axis=-1, keepdims=True)
            o_ref[...] = (x * lax.rsqrt(ms + RMS_EPS) * g_ref[...]).astype(o_ref.dtype)


def rmsnorm_cast(parts, g, tm=256):
    parts = list(parts)
    d = parts[0].shape[1]
    for part in parts:
        tm = math.gcd(tm, part.shape[0])
    ranges = _part_blocks(parts, tm)
    return pl.pallas_call(
        functools.partial(_rmsnorm_kernel, ranges=ranges),
        out_shape=jax.ShapeDtypeStruct((ranges[-1][1] * tm, d), BF16),
        grid=(ranges[-1][1],),
        in_specs=[_part_spec((tm, d), lo, hi, 0, lambda i: (0,)) for lo, hi in ranges]
        + [pl.BlockSpec((1, d), lambda i: (0, 0))],
        out_specs=pl.BlockSpec((tm, d), lambda i: (i, 0)),
        compiler_params=_cparams(("arbitrary",)),
        name="rmsnorm_cast",
    )(*parts, g.reshape(1, d))


def _mm_kernel(*refs, n_pair, cast, mode, res_ranges, out_ranges):
    refs = list(refs)
    a_refs, w_refs = refs[:n_pair], refs[n_pair:2 * n_pair]
    del refs[:2 * n_pair]
    res_refs = [refs.pop(0) for _ in res_ranges]
    p_ref, pw_ref = (refs.pop(0), refs.pop(0)) if mode == "ple" else (None, None)
    o_refs = [refs.pop(0) for _ in out_ranges]
    wb_refs = [refs.pop(0) if cast[i] else w_refs[i] for i in range(n_pair)]
    i = pl.program_id(1)

    @pl.when(i == 0)
    def _():
        for k in range(n_pair):
            if cast[k]:
                wb_refs[k][...] = w_refs[k][...].astype(BF16)

    acc = jnp.dot(a_refs[0][...], wb_refs[0][...], preferred_element_type=F32)
    for k in range(1, n_pair):
        acc = acc + jnp.dot(a_refs[k][...], wb_refs[k][...], preferred_element_type=F32)
    if mode == "ple":
        acc = jnp.dot(p_ref[...], pw_ref[...].astype(BF16), preferred_element_type=F32) * _sigmoid(acc)

    def finish(res_ref):
        val = acc if res_ref is None else res_ref[...] + acc
        for o_ref, (lo, hi) in zip(o_refs, out_ranges):
            if len(o_refs) == 1:
                o_ref[...] = val.astype(o_ref.dtype)
            else:
                @pl.when((i >= lo) & (i < hi))
                def _(o_ref=o_ref):
                    o_ref[...] = val.astype(o_ref.dtype)

    if not res_refs:
        finish(None)
    elif len(res_refs) == 1:
        finish(res_refs[0])
    else:
        for res_ref, (lo, hi) in zip(res_refs, res_ranges):
            @pl.when((i >= lo) & (i < hi))
            def _(res_ref=res_ref):
                finish(res_ref)


def matmul(pairs, n_cols, *, mode="plain", res=None, p=None, pw=None, out_dtype=F32, out_rows=None, tm=1024, tn=512):
    m = pairs[0][0].shape[0]
    res_parts = [] if res is None else (list(res) if isinstance(res, (list, tuple)) else [res])
    out_rows = [m] if out_rows is None else list(out_rows)
    for r in [m] + [part.shape[0] for part in res_parts] + out_rows:
        tm = math.gcd(tm, r)
    assert n_cols % tn == 0
    in_specs = [pl.BlockSpec((tm, a.shape[1]), lambda n, i: (i, 0)) for a, _, _, _ in pairs]
    in_specs += [pl.BlockSpec((a.shape[1], tn), lambda n, i, rb=rb, cb=cb: (rb, n + cb)) for a, _, rb, cb in pairs]
    args = [a for a, _, _, _ in pairs] + [w for _, w, _, _ in pairs]
    cast = tuple(w.dtype != BF16 for _, w, _, _ in pairs)
    res_ranges = _part_blocks(res_parts, tm)
    in_specs += [_part_spec((tm, tn), lo, hi, 1, lambda n, i: (n,)) for lo, hi in res_ranges]
    args += res_parts
    if mode == "ple":
        kp = p.shape[1]
        in_specs += [pl.BlockSpec((tm, kp), lambda n, i: (i, 0)), pl.BlockSpec((kp, tn), lambda n, i: (0, n))]
        args += [p, pw]
    out_ranges = _part_blocks([jax.ShapeDtypeStruct((r, n_cols), out_dtype) for r in out_rows], tm)
    assert out_ranges[-1][1] * tm == m
    outs = pl.pallas_call(
        functools.partial(_mm_kernel, n_pair=len(pairs), cast=cast, mode=mode, res_ranges=res_ranges,
                          out_ranges=out_ranges),
        out_shape=tuple(jax.ShapeDtypeStruct((r, n_cols), out_dtype) for r in out_rows),
        grid=(n_cols // tn, m // tm),
        in_specs=in_specs,
        out_specs=tuple(_part_spec((tm, tn), lo, hi, 1, lambda n, i: (n,)) for lo, hi in out_ranges),
        scratch_shapes=[pltpu.VMEM((a.shape[1], tn), BF16) for (a, _, _, _), c in zip(pairs, cast) if c],
        compiler_params=_cparams(("arbitrary", "arbitrary")),
        name="matmul_" + mode,
    )(*args)
    return outs[0] if len(outs) == 1 else outs


def _perm_cols_kernel(w_ref, p_ref, o_ref):
    o_ref[...] = jnp.dot(w_ref[...].astype(BF16), p_ref[...], preferred_element_type=F32).astype(BF16)


def permute_weight_cols(w, perm_mat, col_block0, n_sections, tm=512):
    k = w.shape[0]
    c = RWKV_WIDTH
    return pl.pallas_call(
        _perm_cols_kernel,
        out_shape=jax.ShapeDtypeStruct((k, n_sections * c), BF16),
        grid=(n_sections, _exact_div(k, tm)),
        in_specs=[pl.BlockSpec((tm, c), lambda sec, i: (i, col_block0 + sec)),
                  pl.BlockSpec((c, c), lambda sec, i: (0, 0))],
        out_specs=pl.BlockSpec((tm, c), lambda sec, i: (i, sec)),
        compiler_params=_cparams(("parallel", "parallel")),
        name="permute_weight_cols",
    )(w, perm_mat)


def _qk_norm_kernel(q_ref, k_ref, v_ref, qg_ref, kg_ref, qo_ref, ko_ref, kbo_ref, vo_ref):
    def head_norm(x, g):
        ms = jnp.mean(x * x, axis=-1, keepdims=True)
        return x * lax.rsqrt(ms + RMS_EPS) * g

    qg = qg_ref[...]
    kg = kg_ref[...]
    for c in range(ATT_WIDTH // HEAD_DIM_A):
        sl = slice(c * HEAD_DIM_A, (c + 1) * HEAD_DIM_A)
        qn = head_norm(q_ref[:, sl], qg)
        qo_ref[:, sl] = (qn * (LOG2E * HEAD_DIM_A ** -0.5)).astype(BF16)
        kn = head_norm(k_ref[:, sl], kg)
        ko_ref[:, sl] = kn
        kbo_ref[:, sl] = kn.astype(BF16)
    vo_ref[...] = v_ref[...].astype(BF16)


def qk_norm(proj_qkv, q_g, k_g, tm=256):
    m = proj_qkv.shape[0]
    w = ATT_WIDTH
    blk = lambda c: pl.BlockSpec((tm, w), lambda i, c=c: (i, c))
    vec = pl.BlockSpec((1, HEAD_DIM_A), lambda i: (0, 0))
    out_blk = pl.BlockSpec((tm, w), lambda i: (i, 0))
    return pl.pallas_call(
        _qk_norm_kernel,
        out_shape=(jax.ShapeDtypeStruct((m, w), BF16), jax.ShapeDtypeStruct((m, w), F32),
                   jax.ShapeDtypeStruct((m, w), BF16), jax.ShapeDtypeStruct((m, w), BF16)),
        grid=(_exact_div(m, tm),),
        in_specs=[blk(0), blk(1), blk(2), vec, vec],
        out_specs=(out_blk, out_blk, out_blk, out_blk),
        compiler_params=_cparams(("parallel",)),
        name="qk_norm",
    )(proj_qkv, proj_qkv, proj_qkv, q_g.reshape(1, -1), k_g.reshape(1, -1))


def _bias_kernel(tab_ref, o_ref, *, rel0, masked, key_major, n_valid):
    _, nr, nc = o_ref.shape
    r = lax.broadcasted_iota(jnp.int32, (nr, nc), 0)
    c = lax.broadcasted_iota(jnp.int32, (nr, nc), 1)
    kpos, qpos = (r, c) if key_major else (c, r)
    rel = rel0 + kpos - qpos
    n = jnp.abs(rel)
    large = jnp.full((nr, nc), T5_BUCKETS // 4, jnp.int32)
    for thr in T5_LOG_THRESHOLDS:
        large = large + jnp.where(n >= thr, 1, 0)
    bucket = jnp.where(n < T5_BUCKETS // 4, n, large) + jnp.where(rel > 0, T5_BUCKETS // 2, 0)
    if masked:
        visible = (kpos // CHUNK) <= (qpos // CHUNK)
    for h in range(N_HEADS_A):
        acc = jnp.zeros((nr, nc), F32)
        for b in range(T5_BUCKETS):
            acc = jnp.where(bucket == b, tab_ref[b, h] * LOG2E, acc)
        if masked:
            acc = jnp.where(visible, acc, NEG_INF)
        if n_valid is not None:
            acc = jnp.where(c < n_valid, acc, NEG_INF)
        o_ref[h] = acc


def bias_tiles(table, nr, nc, *, rel0, masked=False, key_major=False, n_valid=None):
    return pl.pallas_call(
        functools.partial(_bias_kernel, rel0=rel0, masked=masked, key_major=key_major, n_valid=n_valid),
        out_shape=jax.ShapeDtypeStruct((N_HEADS_A, nr, nc), F32),
        in_specs=[pl.BlockSpec(memory_space=pltpu.SMEM)],
        out_specs=pl.BlockSpec(memory_space=pltpu.VMEM),
        compiler_params=_cparams(None),
        name="t5_bias",
    )(table)


def _lambda_value(lq1, lk1, lq2, lk2, lam_init):
    s1 = jnp.sum(lq1 * lk1, axis=-1, keepdims=True)
    s2 = jnp.sum(lq2 * lk2, axis=-1, keepdims=True)
    return jnp.exp(s1) - jnp.exp(s2) + lam_init


def _online_step(s, bias, v, m_prev, l_prev, acc_prev):
    nchunk = max(s.shape[1] // LANES, 1)
    width = s.shape[1] // nchunk
    sc = [s[:, c * width:(c + 1) * width] + bias(c * width, width) for c in range(nchunk)]
    m_new = jnp.maximum(m_prev, jnp.max(functools.reduce(jnp.maximum, sc), axis=-1, keepdims=True))
    alpha = jnp.exp2(m_prev - m_new)
    p = [jnp.exp2(x - m_new) for x in sc]
    l_new = alpha * l_prev + jnp.sum(functools.reduce(lambda a, b: a + b, p), axis=-1, keepdims=True)
    pb = p[0].astype(BF16) if nchunk == 1 else jnp.concatenate([x.astype(BF16) for x in p], axis=1)
    return m_new, l_new, alpha * acc_prev + jnp.dot(pb, v, preferred_element_type=F32)


def _diff_finish(acc1, l1, acc2, l2, lam, g, lam_init):
    o = acc1 / l1 - lam * (acc2 / l2)
    ms = jnp.mean(o * o, axis=-1, keepdims=True)
    return (o * lax.rsqrt(ms + RMS_EPS) * g) * (1.0 - lam_init)


_NT = (((1,), (1,)), ((), ()))
BF16_SUBLANES = 16
VT_ROWS = 2 * HEAD_DIM_A + BF16_SUBLANES
LOG2E = math.log2(math.e)


def _prompt_attn_kernel(far_ref, q_ref, k_ref, vt_ref, bd_ref, bl_ref, lq1, lk1, lq2, lk2, g_ref, ya_hbm,
                        o_ref, m_ref, acc_ref, *, blk, nsub, lam_init):
    h = pl.program_id(0)
    i = pl.program_id(1)
    m_ref[...] = jnp.full(m_ref.shape, NEG_INF, F32)
    acc_ref[...] = jnp.zeros(acc_ref.shape, F32)
    d = HEAD_DIM_A
    hw = 2 * d
    far_bias = far_ref[h]

    def chain(st, bias_tile, vt, idx):
        shift = far_bias if bias_tile is None else None
        if bias_tile is not None:
            st = st + bias_tile[0]
        col_max = jnp.max(st, axis=0, keepdims=True)
        if shift is not None:
            col_max = col_max + shift
        m_prev = m_ref[idx]
        m_new = jnp.maximum(m_prev, col_max)
        alpha = jnp.exp2(m_prev - m_new)
        p = jnp.exp2(st - (m_new if shift is None else m_new - shift))
        acc_ref[idx] = alpha * acc_ref[idx] + jnp.dot(vt, p.astype(BF16), preferred_element_type=F32)
        m_ref[idx] = m_new

    def update(j, ahead):
        kb = k_ref[pl.ds(pl.multiple_of(j * blk, blk), blk), :]
        vt = vt_ref[j]
        todo = []
        for sa in range(nsub):
            if ahead[sa] < 0:
                continue
            bias_tile = None if ahead[sa] >= 2 else (bl_ref if ahead[sa] == 1 else bd_ref)
            for mp in range(2):
                st = lax.dot_general(kb[:, mp * d:(mp + 1) * d], q_ref[sa * blk:(sa + 1) * blk, mp * d:(mp + 1) * d],
                                     _NT, preferred_element_type=F32)
                todo.append((st, bias_tile, 2 * sa + mp))
        for st, bias_tile, idx in todo:
            chain(st, bias_tile, vt, idx)

    def far_body(j, carry):
        update(j, [2] * nsub)
        return carry

    lax.fori_loop(0, jnp.maximum(nsub * i - 1, 0), far_body, 0)

    @pl.when(i >= 1)
    def _():
        update(nsub * i - 1, [sa + 1 for sa in range(nsub)])

    for o in range(nsub):
        update(nsub * i + o, [sa - o for sa in range(nsub)])
    lam = _lambda_value(lq1[...], lk1[...], lq2[...], lk2[...], lam_init)
    for sa in range(nsub):
        a1 = acc_ref[2 * sa]
        a2 = acc_ref[2 * sa + 1]
        ot = a1[:hw] / a1[hw:hw + 1] - lam * (a2[:hw] / a2[hw:hw + 1])
        ms = jnp.mean(ot * ot, axis=0, keepdims=True)
        yt = ot * lax.rsqrt(ms + RMS_EPS)
        o_ref[sa * blk:(sa + 1) * blk, :] = ((yt.T * g_ref[...]) * (1.0 - lam_init)).astype(BF16)


def prompt_attention(qn, kn, vt, ya, t, bias_d, bias_l, far_bias, lams, subln_g, lam_init, blk, nsub=2):
    nsub = min(nsub, t // blk)
    bq = nsub * blk
    hw = 2 * HEAD_DIM_A
    vec = pl.BlockSpec((1, HEAD_DIM_A), lambda h, i, far: (0, 0))
    return pl.pallas_call(
        functools.partial(_prompt_attn_kernel, blk=blk, nsub=nsub, lam_init=lam_init),
        out_shape=jax.ShapeDtypeStruct(ya.shape, BF16),
        input_output_aliases={11: 0},
        grid_spec=pltpu.PrefetchScalarGridSpec(
            num_scalar_prefetch=1,
            grid=(N_HEADS_A, _exact_div(t, bq)),
            in_specs=[pl.BlockSpec((bq, hw), lambda h, i, far: (i, h)),
                      pl.BlockSpec((t, hw), lambda h, i, far: (0, h)),
                      pl.BlockSpec((t // blk, VT_ROWS, blk), lambda h, i, far: (0, h, 0)),
                      pl.BlockSpec((1, blk, blk), lambda h, i, far: (h, 0, 0)),
                      pl.BlockSpec((1, blk, blk), lambda h, i, far: (h, 0, 0)),
                      vec, vec, vec, vec,
                      pl.BlockSpec((1, hw), lambda h, i, far: (0, 0)),
                      pl.BlockSpec(memory_space=pl.ANY)],
            out_specs=pl.BlockSpec((bq, hw), lambda h, i, far: (i, h)),
            scratch_shapes=[pltpu.VMEM((2 * nsub, 1, blk), F32), pltpu.VMEM((2 * nsub, VT_ROWS, blk), F32)]),
        compiler_params=_cparams(("arbitrary", "arbitrary")),
        name="prompt_attention",
    )(far_bias, qn, kn, vt, bias_d, bias_l, *lams, subln_g.reshape(1, hw), ya)


def _sublane_transpose8(tiles, sub):
    a = list(tiles)
    for dist in (4, 2, 1):
        keep = (sub % (2 * dist)) < dist
        nxt = list(a)
        for i in range(SUBLANES):
            if i % (2 * dist) < dist:
                x, y = a[i], a[i + dist]
                nxt[i] = jnp.where(keep, x, pltpu.roll(y, dist, axis=0))
                nxt[i + dist] = jnp.where(keep, pltpu.roll(x, SUBLANES - dist, axis=0), y)
        a = nxt
    return a


def _sample_attn_kernel(q_ref, ck_ref, cv_ref, kn_ref, vn_ref, bp_ref, bn_ref, lq1, lk1, lq2, lk2, g_ref,
                        ya_hbm, o_ref, m_ref, l_ref, acc_ref, ks_ref, vs_ref, *, lam_init):
    t = pl.program_id(1)
    last = t == pl.num_programs(1) - 1
    d = HEAD_DIM_A
    tk = ck_ref.shape[1]
    ts = q_ref.shape[0]
    n = 2 * N_HEADS_A
    sub = lax.broadcasted_iota(jnp.int32, (SUBLANES, LANES), 0)

    @pl.when(t == 0)
    def _():
        m_ref[...] = jnp.full(m_ref.shape, NEG_INF, F32)
        l_ref[...] = jnp.zeros(l_ref.shape, F32)
        acc_ref[...] = jnp.zeros(acc_ref.shape, F32)
        ks_ref[:, pl.ds(tk, LANES), :] = jnp.zeros((n, LANES, d), BF16)
        vs_ref[:, pl.ds(tk, LANES), :] = jnp.zeros((N_HEADS_A, LANES, 2 * d), BF16)
        for c in range(n):
            ks_ref[c, pl.ds(tk, ts), :] = kn_ref[:, c * d:(c + 1) * d]
        for h in range(N_HEADS_A):
            vs_ref[h, pl.ds(tk, ts), :] = vn_ref[:, 2 * h * d:2 * (h + 1) * d]

    def to_head_major(src, dst, p0, lanes):
        halves = [_sublane_transpose8(src[SUBLANES * a:SUBLANES * (a + 1)], sub) for a in range(2)]
        for r in range(SUBLANES):
            dst(r)[pl.ds(p0, BF16_SUBLANES), lanes] = jnp.concatenate([halves[0][r], halves[1][r]], 0).astype(BF16)

    def relayout(g, carry):
        p0 = pl.multiple_of(g * BF16_SUBLANES, BF16_SUBLANES)
        kt = ck_ref[0, pl.ds(p0, BF16_SUBLANES), :, :]
        vt = cv_ref[0, pl.ds(p0, BF16_SUBLANES), :, :]
        for a in range(2):
            rows = slice(a * SUBLANES, (a + 1) * SUBLANES)
            to_head_major([kt[p, rows, :] for p in range(BF16_SUBLANES)], lambda r, a=a: ks_ref.at[a * SUBLANES + r],
                          p0, slice(None))
            lanes = slice(a * LANES, (a + 1) * LANES)
            to_head_major([vt[p, :, lanes] for p in range(BF16_SUBLANES)], lambda r: vs_ref.at[r], p0, lanes)
        return carry

    lax.fori_loop(0, tk // BF16_SUBLANES, relayout, 0)

    state = [(m_ref[c], l_ref[c], acc_ref[c]) for c in range(n)]
    logits = [lax.dot_general(q_ref[:, c * d:(c + 1) * d], ks_ref[c], _NT, preferred_element_type=F32)
              for c in range(n)]
    for c in range(n):
        h = c // 2

        def bias(k0, w, h=h):
            if k0 < tk:
                return bp_ref[h, :, k0:k0 + w]
            return jnp.where(last, bn_ref[h], NEG_INF)

        state[c] = _online_step(logits[c], bias, vs_ref[h], *state[c])
    for c in range(n):
        m_ref[c], l_ref[c], acc_ref[c] = state[c]

    @pl.when(last)
    def _():
        lam = _lambda_value(lq1[...], lk1[...], lq2[...], lk2[...], lam_init)
        for h in range(N_HEADS_A):
            y = _diff_finish(acc_ref[2 * h], l_ref[2 * h], acc_ref[2 * h + 1], l_ref[2 * h + 1], lam,
                             g_ref[...], lam_init)
            o_ref[:, 2 * h * d:2 * (h + 1) * d] = y.astype(BF16)


def sample_attention(qn, kn, vn, ya, row0, cache_k, cache_v, bias_past, bias_new, lams, subln_g, lam_init, tk=1024):
    nb, past = cache_k.shape[:2]
    w = ATT_WIDTH
    ts = bias_new.shape[1]
    tk = min(tk, past)
    blk0 = row0 // ts
    hw = 2 * HEAD_DIM_A
    vec = pl.BlockSpec((1, HEAD_DIM_A), lambda b, t: (0, 0))
    new_rows = pl.BlockSpec((ts, w), lambda b, t: (blk0 + b, 0))
    return pl.pallas_call(
        functools.partial(_sample_attn_kernel, lam_init=lam_init),
        out_shape=jax.ShapeDtypeStruct(ya.shape, BF16),
        grid=(nb, _exact_div(past, tk)),
        in_specs=[new_rows,
                  pl.BlockSpec((1, tk, 2 * N_HEADS_A, HEAD_DIM_A), lambda b, t: (b, t, 0, 0)),
                  pl.BlockSpec((1, tk, N_HEADS_A, hw), lambda b, t: (b, t, 0, 0)),
                  new_rows, new_rows,
                  pl.BlockSpec((N_HEADS_A, ts, tk), lambda b, t: (0, 0, t)),
                  pl.BlockSpec((N_HEADS_A, ts, LANES), lambda b, t: (0, 0, 0)),
                  vec, vec, vec, vec,
                  pl.BlockSpec((1, hw), lambda b, t: (0, 0)),
                  pl.BlockSpec(memory_space=pl.ANY)],
        out_specs=pl.BlockSpec((ts, w), lambda b, t: (blk0 + b, 0)),
        input_output_aliases={12: 0},
        scratch_shapes=[pltpu.VMEM((2 * N_HEADS_A, ts, 1), F32), pltpu.VMEM((2 * N_HEADS_A, ts, 1), F32),
                        pltpu.VMEM((2 * N_HEADS_A, ts, hw), F32),
                        pltpu.VMEM((2 * N_HEADS_A, tk + LANES, HEAD_DIM_A), BF16),
                        pltpu.VMEM((N_HEADS_A, tk + LANES, hw), BF16)],
        compiler_params=_cparams(("arbitrary", "arbitrary")),
        name="sample_attention",
    )(qn, cache_k, cache_v, kn, vn, bias_past, bias_new, *lams, subln_g.reshape(1, hw), ya)


def _group_allreduce(x):
    x = x + pltpu.roll(x, N_HEADS_B, axis=1)
    return x + pltpu.roll(x, 2 * N_HEADS_B, axis=1)


def _head_sum128(x):
    acc = x[:, 0:LANES]
    for c in range(1, K_HI):
        acc = acc + x[:, c * LANES:(c + 1) * LANES]
    return _group_allreduce(acc)


def _tile16(x128):
    return jnp.concatenate([x128] * K_HI, axis=1)


def _token_shift(x_ref, start_ref, buf_ref, mu_ref):
    tm = x_ref.shape[0]
    x = x_ref[...]
    buf_ref[pl.ds(0, SUBLANES), :] = jnp.zeros((SUBLANES, x.shape[1]), F32)
    buf_ref[pl.ds(SUBLANES, tm), :] = x
    shifted = buf_ref[pl.ds(SUBLANES - 1, tm), :]
    first = lax.broadcasted_iota(jnp.int32, (SCAN_TB, 1), 0) == 0
    prev = jnp.concatenate([jnp.where(first, start_ref[0, g:g + 1, :], shifted[g * SCAN_TB:(g + 1) * SCAN_TB])
                            for g in range(tm // SCAN_TB)], 0)
    return x + (prev - x) * mu_ref[...]


def _rwkv_prep_kernel(f_ref, sf_ref, lo_ref, slo_ref, mu_ref, mul_ref, w0_ref, a0_ref, kk_ref, ka_ref, rk_ref,
                      w2_ref, a2_ref, g2_ref,
                      ak_o, wr_o, w_o, b_o, km_o, v_o, g_o, br_o, kr_o, bo_o, fbuf, lbuf):
    c = RWKV_WIDTH
    xm = _token_shift(f_ref, sf_ref, fbuf, mu_ref)
    xl = _token_shift(lo_ref, slo_ref, lbuf, mul_ref)
    r, k, v = xm[:, :c], xm[:, c:2 * c], xm[:, 2 * c:]
    wd, ad, gd = xl[:, :LANES], xl[:, LANES:2 * LANES], xl[:, 2 * LANES:]
    lw = w0_ref[...] + jnp.dot(jnp.tanh(wd).astype(BF16), w2_ref[...].astype(BF16), preferred_element_type=F32)
    z = -lw
    softplus = jnp.maximum(z, 0.0) + jnp.log(1.0 + jnp.exp(-jnp.abs(z)))
    decay = jnp.exp(-jnp.exp(-softplus - 0.5))
    a = _sigmoid(a0_ref[...] + jnp.dot(ad.astype(BF16), a2_ref[...].astype(BF16), preferred_element_type=F32))
    g_o[...] = jnp.dot(_sigmoid(gd).astype(BF16), g2_ref[...].astype(BF16), preferred_element_type=F32)
    kk = k * kk_ref[...]
    norm = jnp.maximum(jnp.sqrt(_tile16(_head_sum128(kk * kk))), 1e-12)
    kk = kk / norm
    kmod = k * (1.0 + (a - 1.0) * ka_ref[...])
    bvec = kk * a
    ak_o[...] = -kk
    wr_o[...] = decay * r
    w_o[...] = decay
    b_o[...] = bvec
    km_o[...] = kmod
    v_o[...] = v
    br_o[...] = _head_sum128(bvec * r)
    kr_o[...] = _head_sum128(kmod * r)
    bo_o[...] = _head_sum128(r * kmod * rk_ref[...])


def rwkv_prep(feat, start, lora, start_lora, prm, tm=128):
    m = feat.shape[0]
    c = RWKV_WIDTH
    row = lambda w: pl.BlockSpec((tm, w), lambda i: (i, 0))
    st = lambda w: pl.BlockSpec((1, SUBLANES, w), lambda i: (i, 0, 0))
    vec = lambda w: pl.BlockSpec((1, w), lambda i: (0, 0))
    mat = lambda r: pl.BlockSpec((r, c), lambda i: (0, 0))
    big = jax.ShapeDtypeStruct((m, c), F32)
    small = jax.ShapeDtypeStruct((m, LANES), F32)
    assert tm // SCAN_TB <= SUBLANES
    return pl.pallas_call(
        _rwkv_prep_kernel,
        out_shape=(big,) * 7 + (small,) * 3,
        grid=(_exact_div(m, tm),),
        in_specs=[row(3 * c), st(3 * c), row(LORA_COLS), st(LORA_COLS), vec(3 * c), vec(LORA_COLS),
                  vec(c), vec(c), vec(c), vec(c), vec(c), mat(LANES), mat(LANES), mat(RANK_G)],
        out_specs=(row(c),) * 7 + (row(LANES),) * 3,
        scratch_shapes=[pltpu.VMEM((tm + SUBLANES, 3 * c), F32), pltpu.VMEM((tm + SUBLANES, LORA_COLS), F32)],
        compiler_params=_cparams(("arbitrary",)),
        name="rwkv_prep",
    )(feat, start, lora, start_lora, prm["mu_rkv"], prm["mu_lora"], prm["w0"], prm["a0"], prm["k_k"], prm["k_a"],
      prm["r_k"], prm["w2"], prm["a2"], prm["g2"])


def _scan_kernel(seq_ref, first_ref, last_ref, ak_ref, wr_ref, w_ref, b_ref, km_ref, v_ref, br_ref, s0_ref,
                 y_ref, sout_ref, s_ref):
    step = pl.program_id(0)

    @pl.when(first_ref[step] == 1)
    def _():
        s_ref[...] = s0_ref[0]

    sub = lax.broadcasted_iota(jnp.int32, (SUBLANES, LANES), 0)
    grp = lax.broadcasted_iota(jnp.int32, (SUBLANES, LANES), 1) // N_HEADS_B
    own_group = grp == (sub % K_LO)
    low_half = sub < K_LO

    def row(ref, t8, s, c):
        tile = ref[pl.ds(t8, SUBLANES), c * LANES:(c + 1) * LANES]
        return jnp.broadcast_to(tile[s:s + 1], (SUBLANES, LANES))

    def time_step(t8, s, y_lo, y_hi):
        vt = []
        for vb in range(V_BLK):
            tile = jnp.where(low_half, row(v_ref, t8, s, 2 * vb), row(v_ref, t8, s, 2 * vb + 1))
            vt.append(_group_allreduce(jnp.where(own_group, tile, 0.0)))
        acc_u = [jnp.zeros((SUBLANES, LANES), F32) for _ in range(V_BLK)]
        acc_y = [jnp.zeros((SUBLANES, LANES), F32) for _ in range(V_BLK)]
        for kh in range(K_HI):
            a_row = row(ak_ref, t8, s, kh)
            wr_row = row(wr_ref, t8, s, kh)
            for vb in range(V_BLK):
                st = s_ref[vb, kh]
                acc_u[vb] = acc_u[vb] + st * a_row
                acc_y[vb] = acc_y[vb] + st * wr_row
        br_row = row(br_ref, t8, s, 0)
        u = [_group_allreduce(x) for x in acc_u]
        for vb in range(V_BLK):
            y = _group_allreduce(acc_y[vb]) + u[vb] * br_row
            ym = jnp.where(own_group, y, 0.0)
            ym = ym + pltpu.roll(ym, 1, axis=0)
            ym = ym + pltpu.roll(ym, 2, axis=0)
            y_lo[vb] = jnp.where(sub == s, pltpu.roll(ym, (s - (K_LO - 1)) % SUBLANES, axis=0), y_lo[vb])
            y_hi[vb] = jnp.where(sub == s, pltpu.roll(ym, (s - (2 * K_LO - 1)) % SUBLANES, axis=0), y_hi[vb])
        for kh in range(K_HI):
            w_row = row(w_ref, t8, s, kh)
            b_row = row(b_ref, t8, s, kh)
            km_row = row(km_ref, t8, s, kh)
            for vb in range(V_BLK):
                s_ref[vb, kh] = s_ref[vb, kh] * w_row + (b_row * u[vb] + km_row * vt[vb])

    def eight_steps(gi, carry):
        t8 = pl.multiple_of(gi * SUBLANES, SUBLANES)
        y_lo = [jnp.zeros((SUBLANES, LANES), F32) for _ in range(V_BLK)]
        y_hi = [jnp.zeros((SUBLANES, LANES), F32) for _ in range(V_BLK)]
        for s in range(SUBLANES):
            time_step(t8, s, y_lo, y_hi)
        for vb in range(V_BLK):
            y_ref[pl.ds(t8, SUBLANES), (2 * vb) * LANES:(2 * vb + 1) * LANES] = y_lo[vb]
            y_ref[pl.ds(t8, SUBLANES), (2 * vb + 1) * LANES:(2 * vb + 2) * LANES] = y_hi[vb]
        return carry

    lax.fori_loop(0, y_ref.shape[0] // SUBLANES, eight_steps, 0)

    @pl.when(last_ref[step] == 1)
    def _():
        sout_ref[0] = s_ref[...]


def rwkv_scan(ops, br, s0, seq_of_step, first, last):
    m, c = ops[0].shape
    nseq = s0.shape[0]
    tb = SCAN_TB
    row = lambda w: pl.BlockSpec((tb, w), lambda i, sq, fi, la: (i, 0))
    st = pl.BlockSpec((1, V_BLK, K_HI, SUBLANES, LANES), lambda i, sq, fi, la: (sq[i], 0, 0, 0, 0))
    return pl.pallas_call(
        _scan_kernel,
        out_shape=(jax.ShapeDtypeStruct((m, c), F32), jax.ShapeDtypeStruct(s0.shape, F32)),
        grid_spec=pltpu.PrefetchScalarGridSpec(
            num_scalar_prefetch=3,
            grid=(m // tb,),
            in_specs=[row(c)] * 6 + [row(LANES), st],
            out_specs=(row(c), st),
            scratch_shapes=[pltpu.VMEM((V_BLK, K_HI, SUBLANES, LANES), F32)]),
        compiler_params=_cparams(("arbitrary",)),
        name="rwkv_scan",
    )(seq_of_step, first, last, *ops, br, s0)


def _rwkv_post_kernel(y_ref, v_ref, g_ref, kr_ref, bo_ref, lg_ref, lb_ref, o_ref):
    v = v_ref[...]
    y = y_ref[...] + v * _tile16(kr_ref[...])
    mean = _tile16(_head_sum128(y)) * (1.0 / HEAD_B)
    yc = y - mean
    var = _tile16(_head_sum128(yc * yc)) * (1.0 / HEAD_B)
    yn = yc * lax.rsqrt(var + LNX_EPS) * lg_ref[...] + lb_ref[...]
    o_ref[...] = ((yn + _tile16(bo_ref[...]) * v) * g_ref[...]).astype(BF16)


def rwkv_post(y, v, g, kr, bo, lnx_g, lnx_b, tm=256):
    m, c = y.shape
    row = lambda w: pl.BlockSpec((tm, w), lambda i: (i, 0))
    vec = pl.BlockSpec((1, c), lambda i: (0, 0))
    return pl.pallas_call(
        _rwkv_post_kernel,
        out_shape=jax.ShapeDtypeStruct((m, c), BF16),
        grid=(_exact_div(m, tm),),
        in_specs=[row(c), row(c), row(c), row(LANES), row(LANES), vec, vec],
        out_specs=row(c),
        compiler_params=_cparams(("parallel",)),
        name="rwkv_post",
    )(y, v, g, kr, bo, lnx_g, lnx_b)


def _rows_to_token_tiles(src_ref, dst_ref):
    rows, cols = src_ref.shape
    sub = lax.broadcasted_iota(jnp.int32, (SUBLANES, LANES), 0)

    def body(g, carry):
        r0 = pl.multiple_of(g * SUBLANES, SUBLANES)
        for a in range(cols // (SUBLANES * LANES)):
            tiles = [src_ref[pl.ds(r0, SUBLANES), (SUBLANES * a + i) * LANES:(SUBLANES * a + i + 1) * LANES]
                     for i in range(SUBLANES)]
            out = _sublane_transpose8(tiles, sub)
            for p in range(SUBLANES):
                dst_ref[r0 + p, SUBLANES * a:SUBLANES * (a + 1), :] = out[p]
        return carry

    lax.fori_loop(0, rows // SUBLANES, body, 0)


def _token_tiles_to_rows(src, n_rows, chunks, emit):
    sub = lax.broadcasted_iota(jnp.int32, (SUBLANES, LANES), 0)

    def body(g, carry):
        r0 = pl.multiple_of(g * BF16_SUBLANES, BF16_SUBLANES)
        for a in range(chunks // SUBLANES):
            halves = [_sublane_transpose8([src(r0 + SUBLANES * b + p)[SUBLANES * a:SUBLANES * (a + 1), :]
                                           for p in range(SUBLANES)], sub) for b in range(2)]
            for i in range(SUBLANES):
                emit(r0, SUBLANES * a + i, jnp.concatenate([halves[0][i], halves[1][i]], 0))
        return carry

    lax.fori_loop(0, n_rows // BF16_SUBLANES, body, 0)


def _router_kernel(x_ref, g_ref, rw_ref, rb_ref, h_ref, route_ref, hbuf):
    x = x_ref[...]
    ms = jnp.mean(x * x, axis=-1, keepdims=True)
    h = x * lax.rsqrt(ms + RMS_EPS) * g_ref[...]
    hbuf[...] = h
    _rows_to_token_tiles(hbuf, h_ref)
    logits = jnp.dot(h, rw_ref[...], precision=lax.Precision.HIGHEST, preferred_element_type=F32) + rb_ref[...]
    lane = lax.broadcasted_iota(jnp.int32, logits.shape, 1)
    big = jnp.int32(LANES)

    def first_argmax(vals, valid):
        masked = jnp.where(valid, vals, -jnp.inf)
        mx = jnp.max(masked, axis=-1, keepdims=True)
        idx = jnp.min(jnp.where(valid & (masked == mx), lane, big), axis=-1, keepdims=True)
        return mx, idx

    is_group = lane < N_GROUPS
    g_max, g_idx = first_argmax(logits, is_group)
    g_top = 1.0 / jnp.sum(jnp.where(is_group, jnp.exp(logits - g_max), 0.0), axis=-1, keepdims=True)
    in_group = (lane >= N_GROUPS) & (lane < N_GROUPS + N_EXPERTS) & ((lane - N_GROUPS) // EXP_PER_GROUP == g_idx)
    i_max, idx1 = first_argmax(logits, in_group)
    z = jnp.sum(jnp.where(in_group, jnp.exp(logits - i_max), 0.0), axis=-1, keepdims=True)
    i_max2, idx2 = first_argmax(logits, in_group & (lane != idx1))
    p1 = 1.0 / z
    p2 = jnp.exp(i_max2 - i_max) / z
    psum = p1 + p2
    gate1 = g_top * p1 / psum
    gate2 = g_top * p2 / psum
    route = jnp.where(lane == 0, (idx1 - N_GROUPS).astype(F32),
                      jnp.where(lane == 1, (idx2 - N_GROUPS).astype(F32),
                                jnp.where(lane == 2, gate1, jnp.where(lane == 3, gate2, 0.0))))
    route_ref[...] = route


def norm_router(x, g, rw, rb, tm=256):
    m, d = x.shape
    return pl.pallas_call(
        _router_kernel,
        out_shape=(jax.ShapeDtypeStruct((m, d // LANES, LANES), F32), jax.ShapeDtypeStruct((m, LANES), F32)),
        grid=(_exact_div(m, tm),),
        in_specs=[pl.BlockSpec((tm, d), lambda i: (i, 0)), pl.BlockSpec((1, d), lambda i: (0, 0)),
                  pl.BlockSpec((d, LANES), lambda i: (0, 0)), pl.BlockSpec((1, LANES), lambda i: (0, 0))],
        out_specs=(pl.BlockSpec((tm, d // LANES, LANES), lambda i: (i, 0, 0)),
                   pl.BlockSpec((tm, LANES), lambda i: (i, 0))),
        scratch_shapes=[pltpu.VMEM((tm, d), F32)],
        compiler_params=_cparams(("arbitrary",)),
        name="norm_router",
    )(x, g.reshape(1, d), rw, rb)


def _row_copy(src_hbm, buf, sem, slot, src_row, dst_row):
    return pltpu.make_async_copy(src_hbm.at[src_row], buf.at[slot, dst_row], sem.at[slot])


def _wait_slot(src_hbm, buf, sem, slot):
    pltpu.make_async_copy(src_hbm.at[pl.ds(0, buf.shape[1])], buf.at[slot], sem.at[slot]).wait()


ROW_ISSUE_UNROLL = 8


def _gather_rows_kernel(idx_ref, nused_ref, src_hbm, o_ref, buf, sem, *, rows):
    b = pl.program_id(0)
    nb = pl.num_programs(0)
    used = nused_ref[0]

    def issue(blk, slot):
        def body(r, c):
            _row_copy(src_hbm, buf, sem, slot, idx_ref[blk * rows + r], r).start()
            return c
        lax.fori_loop(0, rows, body, 0, unroll=ROW_ISSUE_UNROLL)

    @pl.when((b == 0) & (used > 0))
    def _():
        issue(0, 0)

    @pl.when((b + 1 < nb) & (b + 1 < used))
    def _():
        issue(b + 1, (b + 1) % 2)

    @pl.when(b < used)
    def _():
        slot = b % 2
        _wait_slot(src_hbm, buf, sem, slot)

        def emit(r0, j, x):
            o_ref[pl.ds(r0, BF16_SUBLANES), j * LANES:(j + 1) * LANES] = x.astype(o_ref.dtype)

        _token_tiles_to_rows(lambda r: buf[slot, r], rows, buf.shape[2], emit)

    @pl.when(b >= used)
    def _():
        o_ref[...] = jnp.zeros(o_ref.shape, o_ref.dtype)


def gather_rows(src, idx, n_used, rows, out_dtype):
    n = idx.shape[0]
    chunks = src.shape[1]
    d = chunks * LANES
    return pl.pallas_call(
        functools.partial(_gather_rows_kernel, rows=rows),
        out_shape=jax.ShapeDtypeStruct((n, d), out_dtype),
        grid_spec=pltpu.PrefetchScalarGridSpec(
            num_scalar_prefetch=2,
            grid=(n // rows,),
            in_specs=[pl.BlockSpec(memory_space=pl.ANY)],
            out_specs=pl.BlockSpec((rows, d), lambda b, idx, nu: (b, 0)),
            scratch_shapes=[pltpu.VMEM((2, rows, chunks, LANES), src.dtype), pltpu.SemaphoreType.DMA((2,))]),
        compiler_params=_cparams(("arbitrary",)),
        name="gather_rows",
    )(idx, n_used, src)


def _combine_kernel(idx_ref, src_hbm, x_ref, route_ref, o_ref, buf, sem, ybuf, *, rows):
    b = pl.program_id(0)
    nb = pl.num_programs(0)

    def issue(blk, slot):
        def body(r, c):
            base = (blk * rows + r) * TOP_K
            _row_copy(src_hbm, buf, sem, slot, idx_ref[base], r).start()
            _row_copy(src_hbm, buf, sem, slot, idx_ref[base + 1], rows + r).start()
            return c
        lax.fori_loop(0, rows, body, 0, unroll=ROW_ISSUE_UNROLL)

    @pl.when(b == 0)
    def _():
        issue(0, 0)

    @pl.when(b + 1 < nb)
    def _():
        issue(b + 1, (b + 1) % 2)

    slot = b % 2
    _wait_slot(src_hbm, buf, sem, slot)

    def emit(r0, j, x):
        ybuf[pl.ds(r0, BF16_SUBLANES), j * LANES:(j + 1) * LANES] = x

    _token_tiles_to_rows(lambda r: buf[slot, r], TOP_K * rows, buf.shape[2], emit)
    route = route_ref[...]
    o_ref[...] = x_ref[...] + (route[:, 2:3] * ybuf[pl.ds(0, rows), :] + route[:, 3:4] * ybuf[pl.ds(rows, rows), :])


def moe_combine(yb, dest, x, route, rows=128):
    m, d = x.shape
    chunks = yb.shape[1]
    return pl.pallas_call(
        functools.partial(_combine_kernel, rows=rows),
        out_shape=jax.ShapeDtypeStruct((m, d), F32),
        grid_spec=pltpu.PrefetchScalarGridSpec(
            num_scalar_prefetch=1,
            grid=(_exact_div(m, rows),),
            in_specs=[pl.BlockSpec(memory_space=pl.ANY),
                      pl.BlockSpec((rows, d), lambda b, idx: (b, 0)),
                      pl.BlockSpec((rows, LANES), lambda b, idx: (b, 0))],
            out_specs=pl.BlockSpec((rows, d), lambda b, idx: (b, 0)),
            scratch_shapes=[pltpu.VMEM((2, TOP_K * rows, chunks, LANES), F32), pltpu.SemaphoreType.DMA((2,)),
                            pltpu.VMEM((TOP_K * rows, d), F32)]),
        compiler_params=_cparams(("arbitrary",)),
        name="moe_combine",
    )(dest, yb, x, route)


def _expert_runs(block_e, n_used):
    nb = block_e.shape[0]
    idx = jnp.arange(nb, dtype=jnp.int32)
    valid = idx < n_used[0]
    first = valid & ((idx == 0) | (block_e != jnp.roll(block_e, 1)))
    upto = idx[None, :] <= idx[:, None]
    slot = (jnp.sum(jnp.where(upto & first[None, :], 1, 0), axis=1) - 1) % 2
    nxt = jnp.min(jnp.where(first[None, :] & ~upto, idx[None, :], nb), axis=1)
    next_e = jnp.where(nxt < nb, block_e[jnp.minimum(nxt, nb - 1)], -1)
    return first.astype(jnp.int32), slot.astype(jnp.int32), next_e.astype(jnp.int32)


def _stream_expert_weights(b, be_ref, first_ref, slot_ref, next_ref, copies):
    slot = slot_ref[b]

    @pl.when(b == 0)
    def _():
        for cp in copies(be_ref[0], 0):
            cp.start()

    for cp in copies(be_ref[b], slot):
        cp.wait()

    @pl.when(next_ref[b] >= 0)
    def _():
        for cp in copies(next_ref[b], 1 - slot):
            cp.start()

    return slot


WEIGHT_DMA_SPLIT = 4
EXPERT_K_CHUNK = 512


def _expert_up_kernel(be_ref, nused_ref, first_ref, slot_ref, next_ref, x_ref, wg_hbm, wu_hbm, o_ref,
                      wbuf, wgb_ref, wub_ref, sem, *, tf):
    f = pl.program_id(0)
    b = pl.program_id(1)

    def copies(e, slot):
        cols = pl.ds(pl.multiple_of(f * tf, tf), tf)
        rows_per = wbuf.shape[2] // WEIGHT_DMA_SPLIT
        return [pltpu.make_async_copy(w.at[e, pl.ds(q * rows_per, rows_per), cols],
                                      wbuf.at[slot, k, pl.ds(q * rows_per, rows_per)], sem.at[slot, k])
                for k, w in enumerate((wg_hbm, wu_hbm)) for q in range(WEIGHT_DMA_SPLIT)]

    def finish(g, u):
        o_ref[...] = (g * _sigmoid(g) * u).astype(BF16)

    live = b < nused_ref[0]

    @pl.when(live & (first_ref[b] == 1))
    def _():
        slot = _stream_expert_weights(b, be_ref, first_ref, slot_ref, next_ref, copies)
        d = x_ref.shape[1]
        g = jnp.zeros(o_ref.shape, F32)
        u = jnp.zeros(o_ref.shape, F32)
        for k0 in range(0, d, EXPERT_K_CHUNK):
            rows = pl.ds(k0, EXPERT_K_CHUNK)
            wgb_ref[rows, :] = wbuf[slot, 0, rows, :].astype(BF16)
            wub_ref[rows, :] = wbuf[slot, 1, rows, :].astype(BF16)
            xk = x_ref[:, k0:k0 + EXPERT_K_CHUNK]
            g = g + jnp.dot(xk, wgb_ref[rows, :], preferred_element_type=F32)
            u = u + jnp.dot(xk, wub_ref[rows, :], preferred_element_type=F32)
        finish(g, u)

    @pl.when(live & (first_ref[b] == 0))
    def _():
        x = x_ref[...]
        finish(jnp.dot(x, wgb_ref[...], preferred_element_type=F32),
               jnp.dot(x, wub_ref[...], preferred_element_type=F32))

    @pl.when(b >= nused_ref[0])
    def _():
        o_ref[...] = jnp.zeros(o_ref.shape, o_ref.dtype)


def expert_up(xg, block_e, n_used, runs, wg, wu, tf=512):
    n, d = xg.shape
    de = wg.shape[2]
    bm = EXPERT_ROWS
    live = lambda b, nu: jnp.minimum(b, jnp.maximum(nu[0] - 1, 0))
    return pl.pallas_call(
        functools.partial(_expert_up_kernel, tf=tf),
        out_shape=jax.ShapeDtypeStruct((n, de), BF16),
        grid_spec=pltpu.PrefetchScalarGridSpec(
            num_scalar_prefetch=5,
            grid=(de // tf, n // bm),
            in_specs=[pl.BlockSpec((bm, d), lambda f, b, be, nu, fi, sl, ne: (live(b, nu), 0)),
                      pl.BlockSpec(memory_space=pl.ANY), pl.BlockSpec(memory_space=pl.ANY)],
            out_specs=pl.BlockSpec((bm, tf), lambda f, b, be, nu, fi, sl, ne: (b, f)),
            scratch_shapes=[pltpu.VMEM((2, 2, d, tf), F32), pltpu.VMEM((d, tf), BF16), pltpu.VMEM((d, tf), BF16),
                            pltpu.SemaphoreType.DMA((2, 2))]),
        compiler_params=_cparams(("arbitrary", "arbitrary")),
        name="expert_up",
    )(block_e, n_used, *runs, xg, wg, wu)


def _expert_down_kernel(be_ref, nused_ref, first_ref, slot_ref, next_ref, h_ref, wd_hbm, o_ref, wbuf, wdb_ref, sem,
                        ybuf, *, tn):
    c = pl.program_id(0)
    b = pl.program_id(1)

    def copies(e, slot):
        cols = pl.ds(pl.multiple_of(c * tn, tn), tn)
        rows_per = wbuf.shape[1] // WEIGHT_DMA_SPLIT
        return [pltpu.make_async_copy(wd_hbm.at[e, pl.ds(q * rows_per, rows_per), cols],
                                      wbuf.at[slot, pl.ds(q * rows_per, rows_per)], sem.at[slot])
                for q in range(WEIGHT_DMA_SPLIT)]

    live = b < nused_ref[0]

    @pl.when(live & (first_ref[b] == 1))
    def _():
        slot = _stream_expert_weights(b, be_ref, first_ref, slot_ref, next_ref, copies)
        de = h_ref.shape[1]
        y = jnp.zeros(ybuf.shape, F32)
        for k0 in range(0, de, EXPERT_K_CHUNK):
            rows = pl.ds(k0, EXPERT_K_CHUNK)
            wdb_ref[rows, :] = wbuf[slot, rows, :].astype(BF16)
            y = y + jnp.dot(h_ref[:, k0:k0 + EXPERT_K_CHUNK], wdb_ref[rows, :], preferred_element_type=F32)
        ybuf[...] = y

    @pl.when(live & (first_ref[b] == 0))
    def _():
        ybuf[...] = jnp.dot(h_ref[...], wdb_ref[...], preferred_element_type=F32)

    @pl.when(live)
    def _():
        _rows_to_token_tiles(ybuf, o_ref)

    @pl.when(b >= nused_ref[0])
    def _():
        o_ref[...] = jnp.zeros(o_ref.shape, o_ref.dtype)


def expert_down(hmid, block_e, n_used, runs, wd, tn=1024):
    n, de = hmid.shape
    d = wd.shape[2]
    bm = EXPERT_ROWS
    live = lambda b, nu: jnp.minimum(b, jnp.maximum(nu[0] - 1, 0))
    return pl.pallas_call(
        functools.partial(_expert_down_kernel, tn=tn),
        out_shape=jax.ShapeDtypeStruct((n, d // LANES, LANES), F32),
        grid_spec=pltpu.PrefetchScalarGridSpec(
            num_scalar_prefetch=5,
            grid=(d // tn, n // bm),
            in_specs=[pl.BlockSpec((bm, de), lambda c, b, be, nu, fi, sl, ne: (live(b, nu), 0)),
                      pl.BlockSpec(memory_space=pl.ANY)],
            out_specs=pl.BlockSpec((bm, tn // LANES, LANES), lambda c, b, be, nu, fi, sl, ne: (b, c, 0)),
            scratch_shapes=[pltpu.VMEM((2, de, tn), F32), pltpu.VMEM((de, tn), BF16), pltpu.SemaphoreType.DMA((2,)),
                            pltpu.VMEM((bm, tn), F32)]),
        compiler_params=_cparams(("arbitrary", "arbitrary")),
        name="expert_down",
    )(block_e, n_used, *runs, hmid, wd)


def _perm_cols(x):
    pre = x.shape[:-1]
    return jnp.moveaxis(x.reshape(pre + (N_HEADS_B, K_HI, K_LO)), -3, -1).reshape(pre + (RWKV_WIDTH,))


def _unperm_cols(x):
    pre = x.shape[:-1]
    return jnp.moveaxis(x.reshape(pre + (K_HI, K_LO, N_HEADS_B)), -1, -3).reshape(pre + (RWKV_WIDTH,))


def _perm_rows(x):
    return _perm_cols(x.T).T


def _pack_lora_cols(x):
    pad = jnp.zeros(x.shape[:-1] + (LANES - RANK_W,), x.dtype)
    return jnp.concatenate([x[..., :RANK_W], pad, x[..., RANK_W:RANK_W + RANK_A], pad, x[..., RANK_W + RANK_A:]], -1)


def _unpack_lora_cols(x):
    return jnp.concatenate([x[..., :RANK_W], x[..., LANES:LANES + RANK_A], x[..., 2 * LANES:]], -1)


def _state_to_tiles(s):
    n = s.shape[0]
    s = s.reshape(n, N_HEADS_B, V_BLK, SUBLANES, K_HI, K_LO)
    return s.transpose(0, 2, 4, 3, 5, 1).reshape(n, V_BLK, K_HI, SUBLANES, LANES)


def _tiles_to_state(s):
    n = s.shape[0]
    s = s.reshape(n, V_BLK, K_HI, SUBLANES, K_LO, N_HEADS_B)
    return s.transpose(0, 5, 1, 3, 2, 4).reshape(n, N_HEADS_B, HEAD_B, HEAD_B)


def _pad_rows(w, rows):
    return jnp.concatenate([w, jnp.zeros((rows - w.shape[0],) + w.shape[1:], w.dtype)], 0)


def _trunk_layer(xp, xs, pp, ps, cache_k, cache_v, state_wkv, state_shift, t5_table, lam_init, lp, attn_blk=512):
    t, d = xp.shape
    nb, ts, _ = xs.shape
    past = cache_k.shape[1]
    ms = nb * ts
    m = t + ms
    aw = ATT_WIDTH
    c = RWKV_WIDTH
    assert ts == SCAN_TB and t % SCAN_TB == 0 and past % CHUNK == 0 and ts <= CHUNK

    x_parts = [xp, xs.reshape(ms, d)]
    pe = jnp.concatenate([pp, ps.reshape(ms, -1)], 0).astype(BF16)

    w_in = lp["w_in"]
    rkv0 = 3 * aw
    src = jnp.arange(c, dtype=jnp.int32)
    dst = (src % HEAD_B // K_LO) * LANES + (src % K_LO) * N_HEADS_B + src // HEAD_B
    perm_mat = (dst[:, None] == src[None, :]).astype(BF16)
    assert aw == c
    w_rkv = permute_weight_cols(w_in, perm_mat, rkv0 // c, 3)
    w_lora = _pack_lora_cols(w_in[:, rkv0 + 3 * c:])
    h1 = rmsnorm_cast(x_parts, lp["norm1_g"])
    proj_qkv = matmul([(h1, w_in, 0, 0)], 3 * aw)
    feat = matmul([(h1, w_rkv, 0, 0)], 3 * c)
    lora = matmul([(h1, w_lora, 0, 0)], LORA_COLS)

    qn, k_new, kn, vn = qk_norm(proj_qkv, lp["q_norm_g"], lp["k_norm_g"])
    v_new = proj_qkv[:, 2 * aw:]
    lams = [lp[n].reshape(1, HEAD_DIM_A) for n in ("lambda_q1", "lambda_k1", "lambda_q2", "lambda_k2")]
    blk = min(attn_blk, t)
    assert t % blk == 0 and blk >= T5_FAR
    bias_d = bias_tiles(t5_table, blk, blk, rel0=0, masked=True, key_major=True)
    bias_l = bias_tiles(t5_table, blk, blk, rel0=-blk, key_major=True)
    far_bias = t5_table[T5_BUCKETS // 2 - 1] * LOG2E
    nkb = t // blk
    ones_tile = jnp.zeros((nkb, N_HEADS_A, BF16_SUBLANES, blk), BF16).at[:, :, 0, :].set(1.0)
    vt = jnp.transpose(vn[:t].reshape(nkb, blk, N_HEADS_A, 2 * HEAD_DIM_A), (0, 2, 3, 1))
    vt = jnp.concatenate([vt, ones_tile], 2).reshape(nkb, N_HEADS_A * VT_ROWS, blk)
    ya = prompt_attention(qn, kn, vt, jnp.zeros((m, aw), BF16), t, bias_d, bias_l, far_bias, lams, lp["subln_g"], lam_init, blk)
    bias_past = bias_tiles(t5_table, ts, past, rel0=-past)
    bias_new = bias_tiles(t5_table, ts, LANES, rel0=0, n_valid=ts)
    ya = sample_attention(qn, kn, vn, ya, t, cache_k.reshape(nb, past, 2 * N_HEADS_A, HEAD_DIM_A), cache_v,
                          bias_past, bias_new, lams, lp["subln_g"], lam_init)

    shift_rkv = jnp.concatenate([_perm_cols(state_shift[:, 0, i * c:(i + 1) * c]) for i in range(3)], 1)
    shift_lora = _pack_lora_cols(state_shift[:, 0, 3 * c:])
    seq_rows = t + ts * jnp.arange(nb)
    prep_tm = 4 * SCAN_TB

    def run_starts(x, state_rows):
        own = x[SCAN_TB - 1:t - 1:SCAN_TB]
        rows = jnp.concatenate([jnp.zeros((1, x.shape[1]), F32), own, state_rows], 0)
        rows = rows.reshape(m // prep_tm, prep_tm // SCAN_TB, x.shape[1])
        return jnp.concatenate([rows, jnp.zeros((m // prep_tm, SUBLANES - prep_tm // SCAN_TB, x.shape[1]), F32)], 1)

    vec = lambda v: v.reshape(1, -1)
    mu = lp["rwkv_mu"]
    prm = dict(
        mu_rkv=vec(jnp.concatenate([_perm_cols(mu[i * c:(i + 1) * c]) for i in range(3)])),
        mu_lora=vec(_pack_lora_cols(mu[3 * c:])),
        w0=vec(_perm_cols(lp["rwkv_w0"])), a0=vec(_perm_cols(lp["rwkv_a0"])),
        k_k=vec(_perm_cols(lp["rwkv_k_k"])), k_a=vec(_perm_cols(lp["rwkv_k_a"])),
        r_k=vec(_perm_cols(lp["rwkv_r_k"].reshape(-1))),
        w2=_pad_rows(_perm_cols(lp["rwkv_w2"]), LANES), a2=_pad_rows(_perm_cols(lp["rwkv_a2"]), LANES),
        g2=_perm_cols(lp["rwkv_g2"]))
    ak, wr, wdec, bvec, km, vv, gate, br, kr, bonus = rwkv_prep(
        feat, run_starts(feat, shift_rkv), lora, run_starts(lora, shift_lora), prm, tm=prep_tm)
    n_pstep = t // SCAN_TB
    seq_of_step = jnp.concatenate([jnp.zeros((n_pstep,), jnp.int32), 1 + jnp.arange(nb, dtype=jnp.int32)])
    first = jnp.concatenate([jnp.zeros((n_pstep,), jnp.int32).at[0].set(1), jnp.ones((nb,), jnp.int32)])
    last = jnp.concatenate([jnp.zeros((n_pstep,), jnp.int32).at[-1].set(1), jnp.ones((nb,), jnp.int32)])
    s0 = jnp.concatenate([jnp.zeros((1, V_BLK, K_HI, SUBLANES, LANES), F32),
                          _state_to_tiles(state_wkv.astype(F32))], 0)
    y_scan, s_fin = rwkv_scan((ak, wr, wdec, bvec, km, vv), br, s0, seq_of_step, first, last)
    yb = rwkv_post(y_scan, vv, gate, kr, bonus, vec(_perm_cols(lp["lnx_g"])), vec(_perm_cols(lp["lnx_b"])))
    wkv_fin = _tiles_to_state(s_fin)

    def shift_out(rows):
        return jnp.concatenate([_unperm_cols(feat[rows, i * c:(i + 1) * c]) for i in range(3)]
                               + [_unpack_lora_cols(lora[rows])], -1)

    shift_p = shift_out(slice(t - 1, t))
    shift_s = shift_out(slice(t + ts - 1, m, ts))

    w_out_b = matmul([(perm_mat.T, lp["w_out"], 1, 0)], d, out_dtype=BF16)
    x1 = matmul([(ya, lp["w_out"], 0, 0), (yb, w_out_b, 0, 0)], d, mode="residual", res=x_parts)

    rw = jnp.concatenate([lp["rg_w"], lp["ri_w"], jnp.zeros((d, LANES - N_GROUPS - N_EXPERTS), F32)], 1)
    rb = jnp.concatenate([lp["rg_b"], lp["ri_b"].reshape(-1), jnp.zeros((LANES - N_GROUPS - N_EXPERTS,), F32)])
    h2, route = norm_router(x1, lp["norm2_g"], rw, rb.reshape(1, LANES))
    n_assign = m * TOP_K
    flat_e = route[:, :TOP_K].astype(jnp.int32).reshape(n_assign)
    bm = EXPERT_ROWS
    seg = LANES
    onehot = (flat_e[:, None] == jnp.arange(N_EXPERTS, dtype=jnp.int32)[None, :])
    oh = onehot.astype(BF16).reshape(_exact_div(n_assign, seg), seg, N_EXPERTS)
    tri = (jnp.arange(seg)[:, None] >= jnp.arange(seg)[None, :]).astype(BF16)
    within = jnp.einsum("ij,bje->bie", tri, oh, preferred_element_type=F32)
    seg_tot = within[:, -1, :]
    seg_off = jnp.cumsum(seg_tot, axis=0) - seg_tot
    running = (within + seg_off[:, None, :]).reshape(n_assign, N_EXPERTS)
    counts = (seg_off[-1] + seg_tot[-1]).astype(jnp.int32)
    rank = jnp.sum(jnp.where(onehot, running, 0.0), axis=1).astype(jnp.int32) - 1
    pcounts = (counts + bm - 1) // bm * bm
    eid = jnp.arange(N_EXPERTS, dtype=jnp.int32)
    pend = jnp.sum(jnp.where(eid[None, :] <= eid[:, None], pcounts[None, :], 0), axis=1)
    dest = (pend - pcounts)[flat_e] + rank
    n_blocks = n_assign // bm + N_EXPERTS
    rows_tok = jnp.zeros((n_blocks * bm,), jnp.int32).at[dest].set(jnp.arange(n_assign, dtype=jnp.int32) // TOP_K)
    block_row0 = jnp.arange(n_blocks, dtype=jnp.int32) * bm
    block_e = jnp.minimum(jnp.sum((pend[None, :] <= block_row0[:, None]).astype(jnp.int32), axis=1), N_EXPERTS - 1)
    n_used = (pend[-1] // bm).astype(jnp.int32).reshape(1)
    xg = gather_rows(h2, rows_tok, n_used, bm, BF16)
    runs = _expert_runs(block_e, n_used)
    hmid = expert_up(xg, block_e, n_used, runs, lp["e_wg"], lp["e_wu"])
    yexp = expert_down(hmid, block_e, n_used, runs, lp["e_wd"])
    x2 = moe_combine(yexp, dest.astype(jnp.int32), x1, route)

    h3 = rmsnorm_cast([x2], lp["ple_norm_g"])
    yp, ys = matmul([(h3, lp["ple_gate_w"], 0, 0)], d, mode="ple", res=x2, p=pe, pw=lp["ple_proj_w"], out_rows=(t, ms))

    return (yp, ys.reshape(nb, ts, d), k_new, v_new, wkv_fin, shift_p, shift_s)


def kernel(x_prompt, x_sample, p_prompt, p_sample, cache_k, cache_v, state_wkv, state_shift, t5_table, norm1_g, w_in, q_norm_g, k_norm_g, lambda_q1, lambda_k1, lambda_q2, lambda_k2, subln_g, rwkv_mu, rwkv_w0, rwkv_w2, rwkv_a0, rwkv_a2, rwkv_g2, rwkv_k_k, rwkv_k_a, rwkv_r_k, lnx_g, lnx_b, w_out, norm2_g, router_group_w, router_group_b, router_inner_w, router_inner_b, expert_w_gate, expert_w_up, expert_w_down, ple_norm_g, ple_gate_w, ple_proj_w):
    depth = w_in.shape[0]
    bp, t, d = x_prompt.shape
    nb, ts, _ = x_sample.shape
    assert depth == 1 and bp == 1, "one layer and one prompt stream are fused with the sample batch"
    i = 0
    lp = dict(norm1_g=norm1_g[i], w_in=w_in[i], q_norm_g=q_norm_g[i], k_norm_g=k_norm_g[i],
              lambda_q1=lambda_q1[i], lambda_k1=lambda_k1[i], lambda_q2=lambda_q2[i], lambda_k2=lambda_k2[i],
              subln_g=subln_g[i], rwkv_mu=rwkv_mu[i], rwkv_w0=rwkv_w0[i], rwkv_w2=rwkv_w2[i],
              rwkv_a0=rwkv_a0[i], rwkv_a2=rwkv_a2[i], rwkv_g2=rwkv_g2[i], rwkv_k_k=rwkv_k_k[i],
              rwkv_k_a=rwkv_k_a[i], rwkv_r_k=rwkv_r_k[i], lnx_g=lnx_g[i], lnx_b=lnx_b[i], w_out=w_out[i],
              norm2_g=norm2_g[i], rg_w=router_group_w[i], rg_b=router_group_b[i], ri_w=router_inner_w[i],
              ri_b=router_inner_b[i], e_wg=expert_w_gate[i], e_wu=expert_w_up[i], e_wd=expert_w_down[i],
              ple_norm_g=ple_norm_g[i], ple_gate_w=ple_gate_w[i], ple_proj_w=ple_proj_w[i])
    lam_init = 0.8 - 0.6 * math.exp(-0.3 * i)
    yp, ys, k_new, v_new, wkv_fin, shift_p, shift_s = _trunk_layer(
        x_prompt[0], x_sample, p_prompt[i, 0], p_sample[i], cache_k[i], cache_v[i], state_wkv[i],
        state_shift[i], t5_table, lam_init, lp)
    hk = (N_HEADS_A, 2, HEAD_DIM_A)
    hv = (N_HEADS_A, 2 * HEAD_DIM_A)
    return (yp[None], ys,
            k_new[:t].reshape((1, 1, t) + hk), v_new[:t].reshape((1, 1, t) + hv),
            wkv_fin[:1][None], shift_p.reshape(1, 1, 1, -1),
            k_new[t:].reshape((1, nb, ts) + hk), v_new[t:].reshape((1, nb, ts) + hv),
            wkv_fin[1:][None], shift_s.reshape(1, nb, 1, -1))
```

```python
import functools
import math

import jax
import jax.numpy as jnp
from jax import lax
from jax.experimental import pallas as pl
from jax.experimental.pallas import tpu as pltpu

F32 = jnp.float32
BF16 = jnp.bfloat16

LANES = 128
SUBLANES = 8
VMEM_BYTES_V7X = 64 * 1024 * 1024
VMEM_LIMIT = VMEM_BYTES_V7X - 6 * 1024 * 1024

CHUNK = 64
HEAD_DIM_A = 128
N_HEADS_A = 8
ATT_WIDTH = 2 * HEAD_DIM_A * N_HEADS_A
T5_BUCKETS = 32
HEAD_B = 64
N_HEADS_B = 32
RWKV_WIDTH = HEAD_B * N_HEADS_B
RANK_W = 96
RANK_A = 96
RANK_G = 256
LORA_COLS = 512
LNX_EPS = 64e-5
N_GROUPS = 8
EXP_PER_GROUP = 8
N_EXPERTS = N_GROUPS * EXP_PER_GROUP
TOP_K = 2
RMS_EPS = 1e-6
NEG_INF = -1e30
T5_LOG_THRESHOLDS = (12, 16, 23, 32, 46, 64, 91)
T5_FAR = 128

K_LO = LANES // N_HEADS_B
K_HI = HEAD_B // K_LO
V_BLK = HEAD_B // SUBLANES
SCAN_TB = 32

EXPERT_ROWS = 256


def _cparams(sem, vmem=VMEM_LIMIT):
    return pltpu.CompilerParams(dimension_semantics=sem, vmem_limit_bytes=vmem)


def _exact_div(a, b):
    assert a % b == 0, (a, b)
    return a // b


def _sigmoid(x):
    return 1.0 / (1.0 + jnp.exp(-x))


def _part_blocks(parts, tm):
    edges = [0]
    for p in parts:
        edges.append(edges[-1] + _exact_div(p.shape[0], tm))
    return list(zip(edges[:-1], edges[1:]))


def _part_spec(block, lo, hi, row_axis_arg, const_index):
    def index_map(*grid):
        return (jnp.clip(grid[row_axis_arg] - lo, 0, hi - lo - 1),) + const_index(*grid)
    return pl.BlockSpec(block, index_map)


def _rmsnorm_kernel(*refs, ranges):
    x_refs, g_ref, o_ref = refs[:len(ranges)], refs[len(ranges)], refs[len(ranges) + 1]
    i = pl.program_id(0)
    for x_ref, (lo, hi) in zip(x_refs, ranges):
        @pl.when((i >= lo) & (i < hi))
        def _(x_ref=x_ref):
            x = x_ref[...]
            ms = jnp.mean(x * x, axis=-1, keepdims=True)
            o_ref[...] = (x * lax.rsqrt(ms + RMS_EPS) * g_ref[...]).astype(o_ref.dtype)


def rmsnorm_cast(parts, g, tm=256):
    parts = list(parts)
    d = parts[0].shape[1]
    for part in parts:
        tm = math.gcd(tm, part.shape[0])
    ranges = _part_blocks(parts, tm)
    return pl.pallas_call(
        functools.partial(_rmsnorm_kernel, ranges=ranges),
        out_shape=jax.ShapeDtypeStruct((ranges[-1][1] * tm, d), BF16),
        grid=(ranges[-1][1],),
        in_specs=[_part_spec((tm, d), lo, hi, 0, lambda i: (0,)) for lo, hi in ranges]
        + [pl.BlockSpec((1, d), lambda i: (0, 0))],
        out_specs=pl.BlockSpec((tm, d), lambda i: (i, 0)),
        compiler_params=_cparams(("arbitrary",)),
        name="rmsnorm_cast",
    )(*parts, g.reshape(1, d))


def _mm_kernel(*refs, n_pair, cast, transposed, ragged, mode, res_ranges, out_ranges):
    refs = list(refs)
    a_refs, w_refs = refs[:n_pair], refs[n_pair:2 * n_pair]
    del refs[:2 * n_pair]
    res_refs = [refs.pop(0) for _ in res_ranges]
    p_ref, pw_ref = (refs.pop(0), refs.pop(0)) if mode == "ple" else (None, None)
    o_refs = [refs.pop(0) for _ in out_ranges]
    wb_refs = [refs.pop(0) if cast[i] else w_refs[i] for i in range(n_pair)]
    i = pl.program_id(1)

    def product(k):
        dims = _NT if transposed[k] else (((1,), (0,)), ((), ()))
        return lax.dot_general(a_refs[k][...], wb_refs[k][...], dims, preferred_element_type=F32)

    @pl.when(i == 0)
    def _():
        for k in range(n_pair):
            if cast[k]:
                w = w_refs[k][...]
                if ragged[k] is not None:
                    col0, n_valid = ragged[k]
                    row = lax.broadcasted_iota(jnp.int32, w.shape, 0) + (pl.program_id(0) + col0) * w.shape[0]
                    w = jnp.where(row < n_valid, w, 0.0)
                wb_refs[k][...] = w.astype(BF16)

    acc = product(0)
    for k in range(1, n_pair):
        acc = acc + product(k)
    if mode == "ple":
        acc = jnp.dot(p_ref[...], pw_ref[...].astype(BF16), preferred_element_type=F32) * _sigmoid(acc)

    def finish(res_ref):
        val = acc if res_ref is None else res_ref[...] + acc
        for o_ref, (lo, hi) in zip(o_refs, out_ranges):
            if len(o_refs) == 1:
                o_ref[...] = val.astype(o_ref.dtype)
            else:
                @pl.when((i >= lo) & (i < hi))
                def _(o_ref=o_ref):
                    o_ref[...] = val.astype(o_ref.dtype)

    if not res_refs:
        finish(None)
    elif len(res_refs) == 1:
        finish(res_refs[0])
    else:
        for res_ref, (lo, hi) in zip(res_refs, res_ranges):
            @pl.when((i >= lo) & (i < hi))
            def _(res_ref=res_ref):
                finish(res_ref)


def matmul(pairs, n_cols, *, mode="plain", res=None, p=None, pw=None, out_dtype=F32, out_rows=None, tm=1024, tn=512):
    pairs = [tuple(pr) + (False,) * (5 - len(pr)) for pr in pairs]
    m = pairs[0][0].shape[0]
    res_parts = [] if res is None else (list(res) if isinstance(res, (list, tuple)) else [res])
    out_rows = [m] if out_rows is None else list(out_rows)
    for r in [m] + [part.shape[0] for part in res_parts] + out_rows:
        tm = math.gcd(tm, r)
    assert n_cols % tn == 0
    in_specs = [pl.BlockSpec((tm, a.shape[1]), lambda n, i: (i, 0)) for a, _, _, _, _ in pairs]
    w_blocks = [(tn, a.shape[1]) if tr else (a.shape[1], tn) for a, _, _, _, tr in pairs]
    in_specs += [pl.BlockSpec(blk, (lambda n, i, rb=rb, cb=cb: (n + cb, rb)) if tr else
                              (lambda n, i, rb=rb, cb=cb: (rb, n + cb)))
                 for blk, (_, _, rb, cb, tr) in zip(w_blocks, pairs)]
    args = [pr[0] for pr in pairs] + [pr[1] for pr in pairs]
    cast = tuple(pr[1].dtype != BF16 for pr in pairs)
    transposed = tuple(pr[4] for pr in pairs)
    ragged = tuple((cb, w.shape[0]) if tr and cast_k and (cb * tn + n_cols > w.shape[0]) else None
                   for (_, w, _, cb, tr), cast_k in zip(pairs, cast))
    res_ranges = _part_blocks(res_parts, tm)
    in_specs += [_part_spec((tm, tn), lo, hi, 1, lambda n, i: (n,)) for lo, hi in res_ranges]
    args += res_parts
    if mode == "ple":
        kp = p.shape[1]
        in_specs += [pl.BlockSpec((tm, kp), lambda n, i: (i, 0)), pl.BlockSpec((kp, tn), lambda n, i: (0, n))]
        args += [p, pw]
    out_ranges = _part_blocks([jax.ShapeDtypeStruct((r, n_cols), out_dtype) for r in out_rows], tm)
    assert out_ranges[-1][1] * tm == m
    outs = pl.pallas_call(
        functools.partial(_mm_kernel, n_pair=len(pairs), cast=cast, transposed=transposed, ragged=ragged, mode=mode,
                          res_ranges=res_ranges, out_ranges=out_ranges),
        out_shape=tuple(jax.ShapeDtypeStruct((r, n_cols), out_dtype) for r in out_rows),
        grid=(n_cols // tn, m // tm),
        in_specs=in_specs,
        out_specs=tuple(_part_spec((tm, tn), lo, hi, 1, lambda n, i: (n,)) for lo, hi in out_ranges),
        scratch_shapes=[pltpu.VMEM(blk, BF16) for blk, c in zip(w_blocks, cast) if c],
        compiler_params=_cparams(("arbitrary", "arbitrary")),
        name="matmul_" + mode,
    )(*args)
    return outs[0] if len(outs) == 1 else outs


def _perm_rows_kernel(pt_ref, w_ref, o_ref):
    o_ref[...] = jnp.dot(pt_ref[...], w_ref[...].astype(BF16), preferred_element_type=F32).astype(BF16)


def permute_weight_rows(w, perm_t, row_block0, n_sections, tn=512):
    c = RWKV_WIDTH
    n = w.shape[1]
    return pl.pallas_call(
        _perm_rows_kernel,
        out_shape=jax.ShapeDtypeStruct((n_sections * c, n), BF16),
        grid=(n_sections, _exact_div(n, tn)),
        in_specs=[pl.BlockSpec((c, c), lambda sec, j: (0, 0)),
                  pl.BlockSpec((c, tn), lambda sec, j: (row_block0 + sec, j))],
        out_specs=pl.BlockSpec((c, tn), lambda sec, j: (sec, j)),
        compiler_params=_cparams(("parallel", "parallel")),
        name="permute_weight_rows",
    )(perm_t, w)


def _qk_norm_kernel(*refs, ranges):
    q_ref, k_ref, v_ref, qg_ref, kg_ref, qo_ref, kbo_ref, vo_ref = refs[:8]
    n = len(ranges)
    k_outs, v_outs, kbuf = refs[8:8 + n], refs[8 + n:8 + 2 * n], refs[8 + 2 * n]

    def head_norm(x, g):
        ms = jnp.mean(x * x, axis=-1, keepdims=True)
        return x * lax.rsqrt(ms + RMS_EPS) * g

    qg = qg_ref[...]
    kg = kg_ref[...]
    for c in range(ATT_WIDTH // HEAD_DIM_A):
        sl = slice(c * HEAD_DIM_A, (c + 1) * HEAD_DIM_A)
        qn = head_norm(q_ref[:, sl], qg)
        qo_ref[:, sl] = (qn * (LOG2E * HEAD_DIM_A ** -0.5)).astype(BF16)
        kn = head_norm(k_ref[:, sl], kg)
        kbuf[:, sl] = kn
        kbo_ref[:, sl] = kn.astype(BF16)
    vo_ref[...] = v_ref[...].astype(BF16)
    i = pl.program_id(0)
    for k_out, v_out, (lo, hi) in zip(k_outs, v_outs, ranges):
        @pl.when((i >= lo) & (i < hi))
        def _(k_out=k_out, v_out=v_out):
            _rows_to_token_tiles(kbuf, k_out)
            _rows_to_token_tiles(v_ref, v_out, lane_tiles=2)


def qk_norm(proj_qkv, q_g, k_g, part_rows, tm=256):
    m = proj_qkv.shape[0]
    w = ATT_WIDTH
    hw = 2 * HEAD_DIM_A
    for r in part_rows:
        tm = math.gcd(tm, r)
    ranges = _part_blocks([jax.ShapeDtypeStruct((r, w), F32) for r in part_rows], tm)
    blk = lambda c: pl.BlockSpec((tm, w), lambda i, c=c: (i, c))
    vec = pl.BlockSpec((1, HEAD_DIM_A), lambda i: (0, 0))
    out_blk = pl.BlockSpec((tm, w), lambda i: (i, 0))
    rows = jax.ShapeDtypeStruct((m, w), BF16)
    outs = pl.pallas_call(
        functools.partial(_qk_norm_kernel, ranges=ranges),
        out_shape=(rows, rows, rows)
        + tuple(jax.ShapeDtypeStruct((r, w // HEAD_DIM_A, HEAD_DIM_A), F32) for r in part_rows)
        + tuple(jax.ShapeDtypeStruct((r, w // hw, hw), F32) for r in part_rows),
        grid=(_exact_div(m, tm),),
        in_specs=[blk(0), blk(1), blk(2), vec, vec],
        out_specs=(out_blk, out_blk, out_blk)
        + tuple(_part_spec((tm, w // HEAD_DIM_A, HEAD_DIM_A), lo, hi, 0, lambda i: (0, 0)) for lo, hi in ranges)
        + tuple(_part_spec((tm, w // hw, hw), lo, hi, 0, lambda i: (0, 0)) for lo, hi in ranges),
        scratch_shapes=[pltpu.VMEM((tm, w), F32)],
        compiler_params=_cparams(("arbitrary",)),
        name="qk_norm",
    )(proj_qkv, proj_qkv, proj_qkv, q_g.reshape(1, -1), k_g.reshape(1, -1))
    n = len(part_rows)
    return outs[0], outs[1], outs[2], outs[3:3 + n], outs[3 + n:]


def _bias_kernel(tab_ref, o_ref, *, rel0, masked, key_major, n_valid):
    _, nr, nc = o_ref.shape
    r = lax.broadcasted_iota(jnp.int32, (nr, nc), 0)
    c = lax.broadcasted_iota(jnp.int32, (nr, nc), 1)
    kpos, qpos = (r, c) if key_major else (c, r)
    rel = rel0 + kpos - qpos
    n = jnp.abs(rel)
    large = jnp.full((nr, nc), T5_BUCKETS // 4, jnp.int32)
    for thr in T5_LOG_THRESHOLDS:
        large = large + jnp.where(n >= thr, 1, 0)
    bucket = jnp.where(n < T5_BUCKETS // 4, n, large) + jnp.where(rel > 0, T5_BUCKETS // 2, 0)
    if masked:
        visible = (kpos // CHUNK) <= (qpos // CHUNK)
    for h in range(N_HEADS_A):
        acc = jnp.zeros((nr, nc), F32)
        for b in range(T5_BUCKETS):
            acc = jnp.where(bucket == b, tab_ref[b, h] * LOG2E, acc)
        if masked:
            acc = jnp.where(visible, acc, NEG_INF)
        if n_valid is not None:
            acc = jnp.where(c < n_valid, acc, NEG_INF)
        o_ref[h] = acc


def bias_tiles(table, nr, nc, *, rel0, masked=False, key_major=False, n_valid=None):
    return pl.pallas_call(
        functools.partial(_bias_kernel, rel0=rel0, masked=masked, key_major=key_major, n_valid=n_valid),
        out_shape=jax.ShapeDtypeStruct((N_HEADS_A, nr, nc), F32),
        in_specs=[pl.BlockSpec(memory_space=pltpu.SMEM)],
        out_specs=pl.BlockSpec(memory_space=pltpu.VMEM),
        compiler_params=_cparams(None),
        name="t5_bias",
    )(table)


def _lambda_value(lq1, lk1, lq2, lk2, lam_init):
    s1 = jnp.sum(lq1 * lk1, axis=-1, keepdims=True)
    s2 = jnp.sum(lq2 * lk2, axis=-1, keepdims=True)
    return jnp.exp(s1) - jnp.exp(s2) + lam_init


def _online_step(s, bias, v, m_prev, l_prev, acc_prev):
    nchunk = max(s.shape[1] // LANES, 1)
    width = s.shape[1] // nchunk
    sc = [s[:, c * width:(c + 1) * width] + bias(c * width, width) for c in range(nchunk)]
    m_new = jnp.maximum(m_prev, jnp.max(functools.reduce(jnp.maximum, sc), axis=-1, keepdims=True))
    alpha = jnp.exp2(m_prev - m_new)
    p = [jnp.exp2(x - m_new) for x in sc]
    l_new = alpha * l_prev + jnp.sum(functools.reduce(lambda a, b: a + b, p), axis=-1, keepdims=True)
    pb = p[0].astype(BF16) if nchunk == 1 else jnp.concatenate([x.astype(BF16) for x in p], axis=1)
    return m_new, l_new, alpha * acc_prev + jnp.dot(pb, v, preferred_element_type=F32)


def _diff_finish(acc1, l1, acc2, l2, lam, g, lam_init):
    o = acc1 / l1 - lam * (acc2 / l2)
    ms = jnp.mean(o * o, axis=-1, keepdims=True)
    return (o * lax.rsqrt(ms + RMS_EPS) * g) * (1.0 - lam_init)


_NT = (((1,), (1,)), ((), ()))
BF16_SUBLANES = 16
VT_ROWS = 2 * HEAD_DIM_A + BF16_SUBLANES
LOG2E = math.log2(math.e)


def _prompt_attn_kernel(far_ref, q_ref, k_ref, vt_ref, bd_ref, bl_ref, lq1, lk1, lq2, lk2, g_ref, ya_hbm,
                        o_ref, m_ref, acc_ref, *, blk, nsub, lam_init):
    h = pl.program_id(0)
    i = pl.program_id(1)
    m_ref[...] = jnp.full(m_ref.shape, NEG_INF, F32)
    acc_ref[...] = jnp.zeros(acc_ref.shape, F32)
    d = HEAD_DIM_A
    hw = 2 * d
    far_bias = far_ref[h]

    def chain(st, bias_tile, vt, idx):
        shift = far_bias if bias_tile is None else None
        if bias_tile is not None:
            st = st + bias_tile[0]
        col_max = jnp.max(st, axis=0, keepdims=True)
        if shift is not None:
            col_max = col_max + shift
        m_prev = m_ref[idx]
        m_new = jnp.maximum(m_prev, col_max)
        alpha = jnp.exp2(m_prev - m_new)
        p = jnp.exp2(st - (m_new if shift is None else m_new - shift))
        acc_ref[idx] = alpha * acc_ref[idx] + jnp.dot(vt, p.astype(BF16), preferred_element_type=F32)
        m_ref[idx] = m_new

    def update(j, ahead):
        kb = k_ref[pl.ds(pl.multiple_of(j * blk, blk), blk), :]
        vt = vt_ref[j]
        todo = []
        for sa in range(nsub):
            if ahead[sa] < 0:
                continue
            bias_tile = None if ahead[sa] >= 2 else (bl_ref if ahead[sa] == 1 else bd_ref)
            for mp in range(2):
                st = lax.dot_general(kb[:, mp * d:(mp + 1) * d], q_ref[sa * blk:(sa + 1) * blk, mp * d:(mp + 1) * d],
                                     _NT, preferred_element_type=F32)
                todo.append((st, bias_tile, 2 * sa + mp))
        for st, bias_tile, idx in todo:
            chain(st, bias_tile, vt, idx)

    def far_body(j, carry):
        update(j, [2] * nsub)
        return carry

    lax.fori_loop(0, jnp.maximum(nsub * i - 1, 0), far_body, 0)

    @pl.when(i >= 1)
    def _():
        update(nsub * i - 1, [sa + 1 for sa in range(nsub)])

    for o in range(nsub):
        update(nsub * i + o, [sa - o for sa in range(nsub)])
    lam = _lambda_value(lq1[...], lk1[...], lq2[...], lk2[...], lam_init)
    for sa in range(nsub):
        a1 = acc_ref[2 * sa]
        a2 = acc_ref[2 * sa + 1]
        ot = a1[:hw] / a1[hw:hw + 1] - lam * (a2[:hw] / a2[hw:hw + 1])
        ms = jnp.mean(ot * ot, axis=0, keepdims=True)
        yt = ot * lax.rsqrt(ms + RMS_EPS)
        o_ref[sa * blk:(sa + 1) * blk, :] = ((yt.T * g_ref[...]) * (1.0 - lam_init)).astype(BF16)


def prompt_attention(qn, kn, vt, ya, t, bias_d, bias_l, far_bias, lams, subln_g, lam_init, blk, nsub=2):
    nsub = min(nsub, t // blk)
    bq = nsub * blk
    hw = 2 * HEAD_DIM_A
    vec = pl.BlockSpec((1, HEAD_DIM_A), lambda h, i, far: (0, 0))
    return pl.pallas_call(
        functools.partial(_prompt_attn_kernel, blk=blk, nsub=nsub, lam_init=lam_init),
        out_shape=jax.ShapeDtypeStruct(ya.shape, BF16),
        input_output_aliases={11: 0},
        grid_spec=pltpu.PrefetchScalarGridSpec(
            num_scalar_prefetch=1,
            grid=(N_HEADS_A, _exact_div(t, bq)),
            in_specs=[pl.BlockSpec((bq, hw), lambda h, i, far: (i, h)),
                      pl.BlockSpec((t, hw), lambda h, i, far: (0, h)),
                      pl.BlockSpec((t // blk, VT_ROWS, blk), lambda h, i, far: (0, h, 0)),
                      pl.BlockSpec((1, blk, blk), lambda h, i, far: (h, 0, 0)),
                      pl.BlockSpec((1, blk, blk), lambda h, i, far: (h, 0, 0)),
                      vec, vec, vec, vec,
                      pl.BlockSpec((1, hw), lambda h, i, far: (0, 0)),
                      pl.BlockSpec(memory_space=pl.ANY)],
            out_specs=pl.BlockSpec((bq, hw), lambda h, i, far: (i, h)),
            scratch_shapes=[pltpu.VMEM((2 * nsub, 1, blk), F32), pltpu.VMEM((2 * nsub, VT_ROWS, blk), F32)]),
        compiler_params=_cparams(("arbitrary", "arbitrary")),
        name="prompt_attention",
    )(far_bias, qn, kn, vt, bias_d, bias_l, *lams, subln_g.reshape(1, hw), ya)


def _sublane_transpose8(tiles, sub):
    a = list(tiles)
    for dist in (4, 2, 1):
        keep = (sub % (2 * dist)) < dist
        nxt = list(a)
        for i in range(SUBLANES):
            if i % (2 * dist) < dist:
                x, y = a[i], a[i + dist]
                nxt[i] = jnp.where(keep, x, pltpu.roll(y, dist, axis=0))
                nxt[i + dist] = jnp.where(keep, pltpu.roll(x, SUBLANES - dist, axis=0), y)
        a = nxt
    return a


def _sample_attn_kernel(q_ref, ck_ref, cv_ref, kn_ref, vn_ref, bp_ref, bn_ref, lq1, lk1, lq2, lk2, g_ref,
                        ya_hbm, o_ref, m_ref, l_ref, acc_ref, ks_ref, vs_ref, *, lam_init):
    t = pl.program_id(1)
    last = t == pl.num_programs(1) - 1
    d = HEAD_DIM_A
    tk = ck_ref.shape[1]
    ts = q_ref.shape[0]
    n = 2 * N_HEADS_A
    sub = lax.broadcasted_iota(jnp.int32, (SUBLANES, LANES), 0)

    @pl.when(t == 0)
    def _():
        m_ref[...] = jnp.full(m_ref.shape, NEG_INF, F32)
        l_ref[...] = jnp.zeros(l_ref.shape, F32)
        acc_ref[...] = jnp.zeros(acc_ref.shape, F32)
        ks_ref[:, pl.ds(tk, LANES), :] = jnp.zeros((n, LANES, d), BF16)
        vs_ref[:, pl.ds(tk, LANES), :] = jnp.zeros((N_HEADS_A, LANES, 2 * d), BF16)
        for c in range(n):
            ks_ref[c, pl.ds(tk, ts), :] = kn_ref[:, c * d:(c + 1) * d]
        for h in range(N_HEADS_A):
            vs_ref[h, pl.ds(tk, ts), :] = vn_ref[:, 2 * h * d:2 * (h + 1) * d]

    def to_head_major(src, dst, p0, lanes):
        halves = [_sublane_transpose8(src[SUBLANES * a:SUBLANES * (a + 1)], sub) for a in range(2)]
        for r in range(SUBLANES):
            dst(r)[pl.ds(p0, BF16_SUBLANES), lanes] = jnp.concatenate([halves[0][r], halves[1][r]], 0).astype(BF16)

    def relayout(g, carry):
        p0 = pl.multiple_of(g * BF16_SUBLANES, BF16_SUBLANES)
        kt = ck_ref[0, pl.ds(p0, BF16_SUBLANES), :, :]
        vt = cv_ref[0, pl.ds(p0, BF16_SUBLANES), :, :]
        for a in range(2):
            rows = slice(a * SUBLANES, (a + 1) * SUBLANES)
            to_head_major([kt[p, rows, :] for p in range(BF16_SUBLANES)], lambda r, a=a: ks_ref.at[a * SUBLANES + r],
                          p0, slice(None))
            lanes = slice(a * LANES, (a + 1) * LANES)
            to_head_major([vt[p, :, lanes] for p in range(BF16_SUBLANES)], lambda r: vs_ref.at[r], p0, lanes)
        return carry

    lax.fori_loop(0, tk // BF16_SUBLANES, relayout, 0)

    state = [(m_ref[c], l_ref[c], acc_ref[c]) for c in range(n)]
    logits = [lax.dot_general(q_ref[:, c * d:(c + 1) * d], ks_ref[c], _NT, preferred_element_type=F32)
              for c in range(n)]
    for c in range(n):
        h = c // 2

        def bias(k0, w, h=h):
            if k0 < tk:
                return bp_ref[h, :, k0:k0 + w]
            return jnp.where(last, bn_ref[h], NEG_INF)

        state[c] = _online_step(logits[c], bias, vs_ref[h], *state[c])
    for c in range(n):
        m_ref[c], l_ref[c], acc_ref[c] = state[c]

    @pl.when(last)
    def _():
        lam = _lambda_value(lq1[...], lk1[...], lq2[...], lk2[...], lam_init)
        for h in range(N_HEADS_A):
            y = _diff_finish(acc_ref[2 * h], l_ref[2 * h], acc_ref[2 * h + 1], l_ref[2 * h + 1], lam,
                             g_ref[...], lam_init)
            o_ref[:, 2 * h * d:2 * (h + 1) * d] = y.astype(BF16)


def sample_attention(qn, kn, vn, ya, row0, cache_k, cache_v, bias_past, bias_new, lams, subln_g, lam_init, tk=1024):
    nb, past = cache_k.shape[:2]
    w = ATT_WIDTH
    ts = bias_new.shape[1]
    tk = min(tk, past)
    blk0 = row0 // ts
    hw = 2 * HEAD_DIM_A
    vec = pl.BlockSpec((1, HEAD_DIM_A), lambda b, t: (0, 0))
    new_rows = pl.BlockSpec((ts, w), lambda b, t: (blk0 + b, 0))
    return pl.pallas_call(
        functools.partial(_sample_attn_kernel, lam_init=lam_init),
        out_shape=jax.ShapeDtypeStruct(ya.shape, BF16),
        grid=(nb, _exact_div(past, tk)),
        in_specs=[new_rows,
                  pl.BlockSpec((1, tk, 2 * N_HEADS_A, HEAD_DIM_A), lambda b, t: (b, t, 0, 0)),
                  pl.BlockSpec((1, tk, N_HEADS_A, hw), lambda b, t: (b, t, 0, 0)),
                  new_rows, new_rows,
                  pl.BlockSpec((N_HEADS_A, ts, tk), lambda b, t: (0, 0, t)),
                  pl.BlockSpec((N_HEADS_A, ts, LANES), lambda b, t: (0, 0, 0)),
                  vec, vec, vec, vec,
                  pl.BlockSpec((1, hw), lambda b, t: (0, 0)),
                  pl.BlockSpec(memory_space=pl.ANY)],
        out_specs=pl.BlockSpec((ts, w), lambda b, t: (blk0 + b, 0)),
        input_output_aliases={12: 0},
        scratch_shapes=[pltpu.VMEM((2 * N_HEADS_A, ts, 1), F32), pltpu.VMEM((2 * N_HEADS_A, ts, 1), F32),
                        pltpu.VMEM((2 * N_HEADS_A, ts, hw), F32),
                        pltpu.VMEM((2 * N_HEADS_A, tk + LANES, HEAD_DIM_A), BF16),
                        pltpu.VMEM((N_HEADS_A, tk + LANES, hw), BF16)],
        compiler_params=_cparams(("arbitrary", "arbitrary")),
        name="sample_attention",
    )(qn, cache_k, cache_v, kn, vn, bias_past, bias_new, *lams, subln_g.reshape(1, hw), ya)


def _group_allreduce(x):
    x = x + pltpu.roll(x, N_HEADS_B, axis=1)
    return x + pltpu.roll(x, 2 * N_HEADS_B, axis=1)


def _head_sum128(x):
    acc = x[:, 0:LANES]
    for c in range(1, K_HI):
        acc = acc + x[:, c * LANES:(c + 1) * LANES]
    return _group_allreduce(acc)


def _tile16(x128):
    return jnp.concatenate([x128] * K_HI, axis=1)


LORA_USED = RANK_W + RANK_A + RANK_G


def _token_shift(x_ref, pv_ref, st_ref, buf_ref, mu_ref, is_first, is_sample):
    tm = x_ref.shape[0]
    x = x_ref[...]
    buf_ref[pl.ds(0, SUBLANES), :] = jnp.where(is_first, 0.0, pv_ref[...])
    buf_ref[pl.ds(SUBLANES, tm), :] = x
    shifted = buf_ref[pl.ds(SUBLANES - 1, tm), :]
    starts = is_sample & (lax.broadcasted_iota(jnp.int32, (SCAN_TB, 1), 0) == 0)
    prev = jnp.concatenate([jnp.where(starts, st_ref[0, g:g + 1, :], shifted[g * SCAN_TB:(g + 1) * SCAN_TB])
                            for g in range(tm // SCAN_TB)], 0)
    return x + (prev - x) * mu_ref[...]


def _rwkv_prep_kernel(f_ref, pf_ref, sf_ref, lo_ref, plo_ref, slo_ref, mu_ref, mul_ref, w0_ref, a0_ref, kk_ref,
                      ka_ref, rk_ref, w2_ref, a2_ref, g2_ref,
                      ak_o, wr_o, w_o, b_o, km_o, v_o, g_o, br_o, kr_o, bo_o, fbuf, lbuf, *, n_prompt_blocks):
    c = RWKV_WIDTH
    i = pl.program_id(0)
    xm = _token_shift(f_ref, pf_ref, sf_ref, fbuf, mu_ref, i == 0, i >= n_prompt_blocks)
    xl = _token_shift(lo_ref, plo_ref, slo_ref, lbuf, mul_ref, i == 0, i >= n_prompt_blocks)
    r, k, v = xm[:, :c], xm[:, c:2 * c], xm[:, 2 * c:]
    wd, ad, gd = xl[:, :LANES], xl[:, :2 * LANES], xl[:, LANES:]
    lw = w0_ref[...] + jnp.dot(jnp.tanh(wd).astype(BF16), w2_ref[...].astype(BF16), preferred_element_type=F32)
    z = -lw
    softplus = jnp.maximum(z, 0.0) + jnp.log(1.0 + jnp.exp(-jnp.abs(z)))
    decay = jnp.exp(-jnp.exp(-softplus - 0.5))
    a = _sigmoid(a0_ref[...] + jnp.dot(ad.astype(BF16), a2_ref[...].astype(BF16), preferred_element_type=F32))
    g_o[...] = jnp.dot(_sigmoid(gd).astype(BF16), g2_ref[...].astype(BF16), preferred_element_type=F32)
    kk = k * kk_ref[...]
    norm = jnp.maximum(jnp.sqrt(_tile16(_head_sum128(kk * kk))), 1e-12)
    kk = kk / norm
    kmod = k * (1.0 + (a - 1.0) * ka_ref[...])
    bvec = kk * a
    ak_o[...] = -kk
    wr_o[...] = decay * r
    w_o[...] = decay
    b_o[...] = bvec
    km_o[...] = kmod
    v_o[...] = v
    br_o[...] = _head_sum128(bvec * r)
    kr_o[...] = _head_sum128(kmod * r)
    bo_o[...] = _head_sum128(r * kmod * rk_ref[...])


def rwkv_prep(feat, state, lora, state_lora, prm, n_prompt_rows, tm=128):
    m = feat.shape[0]
    c = RWKV_WIDTH
    row = lambda w: pl.BlockSpec((tm, w), lambda i: (i, 0))
    vec = lambda w: pl.BlockSpec((1, w), lambda i: (0, 0))
    mat = lambda r: pl.BlockSpec((r, c), lambda i: (0, 0))
    big = jax.ShapeDtypeStruct((m, c), F32)
    small = jax.ShapeDtypeStruct((m, LANES), F32)
    assert tm // SCAN_TB <= SUBLANES
    npb = _exact_div(n_prompt_rows, tm)
    prev8 = lambda w: pl.BlockSpec((SUBLANES, w), lambda i: (jnp.maximum(i * (tm // SUBLANES) - 1, 0), 0))
    st = lambda w: pl.BlockSpec((1, SUBLANES, w), lambda i: (jnp.maximum(i - npb, 0), 0, 0))
    return pl.pallas_call(
        functools.partial(_rwkv_prep_kernel, n_prompt_blocks=npb),
        out_shape=(big,) * 7 + (small,) * 3,
        grid=(_exact_div(m, tm),),
        in_specs=[row(3 * c), prev8(3 * c), st(3 * c), row(LORA_COLS), prev8(LORA_COLS), st(LORA_COLS),
                  vec(3 * c), vec(LORA_COLS), vec(c), vec(c), vec(c), vec(c), vec(c),
                  mat(LANES), mat(2 * LANES), mat(LORA_COLS - LANES)],
        out_specs=(row(c),) * 7 + (row(LANES),) * 3,
        scratch_shapes=[pltpu.VMEM((tm + SUBLANES, 3 * c), F32), pltpu.VMEM((tm + SUBLANES, LORA_COLS), F32)],
        compiler_params=_cparams(("arbitrary",)),
        name="rwkv_prep",
    )(feat, feat, state, lora, lora, state_lora, prm["mu_rkv"], prm["mu_lora"], prm["w0"], prm["a0"], prm["k_k"], prm["k_a"],
      prm["r_k"], prm["w2"], prm["a2"], prm["g2"])


def _scan_kernel(seq_ref, first_ref, last_ref, ak_ref, wr_ref, w_ref, b_ref, km_ref, v_ref, br_ref, s0_ref,
                 y_ref, sout_ref, s_ref):
    step = pl.program_id(0)

    @pl.when(first_ref[step] == 1)
    def _():
        s_ref[...] = s0_ref[0]

    sub = lax.broadcasted_iota(jnp.int32, (SUBLANES, LANES), 0)
    grp = lax.broadcasted_iota(jnp.int32, (SUBLANES, LANES), 1) // N_HEADS_B
    own_group = grp == (sub % K_LO)
    low_half = sub < K_LO

    def row(ref, t8, s, c):
        tile = ref[pl.ds(t8, SUBLANES), c * LANES:(c + 1) * LANES]
        return jnp.broadcast_to(tile[s:s + 1], (SUBLANES, LANES))

    def time_step(t8, s, y_lo, y_hi):
        vt = []
        for vb in range(V_BLK):
            tile = jnp.where(low_half, row(v_ref, t8, s, 2 * vb), row(v_ref, t8, s, 2 * vb + 1))
            vt.append(_group_allreduce(jnp.where(own_group, tile, 0.0)))
        acc_u = [jnp.zeros((SUBLANES, LANES), F32) for _ in range(V_BLK)]
        acc_y = [jnp.zeros((SUBLANES, LANES), F32) for _ in range(V_BLK)]
        for kh in range(K_HI):
            a_row = row(ak_ref, t8, s, kh)
            wr_row = row(wr_ref, t8, s, kh)
            for vb in range(V_BLK):
                st = s_ref[vb, kh]
                acc_u[vb] = acc_u[vb] + st * a_row
                acc_y[vb] = acc_y[vb] + st * wr_row
        br_row = row(br_ref, t8, s, 0)
        u = [_group_allreduce(x) for x in acc_u]
        for vb in range(V_BLK):
            y = _group_allreduce(acc_y[vb]) + u[vb] * br_row
            ym = jnp.where(own_group, y, 0.0)
            ym = ym + pltpu.roll(ym, 1, axis=0)
            ym = ym + pltpu.roll(ym, 2, axis=0)
            y_lo[vb] = jnp.where(sub == s, pltpu.roll(ym, (s - (K_LO - 1)) % SUBLANES, axis=0), y_lo[vb])
            y_hi[vb] = jnp.where(sub == s, pltpu.roll(ym, (s - (2 * K_LO - 1)) % SUBLANES, axis=0), y_hi[vb])
        for kh in range(K_HI):
            w_row = row(w_ref, t8, s, kh)
            b_row = row(b_ref, t8, s, kh)
            km_row = row(km_ref, t8, s, kh)
            for vb in range(V_BLK):
                s_ref[vb, kh] = s_ref[vb, kh] * w_row + (b_row * u[vb] + km_row * vt[vb])

    def eight_steps(gi, carry):
        t8 = pl.multiple_of(gi * SUBLANES, SUBLANES)
        y_lo = [jnp.zeros((SUBLANES, LANES), F32) for _ in range(V_BLK)]
        y_hi = [jnp.zeros((SUBLANES, LANES), F32) for _ in range(V_BLK)]
        for s in range(SUBLANES):
            time_step(t8, s, y_lo, y_hi)
        for vb in range(V_BLK):
            y_ref[pl.ds(t8, SUBLANES), (2 * vb) * LANES:(2 * vb + 1) * LANES] = y_lo[vb]
            y_ref[pl.ds(t8, SUBLANES), (2 * vb + 1) * LANES:(2 * vb + 2) * LANES] = y_hi[vb]
        return carry

    lax.fori_loop(0, y_ref.shape[0] // SUBLANES, eight_steps, 0)

    @pl.when(last_ref[step] == 1)
    def _():
        sout_ref[0] = s_ref[...]


def rwkv_scan(ops, br, s0, seq_of_step, first, last):
    m, c = ops[0].shape
    nseq = s0.shape[0]
    tb = SCAN_TB
    row = lambda w: pl.BlockSpec((tb, w), lambda i, sq, fi, la: (i, 0))
    st = pl.BlockSpec((1, V_BLK, K_HI, SUBLANES, LANES), lambda i, sq, fi, la: (sq[i], 0, 0, 0, 0))
    return pl.pallas_call(
        _scan_kernel,
        out_shape=(jax.ShapeDtypeStruct((m, c), F32), jax.ShapeDtypeStruct(s0.shape, F32)),
        grid_spec=pltpu.PrefetchScalarGridSpec(
            num_scalar_prefetch=3,
            grid=(m // tb,),
            in_specs=[row(c)] * 6 + [row(LANES), st],
            out_specs=(row(c), st),
            scratch_shapes=[pltpu.VMEM((V_BLK, K_HI, SUBLANES, LANES), F32)]),
        compiler_params=_cparams(("arbitrary",)),
        name="rwkv_scan",
    )(seq_of_step, first, last, *ops, br, s0)


def _rwkv_post_kernel(y_ref, v_ref, g_ref, kr_ref, bo_ref, lg_ref, lb_ref, o_ref):
    v = v_ref[...]
    y = y_ref[...] + v * _tile16(kr_ref[...])
    mean = _tile16(_head_sum128(y)) * (1.0 / HEAD_B)
    yc = y - mean
    var = _tile16(_head_sum128(yc * yc)) * (1.0 / HEAD_B)
    yn = yc * lax.rsqrt(var + LNX_EPS) * lg_ref[...] + lb_ref[...]
    o_ref[...] = ((yn + _tile16(bo_ref[...]) * v) * g_ref[...]).astype(BF16)


def rwkv_post(y, v, g, kr, bo, lnx_g, lnx_b, tm=256):
    m, c = y.shape
    row = lambda w: pl.BlockSpec((tm, w), lambda i: (i, 0))
    vec = pl.BlockSpec((1, c), lambda i: (0, 0))
    return pl.pallas_call(
        _rwkv_post_kernel,
        out_shape=jax.ShapeDtypeStruct((m, c), BF16),
        grid=(_exact_div(m, tm),),
        in_specs=[row(c), row(c), row(c), row(LANES), row(LANES), vec, vec],
        out_specs=row(c),
        compiler_params=_cparams(("parallel",)),
        name="rwkv_post",
    )(y, v, g, kr, bo, lnx_g, lnx_b)


def _rows_to_token_tiles(src_ref, dst_ref, lane_tiles=1):
    rows, cols = src_ref.shape
    sub = lax.broadcasted_iota(jnp.int32, (SUBLANES, LANES), 0)

    def body(g, carry):
        r0 = pl.multiple_of(g * SUBLANES, SUBLANES)
        for a in range(cols // (SUBLANES * LANES * lane_tiles)):
            for lt in range(lane_tiles):
                chunk = lambda i: (SUBLANES * a + i) * lane_tiles + lt
                tiles = [src_ref[pl.ds(r0, SUBLANES), chunk(i) * LANES:(chunk(i) + 1) * LANES]
                         for i in range(SUBLANES)]
                out = _sublane_transpose8(tiles, sub)
                for p in range(SUBLANES):
                    dst_ref[r0 + p, SUBLANES * a:SUBLANES * (a + 1), lt * LANES:(lt + 1) * LANES] = out[p]
        return carry

    lax.fori_loop(0, rows // SUBLANES, body, 0)


def _token_tiles_to_rows(src, n_rows, chunks, emit):
    sub = lax.broadcasted_iota(jnp.int32, (SUBLANES, LANES), 0)

    def body(g, carry):
        r0 = pl.multiple_of(g * BF16_SUBLANES, BF16_SUBLANES)
        for a in range(chunks // SUBLANES):
            halves = [_sublane_transpose8([src(r0 + SUBLANES * b + p)[SUBLANES * a:SUBLANES * (a + 1), :]
                                           for p in range(SUBLANES)], sub) for b in range(2)]
            for i in range(SUBLANES):
                emit(r0, SUBLANES * a + i, jnp.concatenate([halves[0][i], halves[1][i]], 0))
        return carry

    lax.fori_loop(0, n_rows // BF16_SUBLANES, body, 0)


def _router_kernel(x_ref, g_ref, rw_ref, rb_ref, h_ref, route_ref, hbuf):
    x = x_ref[...]
    ms = jnp.mean(x * x, axis=-1, keepdims=True)
    h = x * lax.rsqrt(ms + RMS_EPS) * g_ref[...]
    hbuf[...] = h
    _rows_to_token_tiles(hbuf, h_ref)
    logits = jnp.dot(h, rw_ref[...], precision=lax.Precision.HIGHEST, preferred_element_type=F32) + rb_ref[...]
    lane = lax.broadcasted_iota(jnp.int32, logits.shape, 1)
    big = jnp.int32(LANES)

    def first_argmax(vals, valid):
        masked = jnp.where(valid, vals, -jnp.inf)
        mx = jnp.max(masked, axis=-1, keepdims=True)
        idx = jnp.min(jnp.where(valid & (masked == mx), lane, big), axis=-1, keepdims=True)
        return mx, idx

    is_group = lane < N_GROUPS
    g_max, g_idx = first_argmax(logits, is_group)
    g_top = 1.0 / jnp.sum(jnp.where(is_group, jnp.exp(logits - g_max), 0.0), axis=-1, keepdims=True)
    in_group = (lane >= N_GROUPS) & (lane < N_GROUPS + N_EXPERTS) & ((lane - N_GROUPS) // EXP_PER_GROUP == g_idx)
    i_max, idx1 = first_argmax(logits, in_group)
    z = jnp.sum(jnp.where(in_group, jnp.exp(logits - i_max), 0.0), axis=-1, keepdims=True)
    i_max2, idx2 = first_argmax(logits, in_group & (lane != idx1))
    p1 = 1.0 / z
    p2 = jnp.exp(i_max2 - i_max) / z
    psum = p1 + p2
    gate1 = g_top * p1 / psum
    gate2 = g_top * p2 / psum
    route = jnp.where(lane == 0, (idx1 - N_GROUPS).astype(F32),
                      jnp.where(lane == 1, (idx2 - N_GROUPS).astype(F32),
                                jnp.where(lane == 2, gate1, jnp.where(lane == 3, gate2, 0.0))))
    route_ref[...] = route


def norm_router(x, g, rw, rb, tm=256):
    m, d = x.shape
    return pl.pallas_call(
        _router_kernel,
        out_shape=(jax.ShapeDtypeStruct((m, d // LANES, LANES), F32), jax.ShapeDtypeStruct((m, LANES), F32)),
        grid=(_exact_div(m, tm),),
        in_specs=[pl.BlockSpec((tm, d), lambda i: (i, 0)), pl.BlockSpec((1, d), lambda i: (0, 0)),
                  pl.BlockSpec((d, LANES), lambda i: (0, 0)), pl.BlockSpec((1, LANES), lambda i: (0, 0))],
        out_specs=(pl.BlockSpec((tm, d // LANES, LANES), lambda i: (i, 0, 0)),
                   pl.BlockSpec((tm, LANES), lambda i: (i, 0))),
        scratch_shapes=[pltpu.VMEM((tm, d), F32)],
        compiler_params=_cparams(("arbitrary",)),
        name="norm_router",
    )(x, g.reshape(1, d), rw, rb)


def _row_copy(src_hbm, buf, sem, slot, src_row, dst_row):
    return pltpu.make_async_copy(src_hbm.at[src_row], buf.at[slot, dst_row], sem.at[slot])


def _wait_slot(src_hbm, buf, sem, slot):
    pltpu.make_async_copy(src_hbm.at[pl.ds(0, buf.shape[1])], buf.at[slot], sem.at[slot]).wait()


ROW_ISSUE_UNROLL = 8


def _gather_rows_kernel(idx_ref, nused_ref, src_hbm, o_ref, buf, sem, *, rows):
    b = pl.program_id(0)
    nb = pl.num_programs(0)
    used = nused_ref[0]

    def issue(blk, slot):
        def body(r, c):
            _row_copy(src_hbm, buf, sem, slot, idx_ref[blk * rows + r], r).start()
            return c
        lax.fori_loop(0, rows, body, 0, unroll=ROW_ISSUE_UNROLL)

    @pl.when((b == 0) & (used > 0))
    def _():
        issue(0, 0)

    @pl.when((b + 1 < nb) & (b + 1 < used))
    def _():
        issue(b + 1, (b + 1) % 2)

    @pl.when(b < used)
    def _():
        slot = b % 2
        _wait_slot(src_hbm, buf, sem, slot)

        def emit(r0, j, x):
            o_ref[pl.ds(r0, BF16_SUBLANES), j * LANES:(j + 1) * LANES] = x.astype(o_ref.dtype)

        _token_tiles_to_rows(lambda r: buf[slot, r], rows, buf.shape[2], emit)

    @pl.when(b >= used)
    def _():
        o_ref[...] = jnp.zeros(o_ref.shape, o_ref.dtype)


def gather_rows(src, idx, n_used, rows, out_dtype):
    n = idx.shape[0]
    chunks = src.shape[1]
    d = chunks * LANES
    return pl.pallas_call(
        functools.partial(_gather_rows_kernel, rows=rows),
        out_shape=jax.ShapeDtypeStruct((n, d), out_dtype),
        grid_spec=pltpu.PrefetchScalarGridSpec(
            num_scalar_prefetch=2,
            grid=(n // rows,),
            in_specs=[pl.BlockSpec(memory_space=pl.ANY)],
            out_specs=pl.BlockSpec((rows, d), lambda b, idx, nu: (b, 0)),
            scratch_shapes=[pltpu.VMEM((2, rows, chunks, LANES), src.dtype), pltpu.SemaphoreType.DMA((2,))]),
        compiler_params=_cparams(("arbitrary",)),
        name="gather_rows",
    )(idx, n_used, src)


def _combine_kernel(idx_ref, src_hbm, x_ref, route_ref, o_ref, buf, sem, ybuf, *, rows):
    b = pl.program_id(0)
    nb = pl.num_programs(0)

    def issue(blk, slot):
        def body(r, c):
            base = (blk * rows + r) * TOP_K
            _row_copy(src_hbm, buf, sem, slot, idx_ref[base], r).start()
            _row_copy(src_hbm, buf, sem, slot, idx_ref[base + 1], rows + r).start()
            return c
        lax.fori_loop(0, rows, body, 0, unroll=ROW_ISSUE_UNROLL)

    @pl.when(b == 0)
    def _():
        issue(0, 0)

    @pl.when(b + 1 < nb)
    def _():
        issue(b + 1, (b + 1) % 2)

    slot = b % 2
    _wait_slot(src_hbm, buf, sem, slot)

    def emit(r0, j, x):
        ybuf[pl.ds(r0, BF16_SUBLANES), j * LANES:(j + 1) * LANES] = x

    _token_tiles_to_rows(lambda r: buf[slot, r], TOP_K * rows, buf.shape[2], emit)
    route = route_ref[...]
    o_ref[...] = x_ref[...] + (route[:, 2:3] * ybuf[pl.ds(0, rows), :] + route[:, 3:4] * ybuf[pl.ds(rows, rows), :])


def moe_combine(yb, dest, x, route, rows=128):
    m, d = x.shape
    chunks = yb.shape[1]
    return pl.pallas_call(
        functools.partial(_combine_kernel, rows=rows),
        out_shape=jax.ShapeDtypeStruct((m, d), F32),
        grid_spec=pltpu.PrefetchScalarGridSpec(
            num_scalar_prefetch=1,
            grid=(_exact_div(m, rows),),
            in_specs=[pl.BlockSpec(memory_space=pl.ANY),
                      pl.BlockSpec((rows, d), lambda b, idx: (b, 0)),
                      pl.BlockSpec((rows, LANES), lambda b, idx: (b, 0))],
            out_specs=pl.BlockSpec((rows, d), lambda b, idx: (b, 0)),
            scratch_shapes=[pltpu.VMEM((2, TOP_K * rows, chunks, LANES), F32), pltpu.SemaphoreType.DMA((2,)),
                            pltpu.VMEM((TOP_K * rows, d), F32)]),
        compiler_params=_cparams(("arbitrary",)),
        name="moe_combine",
    )(dest, yb, x, route)


def _expert_runs(block_e, n_used):
    nb = block_e.shape[0]
    idx = jnp.arange(nb, dtype=jnp.int32)
    valid = idx < n_used[0]
    first = valid & ((idx == 0) | (block_e != jnp.roll(block_e, 1)))
    upto = idx[None, :] <= idx[:, None]
    slot = (jnp.sum(jnp.where(upto & first[None, :], 1, 0), axis=1) - 1) % 2
    nxt = jnp.min(jnp.where(first[None, :] & ~upto, idx[None, :], nb), axis=1)
    next_e = jnp.where(nxt < nb, block_e[jnp.minimum(nxt, nb - 1)], -1)
    return first.astype(jnp.int32), slot.astype(jnp.int32), next_e.astype(jnp.int32)


def _stream_expert_weights(b, be_ref, first_ref, slot_ref, next_ref, copies):
    slot = slot_ref[b]

    @pl.when(b == 0)
    def _():
        for cp in copies(be_ref[0], 0):
            cp.start()

    for cp in copies(be_ref[b], slot):
        cp.wait()

    @pl.when(next_ref[b] >= 0)
    def _():
        for cp in copies(next_ref[b], 1 - slot):
            cp.start()

    return slot


WEIGHT_DMA_SPLIT = 4
EXPERT_K_CHUNK = 512


def _expert_up_kernel(be_ref, nused_ref, first_ref, slot_ref, next_ref, x_ref, wg_hbm, wu_hbm, o_ref,
                      wbuf, wgb_ref, wub_ref, sem, *, tf):
    f = pl.program_id(0)
    b = pl.program_id(1)

    def copies(e, slot):
        cols = pl.ds(pl.multiple_of(f * tf, tf), tf)
        rows_per = wbuf.shape[2] // WEIGHT_DMA_SPLIT
        return [pltpu.make_async_copy(w.at[e, pl.ds(q * rows_per, rows_per), cols],
                                      wbuf.at[slot, k, pl.ds(q * rows_per, rows_per)], sem.at[slot, k])
                for k, w in enumerate((wg_hbm, wu_hbm)) for q in range(WEIGHT_DMA_SPLIT)]

    def finish(g, u):
        o_ref[...] = (g * _sigmoid(g) * u).astype(BF16)

    live = b < nused_ref[0]

    @pl.when(live & (first_ref[b] == 1))
    def _():
        slot = _stream_expert_weights(b, be_ref, first_ref, slot_ref, next_ref, copies)
        d = x_ref.shape[1]
        g = jnp.zeros(o_ref.shape, F32)
        u = jnp.zeros(o_ref.shape, F32)
        for k0 in range(0, d, EXPERT_K_CHUNK):
            rows = pl.ds(k0, EXPERT_K_CHUNK)
            wgb_ref[rows, :] = wbuf[slot, 0, rows, :].astype(BF16)
            wub_ref[rows, :] = wbuf[slot, 1, rows, :].astype(BF16)
            xk = x_ref[:, k0:k0 + EXPERT_K_CHUNK]
            g = g + jnp.dot(xk, wgb_ref[rows, :], preferred_element_type=F32)
            u = u + jnp.dot(xk, wub_ref[rows, :], preferred_element_type=F32)
        finish(g, u)

    @pl.when(live & (first_ref[b] == 0))
    def _():
        x = x_ref[...]
        finish(jnp.dot(x, wgb_ref[...], preferred_element_type=F32),
               jnp.dot(x, wub_ref[...], preferred_element_type=F32))

    @pl.when(b >= nused_ref[0])
    def _():
        o_ref[...] = jnp.zeros(o_ref.shape, o_ref.dtype)


def expert_up(xg, block_e, n_used, runs, wg, wu, tf=512):
    n, d = xg.shape
    de = wg.shape[2]
    bm = EXPERT_ROWS
    live = lambda b, nu: jnp.minimum(b, jnp.maximum(nu[0] - 1, 0))
    return pl.pallas_call(
        functools.partial(_expert_up_kernel, tf=tf),
        out_shape=jax.ShapeDtypeStruct((n, de), BF16),
        grid_spec=pltpu.PrefetchScalarGridSpec(
            num_scalar_prefetch=5,
            grid=(de // tf, n // bm),
            in_specs=[pl.BlockSpec((bm, d), lambda f, b, be, nu, fi, sl, ne: (live(b, nu), 0)),
                      pl.BlockSpec(memory_space=pl.ANY), pl.BlockSpec(memory_space=pl.ANY)],
            out_specs=pl.BlockSpec((bm, tf), lambda f, b, be, nu, fi, sl, ne: (b, f)),
            scratch_shapes=[pltpu.VMEM((2, 2, d, tf), F32), pltpu.VMEM((d, tf), BF16), pltpu.VMEM((d, tf), BF16),
                            pltpu.SemaphoreType.DMA((2, 2))]),
        compiler_params=_cparams(("arbitrary", "arbitrary")),
        name="expert_up",
    )(block_e, n_used, *runs, xg, wg, wu)


def _expert_down_kernel(be_ref, nused_ref, first_ref, slot_ref, next_ref, h_ref, wd_hbm, o_ref, wbuf, wdb_ref, sem,
                        ybuf, *, tn):
    c = pl.program_id(0)
    b = pl.program_id(1)

    def copies(e, slot):
        cols = pl.ds(pl.multiple_of(c * tn, tn), tn)
        rows_per = wbuf.shape[1] // WEIGHT_DMA_SPLIT
        return [pltpu.make_async_copy(wd_hbm.at[e, pl.ds(q * rows_per, rows_per), cols],
                                      wbuf.at[slot, pl.ds(q * rows_per, rows_per)], sem.at[slot])
                for q in range(WEIGHT_DMA_SPLIT)]

    live = b < nused_ref[0]

    @pl.when(live & (first_ref[b] == 1))
    def _():
        slot = _stream_expert_weights(b, be_ref, first_ref, slot_ref, next_ref, copies)
        de = h_ref.shape[1]
        y = jnp.zeros(ybuf.shape, F32)
        for k0 in range(0, de, EXPERT_K_CHUNK):
            rows = pl.ds(k0, EXPERT_K_CHUNK)
            wdb_ref[rows, :] = wbuf[slot, rows, :].astype(BF16)
            y = y + jnp.dot(h_ref[:, k0:k0 + EXPERT_K_CHUNK], wdb_ref[rows, :], preferred_element_type=F32)
        ybuf[...] = y

    @pl.when(live & (first_ref[b] == 0))
    def _():
        ybuf[...] = jnp.dot(h_ref[...], wdb_ref[...], preferred_element_type=F32)

    @pl.when(live)
    def _():
        _rows_to_token_tiles(ybuf, o_ref)

    @pl.when(b >= nused_ref[0])
    def _():
        o_ref[...] = jnp.zeros(o_ref.shape, o_ref.dtype)


def expert_down(hmid, block_e, n_used, runs, wd, tn=2048):
    n, de = hmid.shape
    d = wd.shape[2]
    bm = EXPERT_ROWS
    live = lambda b, nu: jnp.minimum(b, jnp.maximum(nu[0] - 1, 0))
    return pl.pallas_call(
        functools.partial(_expert_down_kernel, tn=tn),
        out_shape=jax.ShapeDtypeStruct((n, d // LANES, LANES), F32),
        grid_spec=pltpu.PrefetchScalarGridSpec(
            num_scalar_prefetch=5,
            grid=(d // tn, n // bm),
            in_specs=[pl.BlockSpec((bm, de), lambda c, b, be, nu, fi, sl, ne: (live(b, nu), 0)),
                      pl.BlockSpec(memory_space=pl.ANY)],
            out_specs=pl.BlockSpec((bm, tn // LANES, LANES), lambda c, b, be, nu, fi, sl, ne: (b, c, 0)),
            scratch_shapes=[pltpu.VMEM((2, de, tn), F32), pltpu.VMEM((de, tn), BF16), pltpu.SemaphoreType.DMA((2,)),
                            pltpu.VMEM((bm, tn), F32)]),
        compiler_params=_cparams(("arbitrary", "arbitrary")),
        name="expert_down",
    )(block_e, n_used, *runs, hmid, wd)


def _perm_cols(x):
    pre = x.shape[:-1]
    return jnp.moveaxis(x.reshape(pre + (N_HEADS_B, K_HI, K_LO)), -3, -1).reshape(pre + (RWKV_WIDTH,))


def _unperm_cols(x):
    pre = x.shape[:-1]
    return jnp.moveaxis(x.reshape(pre + (K_HI, K_LO, N_HEADS_B)), -1, -3).reshape(pre + (RWKV_WIDTH,))


def _state_to_tiles(s):
    n = s.shape[0]
    s = s.reshape(n, N_HEADS_B, V_BLK, SUBLANES, K_HI, K_LO)
    return s.transpose(0, 2, 4, 3, 5, 1).reshape(n, V_BLK, K_HI, SUBLANES, LANES)


def _tiles_to_state(s):
    n = s.shape[0]
    s = s.reshape(n, V_BLK, K_HI, SUBLANES, K_LO, N_HEADS_B)
    return s.transpose(0, 5, 1, 3, 2, 4).reshape(n, N_HEADS_B, HEAD_B, HEAD_B)


def _pad_rows(w, row0, rows):
    return jnp.zeros((rows,) + w.shape[1:], w.dtype).at[row0:row0 + w.shape[0]].set(w)


def _trunk_layer(xp, xs, pp, ps, cache_k, cache_v, state_wkv, state_shift, t5_table, lam_init, lp, attn_blk=512):
    t, d = xp.shape
    nb, ts, _ = xs.shape
    past = cache_k.shape[1]
    ms = nb * ts
    m = t + ms
    aw = ATT_WIDTH
    c = RWKV_WIDTH
    assert ts == SCAN_TB and t % SCAN_TB == 0 and past % CHUNK == 0 and ts <= CHUNK

    x_parts = [xp, xs.reshape(ms, d)]
    pe = jnp.concatenate([pp, ps.reshape(ms, -1)], 0).astype(BF16)

    w_t = lp["w_in"].T
    rkv0 = 3 * aw
    src = jnp.arange(c, dtype=jnp.int32)
    dst = (src % HEAD_B // K_LO) * LANES + (src % K_LO) * N_HEADS_B + src // HEAD_B
    perm_t = (src[:, None] == dst[None, :]).astype(BF16)
    assert aw == c
    w_rkv_t = permute_weight_rows(w_t, perm_t, rkv0 // c, 3)
    h1 = rmsnorm_cast(x_parts, lp["norm1_g"])
    tn = 512
    proj_qkv = matmul([(h1, w_t, 0, 0, True)], 3 * aw, tn=tn)
    feat = matmul([(h1, w_rkv_t, 0, 0, True)], 3 * c, tn=tn)
    lora = matmul([(h1, w_t, 0, (rkv0 + 3 * c) // tn, True)], LORA_COLS, tn=tn)

    qn, kn, vn, k_new, v_new = qk_norm(proj_qkv, lp["q_norm_g"], lp["k_norm_g"], (t, ms))
    lams = [lp[n].reshape(1, HEAD_DIM_A) for n in ("lambda_q1", "lambda_k1", "lambda_q2", "lambda_k2")]
    blk = min(attn_blk, t)
    assert t % blk == 0 and blk >= T5_FAR
    bias_d = bias_tiles(t5_table, blk, blk, rel0=0, masked=True, key_major=True)
    bias_l = bias_tiles(t5_table, blk, blk, rel0=-blk, key_major=True)
    far_bias = t5_table[T5_BUCKETS // 2 - 1] * LOG2E
    nkb = t // blk
    ones_tile = jnp.zeros((nkb, N_HEADS_A, BF16_SUBLANES, blk), BF16).at[:, :, 0, :].set(1.0)
    vt = jnp.transpose(vn[:t].reshape(nkb, blk, N_HEADS_A, 2 * HEAD_DIM_A), (0, 2, 3, 1))
    vt = jnp.concatenate([vt, ones_tile], 2).reshape(nkb, N_HEADS_A * VT_ROWS, blk)
    ya = prompt_attention(qn, kn, vt, jnp.zeros((m, aw), BF16), t, bias_d, bias_l, far_bias, lams, lp["subln_g"], lam_init, blk)
    bias_past = bias_tiles(t5_table, ts, past, rel0=-past)
    bias_new = bias_tiles(t5_table, ts, LANES, rel0=0, n_valid=ts)
    ya = sample_attention(qn, kn, vn, ya, t, cache_k.reshape(nb, past, 2 * N_HEADS_A, HEAD_DIM_A), cache_v,
                          bias_past, bias_new, lams, lp["subln_g"], lam_init)

    shift_rkv = jnp.concatenate([_perm_cols(state_shift[:, 0, i * c:(i + 1) * c]) for i in range(3)], 1)
    lora_pad = jnp.zeros((LORA_COLS - LORA_USED,), F32)
    shift_lora = jnp.concatenate([state_shift[:, 0, 3 * c:], jnp.broadcast_to(lora_pad, (nb, lora_pad.shape[0]))], 1)
    prep_tm = 4 * SCAN_TB
    seqs_per_block = prep_tm // ts

    def block_states(rows):
        rows = rows.reshape(ms // prep_tm, seqs_per_block, rows.shape[1])
        return jnp.concatenate([rows, jnp.zeros((ms // prep_tm, SUBLANES - seqs_per_block, rows.shape[2]), F32)], 1)

    vec = lambda v: v.reshape(1, -1)
    mu = lp["rwkv_mu"]
    prm = dict(
        mu_rkv=vec(jnp.concatenate([_perm_cols(mu[i * c:(i + 1) * c]) for i in range(3)])),
        mu_lora=vec(jnp.concatenate([mu[3 * c:], lora_pad])),
        w0=vec(_perm_cols(lp["rwkv_w0"])), a0=vec(_perm_cols(lp["rwkv_a0"])),
        k_k=vec(_perm_cols(lp["rwkv_k_k"])), k_a=vec(_perm_cols(lp["rwkv_k_a"])),
        r_k=vec(_perm_cols(lp["rwkv_r_k"].reshape(-1))),
        w2=_pad_rows(_perm_cols(lp["rwkv_w2"]), 0, LANES), a2=_pad_rows(_perm_cols(lp["rwkv_a2"]), RANK_W, 2 * LANES),
        g2=_pad_rows(_perm_cols(lp["rwkv_g2"]), RANK_W + RANK_A - LANES, LORA_COLS - LANES))
    ak, wr, wdec, bvec, km, vv, gate, br, kr, bonus = rwkv_prep(
        feat, block_states(shift_rkv), lora, block_states(shift_lora), prm, t, tm=prep_tm)
    n_pstep = t // SCAN_TB
    seq_of_step = jnp.concatenate([jnp.zeros((n_pstep,), jnp.int32), 1 + jnp.arange(nb, dtype=jnp.int32)])
    first = jnp.concatenate([jnp.zeros((n_pstep,), jnp.int32).at[0].set(1), jnp.ones((nb,), jnp.int32)])
    last = jnp.concatenate([jnp.zeros((n_pstep,), jnp.int32).at[-1].set(1), jnp.ones((nb,), jnp.int32)])
    s0 = jnp.concatenate([jnp.zeros((1, V_BLK, K_HI, SUBLANES, LANES), F32),
                          _state_to_tiles(state_wkv.astype(F32))], 0)
    y_scan, s_fin = rwkv_scan((ak, wr, wdec, bvec, km, vv), br, s0, seq_of_step, first, last)
    yb = rwkv_post(y_scan, vv, gate, kr, bonus, vec(_perm_cols(lp["lnx_g"])), vec(_perm_cols(lp["lnx_b"])))
    wkv_fin = _tiles_to_state(s_fin)

    def shift_out(rows):
        return jnp.concatenate([_unperm_cols(feat[rows, i * c:(i + 1) * c]) for i in range(3)]
                               + [lora[rows, :LORA_USED]], -1)

    shift_p = shift_out(slice(t - 1, t))
    shift_s = shift_out(slice(t + ts - 1, m, ts))

    w_out_b = matmul([(perm_t, lp["w_out"], 1, 0)], d, out_dtype=BF16)
    x1 = matmul([(ya, lp["w_out"], 0, 0), (yb, w_out_b, 0, 0)], d, mode="residual", res=x_parts)

    rw = jnp.concatenate([lp["rg_w"], lp["ri_w"], jnp.zeros((d, LANES - N_GROUPS - N_EXPERTS), F32)], 1)
    rb = jnp.concatenate([lp["rg_b"], lp["ri_b"].reshape(-1), jnp.zeros((LANES - N_GROUPS - N_EXPERTS,), F32)])
    h2, route = norm_router(x1, lp["norm2_g"], rw, rb.reshape(1, LANES))
    n_assign = m * TOP_K
    flat_e = route[:, :TOP_K].astype(jnp.int32).reshape(n_assign)
    bm = EXPERT_ROWS
    seg = LANES
    onehot = (flat_e[:, None] == jnp.arange(N_EXPERTS, dtype=jnp.int32)[None, :])
    oh = onehot.astype(BF16).reshape(_exact_div(n_assign, seg), seg, N_EXPERTS)
    tri = (jnp.arange(seg)[:, None] >= jnp.arange(seg)[None, :]).astype(BF16)
    within = jnp.einsum("ij,bje->bie", tri, oh, preferred_element_type=F32)
    seg_tot = within[:, -1, :]
    seg_off = jnp.cumsum(seg_tot, axis=0) - seg_tot
    running = (within + seg_off[:, None, :]).reshape(n_assign, N_EXPERTS)
    counts = (seg_off[-1] + seg_tot[-1]).astype(jnp.int32)
    rank = jnp.sum(jnp.where(onehot, running, 0.0), axis=1).astype(jnp.int32) - 1
    pcounts = (counts + bm - 1) // bm * bm
    eid = jnp.arange(N_EXPERTS, dtype=jnp.int32)
    pend = jnp.sum(jnp.where(eid[None, :] <= eid[:, None], pcounts[None, :], 0), axis=1)
    dest = (pend - pcounts)[flat_e] + rank
    n_blocks = n_assign // bm + N_EXPERTS
    rows_tok = jnp.zeros((n_blocks * bm,), jnp.int32).at[dest].set(jnp.arange(n_assign, dtype=jnp.int32) // TOP_K)
    block_row0 = jnp.arange(n_blocks, dtype=jnp.int32) * bm
    block_e = jnp.minimum(jnp.sum((pend[None, :] <= block_row0[:, None]).astype(jnp.int32), axis=1), N_EXPERTS - 1)
    n_used = (pend[-1] // bm).astype(jnp.int32).reshape(1)
    xg = gather_rows(h2, rows_tok, n_used, bm, BF16)
    runs = _expert_runs(block_e, n_used)
    hmid = expert_up(xg, block_e, n_used, runs, lp["e_wg"], lp["e_wu"])
    yexp = expert_down(hmid, block_e, n_used, runs, lp["e_wd"])
    x2 = moe_combine(yexp, dest.astype(jnp.int32), x1, route)

    h3 = rmsnorm_cast([x2], lp["ple_norm_g"])
    yp, ys = matmul([(h3, lp["ple_gate_w"], 0, 0)], d, mode="ple", res=x2, p=pe, pw=lp["ple_proj_w"], out_rows=(t, ms))

    return (yp, ys.reshape(nb, ts, d), k_new, v_new, wkv_fin, shift_p, shift_s)


def kernel(x_prompt, x_sample, p_prompt, p_sample, cache_k, cache_v, state_wkv, state_shift, t5_table, norm1_g, w_in, q_norm_g, k_norm_g, lambda_q1, lambda_k1, lambda_q2, lambda_k2, subln_g, rwkv_mu, rwkv_w0, rwkv_w2, rwkv_a0, rwkv_a2, rwkv_g2, rwkv_k_k, rwkv_k_a, rwkv_r_k, lnx_g, lnx_b, w_out, norm2_g, router_group_w, router_group_b, router_inner_w, router_inner_b, expert_w_gate, expert_w_up, expert_w_down, ple_norm_g, ple_gate_w, ple_proj_w):
    depth = w_in.shape[0]
    bp, t, d = x_prompt.shape
    nb, ts, _ = x_sample.shape
    assert depth == 1 and bp == 1, "one layer and one prompt stream are fused with the sample batch"
    i = 0
    lp = dict(norm1_g=norm1_g[i], w_in=w_in[i], q_norm_g=q_norm_g[i], k_norm_g=k_norm_g[i],
              lambda_q1=lambda_q1[i], lambda_k1=lambda_k1[i], lambda_q2=lambda_q2[i], lambda_k2=lambda_k2[i],
              subln_g=subln_g[i], rwkv_mu=rwkv_mu[i], rwkv_w0=rwkv_w0[i], rwkv_w2=rwkv_w2[i],
              rwkv_a0=rwkv_a0[i], rwkv_a2=rwkv_a2[i], rwkv_g2=rwkv_g2[i], rwkv_k_k=rwkv_k_k[i],
              rwkv_k_a=rwkv_k_a[i], rwkv_r_k=rwkv_r_k[i], lnx_g=lnx_g[i], lnx_b=lnx_b[i], w_out=w_out[i],
              norm2_g=norm2_g[i], rg_w=router_group_w[i], rg_b=router_group_b[i], ri_w=router_inner_w[i],
              ri_b=router_inner_b[i], e_wg=expert_w_gate[i], e_wu=expert_w_up[i], e_wd=expert_w_down[i],
              ple_norm_g=ple_norm_g[i], ple_gate_w=ple_gate_w[i], ple_proj_w=ple_proj_w[i])
    lam_init = 0.8 - 0.6 * math.exp(-0.3 * i)
    yp, ys, k_new, v_new, wkv_fin, shift_p, shift_s = _trunk_layer(
        x_prompt[0], x_sample, p_prompt[i, 0], p_sample[i], cache_k[i], cache_v[i], state_wkv[i],
        state_shift[i], t5_table, lam_init, lp)
    hk = (N_HEADS_A, 2, HEAD_DIM_A)
    hv = (N_HEADS_A, 2 * HEAD_DIM_A)
    return (yp[None], ys,
            k_new[0].reshape((1, 1, t) + hk), v_new[0].reshape((1, 1, t) + hv),
            wkv_fin[:1][None], shift_p.reshape(1, 1, 1, -1),
            k_new[1].reshape((1, nb, ts) + hk), v_new[1].reshape((1, nb, ts) + hv),
            wkv_fin[1:][None], shift_s.reshape(1, nb, 1, -1))
```

```python
import functools
import math

import jax
import jax.numpy as jnp
from jax import lax
from jax.experimental import pallas as pl
from jax.experimental.pallas import tpu as pltpu

F32 = jnp.float32
BF16 = jnp.bfloat16

LANES = 128
SUBLANES = 8
VMEM_BYTES_V7X = 64 * 1024 * 1024
VMEM_LIMIT = VMEM_BYTES_V7X - 6 * 1024 * 1024

CHUNK = 64
HEAD_DIM_A = 128
N_HEADS_A = 8
ATT_WIDTH = 2 * HEAD_DIM_A * N_HEADS_A
T5_BUCKETS = 32
HEAD_B = 64
N_HEADS_B = 32
RWKV_WIDTH = HEAD_B * N_HEADS_B
RANK_W = 96
RANK_A = 96
RANK_G = 256
LORA_COLS = 512
LNX_EPS = 64e-5
N_GROUPS = 8
EXP_PER_GROUP = 8
N_EXPERTS = N_GROUPS * EXP_PER_GROUP
TOP_K = 2
RMS_EPS = 1e-6
NEG_INF = -1e30
T5_LOG_THRESHOLDS = (12, 16, 23, 32, 46, 64, 91)
T5_FAR = 128

K_LO = LANES // N_HEADS_B
K_HI = HEAD_B // K_LO
V_BLK = HEAD_B // SUBLANES
SCAN_TB = 32

EXPERT_ROWS = 256


def _cparams(sem, vmem=VMEM_LIMIT):
    return pltpu.CompilerParams(dimension_semantics=sem, vmem_limit_bytes=vmem)


def _exact_div(a, b):
    assert a % b == 0, (a, b)
    return a // b


def _sigmoid(x):
    return 1.0 / (1.0 + jnp.exp(-x))


def _part_blocks(parts, tm):
    edges = [0]
    for p in parts:
        edges.append(edges[-1] + _exact_div(p.shape[0], tm))
    return list(zip(edges[:-1], edges[1:]))


def _part_spec(block, lo, hi, row_axis_arg, const_index):
    def index_map(*grid):
        return (jnp.clip(grid[row_axis_arg] - lo, 0, hi - lo - 1),) + const_index(*grid)
    return pl.BlockSpec(block, index_map)


def _rmsnorm_kernel(*refs, ranges):
    x_refs, g_ref, o_ref = refs[:len(ranges)], refs[len(ranges)], refs[len(ranges) + 1]
    i = pl.program_id(0)
    for x_ref, (lo, hi) in zip(x_refs, ranges):
        @pl.when((i >= lo) & (i < hi))
        def _(x_ref=x_ref):
            x = x_ref[...]
            ms = jnp.mean(x * x, axis=-1, keepdims=True)
            o_ref[...] = (x * lax.rsqrt(ms + RMS_EPS) * g_ref[...]).astype(o_ref.dtype)


def rmsnorm_cast(parts, g, tm=256):
    parts = list(parts)
    d = parts[0].shape[1]
    for part in parts:
        tm = math.gcd(tm, part.shape[0])
    ranges = _part_blocks(parts, tm)
    return pl.pallas_call(
        functools.partial(_rmsnorm_kernel, ranges=ranges),
        out_shape=jax.ShapeDtypeStruct((ranges[-1][1] * tm, d), BF16),
        grid=(ranges[-1][1],),
        in_specs=[_part_spec((tm, d), lo, hi, 0, lambda i: (0,)) for lo, hi in ranges]
        + [pl.BlockSpec((1, d), lambda i: (0, 0))],
        out_specs=pl.BlockSpec((tm, d), lambda i: (i, 0)),
        compiler_params=_cparams(("arbitrary",)),
        name="rmsnorm_cast",
    )(*parts, g.reshape(1, d))


def _mm_kernel(*refs, n_pair, cast, transposed, ragged, mode, res_ranges, out_ranges):
    refs = list(refs)
    a_refs, w_refs = refs[:n_pair], refs[n_pair:2 * n_pair]
    del refs[:2 * n_pair]
    res_refs = [refs.pop(0) for _ in res_ranges]
    p_ref, pw_ref = (refs.pop(0), refs.pop(0)) if mode == "ple" else (None, None)
    o_refs = [refs.pop(0) for _ in out_ranges]
    wb_refs = [refs.pop(0) if cast[i] else w_refs[i] for i in range(n_pair)]
    i = pl.program_id(1)

    def product(k):
        dims = _NT if transposed[k] else (((1,), (0,)), ((), ()))
        return lax.dot_general(a_refs[k][...], wb_refs[k][...], dims, preferred_element_type=F32)

    @pl.when(i == 0)
    def _():
        for k in range(n_pair):
            if cast[k]:
                w = w_refs[k][...]
                if ragged[k] is not None:
                    col0, n_valid = ragged[k]
                    row = lax.broadcasted_iota(jnp.int32, w.shape, 0) + (pl.program_id(0) + col0) * w.shape[0]
                    w = jnp.where(row < n_valid, w, 0.0)
                wb_refs[k][...] = w.astype(BF16)

    acc = product(0)
    for k in range(1, n_pair):
        acc = acc + product(k)
    if mode == "ple":
        acc = jnp.dot(p_ref[...], pw_ref[...].astype(BF16), preferred_element_type=F32) * _sigmoid(acc)

    def finish(res_ref):
        val = acc if res_ref is None else res_ref[...] + acc
        for o_ref, (lo, hi) in zip(o_refs, out_ranges):
            if len(o_refs) == 1:
                o_ref[...] = val.astype(o_ref.dtype)
            else:
                @pl.when((i >= lo) & (i < hi))
                def _(o_ref=o_ref):
                    o_ref[...] = val.astype(o_ref.dtype)

    if not res_refs:
        finish(None)
    elif len(res_refs) == 1:
        finish(res_refs[0])
    else:
        for res_ref, (lo, hi) in zip(res_refs, res_ranges):
            @pl.when((i >= lo) & (i < hi))
            def _(res_ref=res_ref):
                finish(res_ref)


def matmul(pairs, n_cols, *, mode="plain", res=None, p=None, pw=None, out_dtype=F32, out_rows=None, tm=1024, tn=512):
    pairs = [tuple(pr) + (False,) * (5 - len(pr)) for pr in pairs]
    m = pairs[0][0].shape[0]
    res_parts = [] if res is None else (list(res) if isinstance(res, (list, tuple)) else [res])
    out_rows = [m] if out_rows is None else list(out_rows)
    for r in [m] + [part.shape[0] for part in res_parts] + out_rows:
        tm = math.gcd(tm, r)
    assert n_cols % tn == 0
    in_specs = [pl.BlockSpec((tm, a.shape[1]), lambda n, i: (i, 0)) for a, _, _, _, _ in pairs]
    w_blocks = [(tn, a.shape[1]) if tr else (a.shape[1], tn) for a, _, _, _, tr in pairs]
    in_specs += [pl.BlockSpec(blk, (lambda n, i, rb=rb, cb=cb: (n + cb, rb)) if tr else
                              (lambda n, i, rb=rb, cb=cb: (rb, n + cb)))
                 for blk, (_, _, rb, cb, tr) in zip(w_blocks, pairs)]
    args = [pr[0] for pr in pairs] + [pr[1] for pr in pairs]
    cast = tuple(pr[1].dtype != BF16 for pr in pairs)
    transposed = tuple(pr[4] for pr in pairs)
    ragged = tuple((cb, w.shape[0]) if tr and cast_k and (cb * tn + n_cols > w.shape[0]) else None
                   for (_, w, _, cb, tr), cast_k in zip(pairs, cast))
    res_ranges = _part_blocks(res_parts, tm)
    in_specs += [_part_spec((tm, tn), lo, hi, 1, lambda n, i: (n,)) for lo, hi in res_ranges]
    args += res_parts
    if mode == "ple":
        kp = p.shape[1]
        in_specs += [pl.BlockSpec((tm, kp), lambda n, i: (i, 0)), pl.BlockSpec((kp, tn), lambda n, i: (0, n))]
        args += [p, pw]
    out_ranges = _part_blocks([jax.ShapeDtypeStruct((r, n_cols), out_dtype) for r in out_rows], tm)
    assert out_ranges[-1][1] * tm == m
    outs = pl.pallas_call(
        functools.partial(_mm_kernel, n_pair=len(pairs), cast=cast, transposed=transposed, ragged=ragged, mode=mode,
                          res_ranges=res_ranges, out_ranges=out_ranges),
        out_shape=tuple(jax.ShapeDtypeStruct((r, n_cols), out_dtype) for r in out_rows),
        grid=(n_cols // tn, m // tm),
        in_specs=in_specs,
        out_specs=tuple(_part_spec((tm, tn), lo, hi, 1, lambda n, i: (n,)) for lo, hi in out_ranges),
        scratch_shapes=[pltpu.VMEM(blk, BF16) for blk, c in zip(w_blocks, cast) if c],
        compiler_params=_cparams(("arbitrary", "arbitrary")),
        name="matmul_" + mode,
    )(*args)
    return outs[0] if len(outs) == 1 else outs


def _perm_rows_kernel(pt_ref, w_ref, o_ref):
    o_ref[...] = jnp.dot(pt_ref[...], w_ref[...].astype(BF16), preferred_element_type=F32).astype(BF16)


def permute_weight_rows(w, perm_t, row_block0, n_sections, tn=512):
    c = RWKV_WIDTH
    n = w.shape[1]
    return pl.pallas_call(
        _perm_rows_kernel,
        out_shape=jax.ShapeDtypeStruct((n_sections * c, n), BF16),
        grid=(n_sections, _exact_div(n, tn)),
        in_specs=[pl.BlockSpec((c, c), lambda sec, j: (0, 0)),
                  pl.BlockSpec((c, tn), lambda sec, j: (row_block0 + sec, j))],
        out_specs=pl.BlockSpec((c, tn), lambda sec, j: (sec, j)),
        compiler_params=_cparams(("parallel", "parallel")),
        name="permute_weight_rows",
    )(perm_t, w)


def _qk_norm_kernel(*refs, ranges):
    q_ref, k_ref, v_ref, qg_ref, kg_ref, qo_ref, kbo_ref, vo_ref = refs[:8]
    n = len(ranges)
    k_outs, v_outs, kbuf = refs[8:8 + n], refs[8 + n:8 + 2 * n], refs[8 + 2 * n]

    def head_norm(x, g):
        ms = jnp.mean(x * x, axis=-1, keepdims=True)
        return x * lax.rsqrt(ms + RMS_EPS) * g

    qg = qg_ref[...]
    kg = kg_ref[...]
    for c in range(ATT_WIDTH // HEAD_DIM_A):
        sl = slice(c * HEAD_DIM_A, (c + 1) * HEAD_DIM_A)
        qn = head_norm(q_ref[:, sl], qg)
        qo_ref[:, sl] = (qn * (LOG2E * HEAD_DIM_A ** -0.5)).astype(BF16)
        kn = head_norm(k_ref[:, sl], kg)
        kbuf[:, sl] = kn
        kbo_ref[:, sl] = kn.astype(BF16)
    vo_ref[...] = v_ref[...].astype(BF16)
    i = pl.program_id(0)
    for k_out, v_out, (lo, hi) in zip(k_outs, v_outs, ranges):
        @pl.when((i >= lo) & (i < hi))
        def _(k_out=k_out, v_out=v_out):
            _rows_to_token_tiles(kbuf, k_out)
            _rows_to_token_tiles(v_ref, v_out, lane_tiles=2)


def qk_norm(proj_qkv, q_g, k_g, part_rows, tm=256):
    m = proj_qkv.shape[0]
    w = ATT_WIDTH
    hw = 2 * HEAD_DIM_A
    for r in part_rows:
        tm = math.gcd(tm, r)
    ranges = _part_blocks([jax.ShapeDtypeStruct((r, w), F32) for r in part_rows], tm)
    blk = lambda c: pl.BlockSpec((tm, w), lambda i, c=c: (i, c))
    vec = pl.BlockSpec((1, HEAD_DIM_A), lambda i: (0, 0))
    out_blk = pl.BlockSpec((tm, w), lambda i: (i, 0))
    rows = jax.ShapeDtypeStruct((m, w), BF16)
    outs = pl.pallas_call(
        functools.partial(_qk_norm_kernel, ranges=ranges),
        out_shape=(rows, rows, rows)
        + tuple(jax.ShapeDtypeStruct((r, w // HEAD_DIM_A, HEAD_DIM_A), F32) for r in part_rows)
        + tuple(jax.ShapeDtypeStruct((r, w // hw, hw), F32) for r in part_rows),
        grid=(_exact_div(m, tm),),
        in_specs=[blk(0), blk(1), blk(2), vec, vec],
        out_specs=(out_blk, out_blk, out_blk)
        + tuple(_part_spec((tm, w // HEAD_DIM_A, HEAD_DIM_A), lo, hi, 0, lambda i: (0, 0)) for lo, hi in ranges)
        + tuple(_part_spec((tm, w // hw, hw), lo, hi, 0, lambda i: (0, 0)) for lo, hi in ranges),
        scratch_shapes=[pltpu.VMEM((tm, w), F32)],
        compiler_params=_cparams(("arbitrary",)),
        name="qk_norm",
    )(proj_qkv, proj_qkv, proj_qkv, q_g.reshape(1, -1), k_g.reshape(1, -1))
    n = len(part_rows)
    return outs[0], outs[1], outs[2], outs[3:3 + n], outs[3 + n:]


def _bias_kernel(tab_ref, o_ref, *, rel0, masked, key_major, n_valid):
    _, nr, nc = o_ref.shape
    r = lax.broadcasted_iota(jnp.int32, (nr, nc), 0)
    c = lax.broadcasted_iota(jnp.int32, (nr, nc), 1)
    kpos, qpos = (r, c) if key_major else (c, r)
    rel = rel0 + kpos - qpos
    n = jnp.abs(rel)
    large = jnp.full((nr, nc), T5_BUCKETS // 4, jnp.int32)
    for thr in T5_LOG_THRESHOLDS:
        large = large + jnp.where(n >= thr, 1, 0)
    bucket = jnp.where(n < T5_BUCKETS // 4, n, large) + jnp.where(rel > 0, T5_BUCKETS // 2, 0)
    if masked:
        visible = (kpos // CHUNK) <= (qpos // CHUNK)
    for h in range(N_HEADS_A):
        acc = jnp.zeros((nr, nc), F32)
        for b in range(T5_BUCKETS):
            acc = jnp.where(bucket == b, tab_ref[b, h] * LOG2E, acc)
        if masked:
            acc = jnp.where(visible, acc, NEG_INF)
        if n_valid is not None:
            acc = jnp.where(c < n_valid, acc, NEG_INF)
        o_ref[h] = acc


def bias_tiles(table, nr, nc, *, rel0, masked=False, key_major=False, n_valid=None):
    return pl.pallas_call(
        functools.partial(_bias_kernel, rel0=rel0, masked=masked, key_major=key_major, n_valid=n_valid),
        out_shape=jax.ShapeDtypeStruct((N_HEADS_A, nr, nc), F32),
        in_specs=[pl.BlockSpec(memory_space=pltpu.SMEM)],
        out_specs=pl.BlockSpec(memory_space=pltpu.VMEM),
        compiler_params=_cparams(None),
        name="t5_bias",
    )(table)


def _lambda_value(lq1, lk1, lq2, lk2, lam_init):
    s1 = jnp.sum(lq1 * lk1, axis=-1, keepdims=True)
    s2 = jnp.sum(lq2 * lk2, axis=-1, keepdims=True)
    return jnp.exp(s1) - jnp.exp(s2) + lam_init


def _online_step(s, bias, v, m_prev, l_prev, acc_prev):
    nchunk = max(s.shape[1] // LANES, 1)
    width = s.shape[1] // nchunk
    sc = [s[:, c * width:(c + 1) * width] + bias(c * width, width) for c in range(nchunk)]
    m_new = jnp.maximum(m_prev, jnp.max(functools.reduce(jnp.maximum, sc), axis=-1, keepdims=True))
    alpha = jnp.exp2(m_prev - m_new)
    p = [jnp.exp2(x - m_new) for x in sc]
    l_new = alpha * l_prev + jnp.sum(functools.reduce(lambda a, b: a + b, p), axis=-1, keepdims=True)
    pb = p[0].astype(BF16) if nchunk == 1 else jnp.concatenate([x.astype(BF16) for x in p], axis=1)
    return m_new, l_new, alpha * acc_prev + jnp.dot(pb, v, preferred_element_type=F32)


def _diff_finish(acc1, l1, acc2, l2, lam, g, lam_init):
    o = acc1 / l1 - lam * (acc2 / l2)
    ms = jnp.mean(o * o, axis=-1, keepdims=True)
    return (o * lax.rsqrt(ms + RMS_EPS) * g) * (1.0 - lam_init)


_NT = (((1,), (1,)), ((), ()))
BF16_SUBLANES = 16
VT_ROWS = 2 * HEAD_DIM_A + BF16_SUBLANES
LOG2E = math.log2(math.e)


def _prompt_attn_kernel(far_ref, q_ref, k_ref, vt_ref, bd_ref, bl_ref, lq1, lk1, lq2, lk2, g_ref, ya_hbm,
                        o_ref, m_ref, acc_ref, *, blk, nsub, lam_init):
    h = pl.program_id(0)
    i = pl.program_id(1)
    m_ref[...] = jnp.full(m_ref.shape, NEG_INF, F32)
    acc_ref[...] = jnp.zeros(acc_ref.shape, F32)
    d = HEAD_DIM_A
    hw = 2 * d
    far_bias = far_ref[h]

    def chain(st, bias_tile, vt, idx):
        shift = far_bias if bias_tile is None else None
        if bias_tile is not None:
            st = st + bias_tile[0]
        col_max = jnp.max(st, axis=0, keepdims=True)
        if shift is not None:
            col_max = col_max + shift
        m_prev = m_ref[idx]
        m_new = jnp.maximum(m_prev, col_max)
        alpha = jnp.exp2(m_prev - m_new)
        p = jnp.exp2(st - (m_new if shift is None else m_new - shift))
        acc_ref[idx] = alpha * acc_ref[idx] + jnp.dot(vt, p.astype(BF16), preferred_element_type=F32)
        m_ref[idx] = m_new

    def update(j, ahead):
        kb = k_ref[pl.ds(pl.multiple_of(j * blk, blk), blk), :]
        vt = vt_ref[j]
        todo = []
        for sa in range(nsub):
            if ahead[sa] < 0:
                continue
            bias_tile = None if ahead[sa] >= 2 else (bl_ref if ahead[sa] == 1 else bd_ref)
            for mp in range(2):
                st = lax.dot_general(kb[:, mp * d:(mp + 1) * d], q_ref[sa * blk:(sa + 1) * blk, mp * d:(mp + 1) * d],
                                     _NT, preferred_element_type=F32)
                todo.append((st, bias_tile, 2 * sa + mp))
        for st, bias_tile, idx in todo:
            chain(st, bias_tile, vt, idx)

    def far_body(j, carry):
        update(j, [2] * nsub)
        return carry

    lax.fori_loop(0, jnp.maximum(nsub * i - 1, 0), far_body, 0)

    @pl.when(i >= 1)
    def _():
        update(nsub * i - 1, [sa + 1 for sa in range(nsub)])

    for o in range(nsub):
        update(nsub * i + o, [sa - o for sa in range(nsub)])
    lam = _lambda_value(lq1[...], lk1[...], lq2[...], lk2[...], lam_init)
    for sa in range(nsub):
        a1 = acc_ref[2 * sa]
        a2 = acc_ref[2 * sa + 1]
        ot = a1[:hw] / a1[hw:hw + 1] - lam * (a2[:hw] / a2[hw:hw + 1])
        ms = jnp.mean(ot * ot, axis=0, keepdims=True)
        yt = ot * lax.rsqrt(ms + RMS_EPS)
        o_ref[sa * blk:(sa + 1) * blk, :] = ((yt.T * g_ref[...]) * (1.0 - lam_init)).astype(BF16)


def prompt_attention(qn, kn, vt, ya, t, bias_d, bias_l, far_bias, lams, subln_g, lam_init, blk, nsub=2):
    nsub = min(nsub, t // blk)
    bq = nsub * blk
    hw = 2 * HEAD_DIM_A
    vec = pl.BlockSpec((1, HEAD_DIM_A), lambda h, i, far: (0, 0))
    return pl.pallas_call(
        functools.partial(_prompt_attn_kernel, blk=blk, nsub=nsub, lam_init=lam_init),
        out_shape=jax.ShapeDtypeStruct(ya.shape, BF16),
        input_output_aliases={11: 0},
        grid_spec=pltpu.PrefetchScalarGridSpec(
            num_scalar_prefetch=1,
            grid=(N_HEADS_A, _exact_div(t, bq)),
            in_specs=[pl.BlockSpec((bq, hw), lambda h, i, far: (i, h)),
                      pl.BlockSpec((t, hw), lambda h, i, far: (0, h)),
                      pl.BlockSpec((t // blk, VT_ROWS, blk), lambda h, i, far: (0, h, 0)),
                      pl.BlockSpec((1, blk, blk), lambda h, i, far: (h, 0, 0)),
                      pl.BlockSpec((1, blk, blk), lambda h, i, far: (h, 0, 0)),
                      vec, vec, vec, vec,
                      pl.BlockSpec((1, hw), lambda h, i, far: (0, 0)),
                      pl.BlockSpec(memory_space=pl.ANY)],
            out_specs=pl.BlockSpec((bq, hw), lambda h, i, far: (i, h)),
            scratch_shapes=[pltpu.VMEM((2 * nsub, 1, blk), F32), pltpu.VMEM((2 * nsub, VT_ROWS, blk), F32)]),
        compiler_params=_cparams(("arbitrary", "arbitrary")),
        name="prompt_attention",
    )(far_bias, qn, kn, vt, bias_d, bias_l, *lams, subln_g.reshape(1, hw), ya)


def _sublane_transpose8(tiles, sub):
    a = list(tiles)
    for dist in (4, 2, 1):
        keep = (sub % (2 * dist)) < dist
        nxt = list(a)
        for i in range(SUBLANES):
            if i % (2 * dist) < dist:
                x, y = a[i], a[i + dist]
                nxt[i] = jnp.where(keep, x, pltpu.roll(y, dist, axis=0))
                nxt[i + dist] = jnp.where(keep, pltpu.roll(x, SUBLANES - dist, axis=0), y)
        a = nxt
    return a


def _sample_attn_kernel(q_ref, ck_ref, cv_ref, kn_ref, vn_ref, bp_ref, bn_ref, lq1, lk1, lq2, lk2, g_ref,
                        ya_hbm, o_ref, m_ref, l_ref, acc_ref, ks_ref, vs_ref, *, lam_init):
    t = pl.program_id(1)
    last = t == pl.num_programs(1) - 1
    d = HEAD_DIM_A
    tk = ck_ref.shape[1]
    ts = q_ref.shape[0]
    n = 2 * N_HEADS_A
    sub = lax.broadcasted_iota(jnp.int32, (SUBLANES, LANES), 0)

    @pl.when(t == 0)
    def _():
        m_ref[...] = jnp.full(m_ref.shape, NEG_INF, F32)
        l_ref[...] = jnp.zeros(l_ref.shape, F32)
        acc_ref[...] = jnp.zeros(acc_ref.shape, F32)
        ks_ref[:, pl.ds(tk, LANES), :] = jnp.zeros((n, LANES, d), BF16)
        vs_ref[:, pl.ds(tk, LANES), :] = jnp.zeros((N_HEADS_A, LANES, 2 * d), BF16)
        for c in range(n):
            ks_ref[c, pl.ds(tk, ts), :] = kn_ref[:, c * d:(c + 1) * d]
        for h in range(N_HEADS_A):
            vs_ref[h, pl.ds(tk, ts), :] = vn_ref[:, 2 * h * d:2 * (h + 1) * d]

    def to_head_major(src, dst, p0, lanes):
        halves = [_sublane_transpose8(src[SUBLANES * a:SUBLANES * (a + 1)], sub) for a in range(2)]
        for r in range(SUBLANES):
            dst(r)[pl.ds(p0, BF16_SUBLANES), lanes] = jnp.concatenate([halves[0][r], halves[1][r]], 0).astype(BF16)

    def relayout(g, carry):
        p0 = pl.multiple_of(g * BF16_SUBLANES, BF16_SUBLANES)
        kt = ck_ref[0, pl.ds(p0, BF16_SUBLANES), :, :]
        vt = cv_ref[0, pl.ds(p0, BF16_SUBLANES), :, :]
        for a in range(2):
            rows = slice(a * SUBLANES, (a + 1) * SUBLANES)
            to_head_major([kt[p, rows, :] for p in range(BF16_SUBLANES)], lambda r, a=a: ks_ref.at[a * SUBLANES + r],
                          p0, slice(None))
            lanes = slice(a * LANES, (a + 1) * LANES)
            to_head_major([vt[p, :, lanes] for p in range(BF16_SUBLANES)], lambda r: vs_ref.at[r], p0, lanes)
        return carry

    lax.fori_loop(0, tk // BF16_SUBLANES, relayout, 0)

    state = [(m_ref[c], l_ref[c], acc_ref[c]) for c in range(n)]
    logits = [lax.dot_general(q_ref[:, c * d:(c + 1) * d], ks_ref[c], _NT, preferred_element_type=F32)
              for c in range(n)]
    for c in range(n):
        h = c // 2

        def bias(k0, w, h=h):
            if k0 < tk:
                return bp_ref[h, :, k0:k0 + w]
            return jnp.where(last, bn_ref[h], NEG_INF)

        state[c] = _online_step(logits[c], bias, vs_ref[h], *state[c])
    for c in range(n):
        m_ref[c], l_ref[c], acc_ref[c] = state[c]

    @pl.when(last)
    def _():
        lam = _lambda_value(lq1[...], lk1[...], lq2[...], lk2[...], lam_init)
        for h in range(N_HEADS_A):
            y = _diff_finish(acc_ref[2 * h], l_ref[2 * h], acc_ref[2 * h + 1], l_ref[2 * h + 1], lam,
                             g_ref[...], lam_init)
            o_ref[:, 2 * h * d:2 * (h + 1) * d] = y.astype(BF16)


def sample_attention(qn, kn, vn, ya, row0, cache_k, cache_v, bias_past, bias_new, lams, subln_g, lam_init, tk=1024):
    nb, past = cache_k.shape[:2]
    w = ATT_WIDTH
    ts = bias_new.shape[1]
    tk = min(tk, past)
    blk0 = row0 // ts
    hw = 2 * HEAD_DIM_A
    vec = pl.BlockSpec((1, HEAD_DIM_A), lambda b, t: (0, 0))
    new_rows = pl.BlockSpec((ts, w), lambda b, t: (blk0 + b, 0))
    return pl.pallas_call(
        functools.partial(_sample_attn_kernel, lam_init=lam_init),
        out_shape=jax.ShapeDtypeStruct(ya.shape, BF16),
        grid=(nb, _exact_div(past, tk)),
        in_specs=[new_rows,
                  pl.BlockSpec((1, tk, 2 * N_HEADS_A, HEAD_DIM_A), lambda b, t: (b, t, 0, 0)),
                  pl.BlockSpec((1, tk, N_HEADS_A, hw), lambda b, t: (b, t, 0, 0)),
                  new_rows, new_rows,
                  pl.BlockSpec((N_HEADS_A, ts, tk), lambda b, t: (0, 0, t)),
                  pl.BlockSpec((N_HEADS_A, ts, LANES), lambda b, t: (0, 0, 0)),
                  vec, vec, vec, vec,
                  pl.BlockSpec((1, hw), lambda b, t: (0, 0)),
                  pl.BlockSpec(memory_space=pl.ANY)],
        out_specs=pl.BlockSpec((ts, w), lambda b, t: (blk0 + b, 0)),
        input_output_aliases={12: 0},
        scratch_shapes=[pltpu.VMEM((2 * N_HEADS_A, ts, 1), F32), pltpu.VMEM((2 * N_HEADS_A, ts, 1), F32),
                        pltpu.VMEM((2 * N_HEADS_A, ts, hw), F32),
                        pltpu.VMEM((2 * N_HEADS_A, tk + LANES, HEAD_DIM_A), BF16),
                        pltpu.VMEM((N_HEADS_A, tk + LANES, hw), BF16)],
        compiler_params=_cparams(("arbitrary", "arbitrary")),
        name="sample_attention",
    )(qn, cache_k, cache_v, kn, vn, bias_past, bias_new, *lams, subln_g.reshape(1, hw), ya)


def _group_allreduce(x):
    r1 = pltpu.roll(x, N_HEADS_B, axis=1)
    r2 = pltpu.roll(x, 2 * N_HEADS_B, axis=1)
    r3 = pltpu.roll(x, 3 * N_HEADS_B, axis=1)
    return (x + r1) + (r2 + r3)


def _head_sum128(x):
    acc = x[:, 0:LANES]
    for c in range(1, K_HI):
        acc = acc + x[:, c * LANES:(c + 1) * LANES]
    return _group_allreduce(acc)


def _tile16(x128):
    return jnp.concatenate([x128] * K_HI, axis=1)


LORA_USED = RANK_W + RANK_A + RANK_G


def _token_shift(x_ref, pv_ref, st_ref, buf_ref, mu_ref, is_first, is_sample):
    tm = x_ref.shape[0]
    x = x_ref[...]
    buf_ref[pl.ds(0, SUBLANES), :] = jnp.where(is_first, 0.0, pv_ref[...])
    buf_ref[pl.ds(SUBLANES, tm), :] = x
    shifted = buf_ref[pl.ds(SUBLANES - 1, tm), :]
    starts = is_sample & (lax.broadcasted_iota(jnp.int32, (SCAN_TB, 1), 0) == 0)
    prev = jnp.concatenate([jnp.where(starts, st_ref[0, g:g + 1, :], shifted[g * SCAN_TB:(g + 1) * SCAN_TB])
                            for g in range(tm // SCAN_TB)], 0)
    return x + (prev - x) * mu_ref[...]


def _rwkv_prep_kernel(f_ref, pf_ref, sf_ref, lo_ref, plo_ref, slo_ref, mu_ref, mul_ref, w0_ref, a0_ref, kk_ref,
                      ka_ref, rk_ref, w2_ref, a2_ref, g2_ref,
                      ak_o, wr_o, w_o, b_o, km_o, v_o, g_o, br_o, kr_o, bo_o, fbuf, lbuf, *, n_prompt_blocks):
    c = RWKV_WIDTH
    i = pl.program_id(0)
    xm = _token_shift(f_ref, pf_ref, sf_ref, fbuf, mu_ref, i == 0, i >= n_prompt_blocks)
    xl = _token_shift(lo_ref, plo_ref, slo_ref, lbuf, mul_ref, i == 0, i >= n_prompt_blocks)
    r, k, v = xm[:, :c], xm[:, c:2 * c], xm[:, 2 * c:]
    wd, ad, gd = xl[:, :LANES], xl[:, :2 * LANES], xl[:, LANES:]
    lw = w0_ref[...] + jnp.dot(jnp.tanh(wd).astype(BF16), w2_ref[...].astype(BF16), preferred_element_type=F32)
    z = -lw
    softplus = jnp.maximum(z, 0.0) + jnp.log(1.0 + jnp.exp(-jnp.abs(z)))
    log_decay = -jnp.exp(-softplus - 0.5)
    tm = lw.shape[0]
    ri = lax.broadcasted_iota(jnp.int32, (tm, tm), 0)
    ci = lax.broadcasted_iota(jnp.int32, (tm, tm), 1)
    same_run_upto = ((ri // SCAN_TB) == (ci // SCAN_TB)) & (ci <= ri)
    csum = jnp.dot(jnp.where(same_run_upto, 1.0, 0.0), log_decay, precision=lax.Precision.HIGHEST,
                   preferred_element_type=F32)
    p_incl = jnp.exp(csum)
    p_excl = jnp.exp(csum - log_decay)
    inv_incl = jnp.exp(-csum)
    a = _sigmoid(a0_ref[...] + jnp.dot(ad.astype(BF16), a2_ref[...].astype(BF16), preferred_element_type=F32))
    g_o[...] = jnp.dot(_sigmoid(gd).astype(BF16), g2_ref[...].astype(BF16), preferred_element_type=F32)
    kk = k * kk_ref[...]
    norm = jnp.maximum(jnp.sqrt(_tile16(_head_sum128(kk * kk))), 1e-12)
    kk = kk / norm
    kmod = k * (1.0 + (a - 1.0) * ka_ref[...])
    bvec = kk * a
    ak_o[...] = -kk * p_excl
    wr_o[...] = p_incl * r
    w_o[...] = p_incl
    b_o[...] = bvec * inv_incl
    km_o[...] = kmod * inv_incl
    v_o[...] = v
    br_o[...] = _head_sum128(bvec * r)
    kr_o[...] = _head_sum128(kmod * r)
    bo_o[...] = _head_sum128(r * kmod * rk_ref[...])


def rwkv_prep(feat, state, lora, state_lora, prm, n_prompt_rows, tm=128):
    m = feat.shape[0]
    c = RWKV_WIDTH
    row = lambda w: pl.BlockSpec((tm, w), lambda i: (i, 0))
    vec = lambda w: pl.BlockSpec((1, w), lambda i: (0, 0))
    mat = lambda r: pl.BlockSpec((r, c), lambda i: (0, 0))
    big = jax.ShapeDtypeStruct((m, c), F32)
    small = jax.ShapeDtypeStruct((m, LANES), F32)
    assert tm // SCAN_TB <= SUBLANES
    npb = _exact_div(n_prompt_rows, tm)
    prev8 = lambda w: pl.BlockSpec((SUBLANES, w), lambda i: (jnp.maximum(i * (tm // SUBLANES) - 1, 0), 0))
    st = lambda w: pl.BlockSpec((1, SUBLANES, w), lambda i: (jnp.maximum(i - npb, 0), 0, 0))
    return pl.pallas_call(
        functools.partial(_rwkv_prep_kernel, n_prompt_blocks=npb),
        out_shape=(big,) * 7 + (small,) * 3,
        grid=(_exact_div(m, tm),),
        in_specs=[row(3 * c), prev8(3 * c), st(3 * c), row(LORA_COLS), prev8(LORA_COLS), st(LORA_COLS),
                  vec(3 * c), vec(LORA_COLS), vec(c), vec(c), vec(c), vec(c), vec(c),
                  mat(LANES), mat(2 * LANES), mat(LORA_COLS - LANES)],
        out_specs=(row(c),) * 7 + (row(LANES),) * 3,
        scratch_shapes=[pltpu.VMEM((tm + SUBLANES, 3 * c), F32), pltpu.VMEM((tm + SUBLANES, LORA_COLS), F32)],
        compiler_params=_cparams(("arbitrary",)),
        name="rwkv_prep",
    )(feat, feat, state, lora, lora, state_lora, prm["mu_rkv"], prm["mu_lora"], prm["w0"], prm["a0"], prm["k_k"], prm["k_a"],
      prm["r_k"], prm["w2"], prm["a2"], prm["g2"])


def _scan_kernel(seq_ref, first_ref, last_ref, ak_ref, wr_ref, w_ref, b_ref, km_ref, v_ref, br_ref, s0_ref,
                 y_ref, sout_ref, s_ref):
    step = pl.program_id(0)

    @pl.when(first_ref[step] == 1)
    def _():
        s_ref[...] = s0_ref[0]

    sub = lax.broadcasted_iota(jnp.int32, (SUBLANES, LANES), 0)
    grp = lax.broadcasted_iota(jnp.int32, (SUBLANES, LANES), 1) // N_HEADS_B
    own_group = grp == (sub % K_LO)
    low_half = sub < K_LO

    def row(ref, t8, s, c):
        tile = ref[pl.ds(t8, SUBLANES), c * LANES:(c + 1) * LANES]
        return jnp.broadcast_to(tile[s:s + 1], (SUBLANES, LANES))

    def time_step(t8, s, y_lo, y_hi):
        vt = []
        for vb in range(V_BLK):
            tile = jnp.where(low_half, row(v_ref, t8, s, 2 * vb), row(v_ref, t8, s, 2 * vb + 1))
            vt.append(_group_allreduce(jnp.where(own_group, tile, 0.0)))
        acc_u = [[None, None] for _ in range(V_BLK)]
        acc_y = [[None, None] for _ in range(V_BLK)]
        for kh in range(K_HI):
            a_row = row(ak_ref, t8, s, kh)
            wr_row = row(wr_ref, t8, s, kh)
            for vb in range(V_BLK):
                st = s_ref[vb, kh]
                pu, py = st * a_row, st * wr_row
                acc_u[vb][kh % 2] = pu if acc_u[vb][kh % 2] is None else acc_u[vb][kh % 2] + pu
                acc_y[vb][kh % 2] = py if acc_y[vb][kh % 2] is None else acc_y[vb][kh % 2] + py
        acc_u = [a + b for a, b in acc_u]
        acc_y = [a + b for a, b in acc_y]
        br_row = row(br_ref, t8, s, 0)
        u = [_group_allreduce(x) for x in acc_u]
        for vb in range(V_BLK):
            y = _group_allreduce(acc_y[vb]) + u[vb] * br_row
            ym = jnp.where(own_group, y, 0.0)
            ym = ym + pltpu.roll(ym, 1, axis=0)
            ym = ym + pltpu.roll(ym, 2, axis=0)
            y_lo[vb] = jnp.where(sub == s, pltpu.roll(ym, (s - (K_LO - 1)) % SUBLANES, axis=0), y_lo[vb])
            y_hi[vb] = jnp.where(sub == s, pltpu.roll(ym, (s - (2 * K_LO - 1)) % SUBLANES, axis=0), y_hi[vb])
        for kh in range(K_HI):
            b_row = row(b_ref, t8, s, kh)
            km_row = row(km_ref, t8, s, kh)
            for vb in range(V_BLK):
                s_ref[vb, kh] = s_ref[vb, kh] + (b_row * u[vb] + km_row * vt[vb])

    def eight_steps(gi, carry):
        t8 = pl.multiple_of(gi * SUBLANES, SUBLANES)
        y_lo = [jnp.zeros((SUBLANES, LANES), F32) for _ in range(V_BLK)]
        y_hi = [jnp.zeros((SUBLANES, LANES), F32) for _ in range(V_BLK)]
        for s in range(SUBLANES):
            time_step(t8, s, y_lo, y_hi)
        for vb in range(V_BLK):
            y_ref[pl.ds(t8, SUBLANES), (2 * vb) * LANES:(2 * vb + 1) * LANES] = y_lo[vb]
            y_ref[pl.ds(t8, SUBLANES), (2 * vb + 1) * LANES:(2 * vb + 2) * LANES] = y_hi[vb]
        return carry

    lax.fori_loop(0, y_ref.shape[0] // SUBLANES, eight_steps, 0)
    last8 = y_ref.shape[0] - SUBLANES
    for kh in range(K_HI):
        p_end = row(w_ref, last8, SUBLANES - 1, kh)
        for vb in range(V_BLK):
            s_ref[vb, kh] = s_ref[vb, kh] * p_end

    @pl.when(last_ref[step] == 1)
    def _():
        sout_ref[0] = s_ref[...]


def rwkv_scan(ops, br, s0, seq_of_step, first, last):
    m, c = ops[0].shape
    nseq = s0.shape[0]
    tb = SCAN_TB
    row = lambda w: pl.BlockSpec((tb, w), lambda i, sq, fi, la: (i, 0))
    st = pl.BlockSpec((1, V_BLK, K_HI, SUBLANES, LANES), lambda i, sq, fi, la: (sq[i], 0, 0, 0, 0))
    return pl.pallas_call(
        _scan_kernel,
        out_shape=(jax.ShapeDtypeStruct((m, c), F32), jax.ShapeDtypeStruct(s0.shape, F32)),
        grid_spec=pltpu.PrefetchScalarGridSpec(
            num_scalar_prefetch=3,
            grid=(m // tb,),
            in_specs=[row(c)] * 6 + [row(LANES), st],
            out_specs=(row(c), st),
            scratch_shapes=[pltpu.VMEM((V_BLK, K_HI, SUBLANES, LANES), F32)]),
        compiler_params=_cparams(("arbitrary",)),
        name="rwkv_scan",
    )(seq_of_step, first, last, *ops, br, s0)


def _rwkv_post_kernel(y_ref, v_ref, g_ref, kr_ref, bo_ref, lg_ref, lb_ref, o_ref):
    v = v_ref[...]
    y = y_ref[...] + v * _tile16(kr_ref[...])
    mean = _tile16(_head_sum128(y)) * (1.0 / HEAD_B)
    yc = y - mean
    var = _tile16(_head_sum128(yc * yc)) * (1.0 / HEAD_B)
    yn = yc * lax.rsqrt(var + LNX_EPS) * lg_ref[...] + lb_ref[...]
    o_ref[...] = ((yn + _tile16(bo_ref[...]) * v) * g_ref[...]).astype(BF16)


def rwkv_post(y, v, g, kr, bo, lnx_g, lnx_b, tm=256):
    m, c = y.shape
    row = lambda w: pl.BlockSpec((tm, w), lambda i: (i, 0))
    vec = pl.BlockSpec((1, c), lambda i: (0, 0))
    return pl.pallas_call(
        _rwkv_post_kernel,
        out_shape=jax.ShapeDtypeStruct((m, c), BF16),
        grid=(_exact_div(m, tm),),
        in_specs=[row(c), row(c), row(c), row(LANES), row(LANES), vec, vec],
        out_specs=row(c),
        compiler_params=_cparams(("parallel",)),
        name="rwkv_post",
    )(y, v, g, kr, bo, lnx_g, lnx_b)


def _rows_to_token_tiles(src_ref, dst_ref, lane_tiles=1):
    rows, cols = src_ref.shape
    sub = lax.broadcasted_iota(jnp.int32, (SUBLANES, LANES), 0)

    def body(g, carry):
        r0 = pl.multiple_of(g * SUBLANES, SUBLANES)
        for a in range(cols // (SUBLANES * LANES * lane_tiles)):
            for lt in range(lane_tiles):
                chunk = lambda i: (SUBLANES * a + i) * lane_tiles + lt
                tiles = [src_ref[pl.ds(r0, SUBLANES), chunk(i) * LANES:(chunk(i) + 1) * LANES]
                         for i in range(SUBLANES)]
                out = _sublane_transpose8(tiles, sub)
                for p in range(SUBLANES):
                    dst_ref[r0 + p, SUBLANES * a:SUBLANES * (a + 1), lt * LANES:(lt + 1) * LANES] = out[p]
        return carry

    lax.fori_loop(0, rows // SUBLANES, body, 0)


def _token_tiles_to_rows(src, n_rows, chunks, emit):
    sub = lax.broadcasted_iota(jnp.int32, (SUBLANES, LANES), 0)

    def body(g, carry):
        r0 = pl.multiple_of(g * BF16_SUBLANES, BF16_SUBLANES)
        for a in range(chunks // SUBLANES):
            halves = [_sublane_transpose8([src(r0 + SUBLANES * b + p)[SUBLANES * a:SUBLANES * (a + 1), :]
                                           for p in range(SUBLANES)], sub) for b in range(2)]
            for i in range(SUBLANES):
                emit(r0, SUBLANES * a + i, jnp.concatenate([halves[0][i], halves[1][i]], 0))
        return carry

    lax.fori_loop(0, n_rows // BF16_SUBLANES, body, 0)


def _router_kernel(x_ref, g_ref, rw_ref, rb_ref, h_ref, route_ref, hbuf):
    x = x_ref[...]
    ms = jnp.mean(x * x, axis=-1, keepdims=True)
    h = x * lax.rsqrt(ms + RMS_EPS) * g_ref[...]
    hbuf[...] = h
    _rows_to_token_tiles(hbuf, h_ref)
    logits = jnp.dot(h, rw_ref[...], precision=lax.Precision.HIGHEST, preferred_element_type=F32) + rb_ref[...]
    lane = lax.broadcasted_iota(jnp.int32, logits.shape, 1)
    big = jnp.int32(LANES)

    def first_argmax(vals, valid):
        masked = jnp.where(valid, vals, -jnp.inf)
        mx = jnp.max(masked, axis=-1, keepdims=True)
        idx = jnp.min(jnp.where(valid & (masked == mx), lane, big), axis=-1, keepdims=True)
        return mx, idx

    is_group = lane < N_GROUPS
    g_max, g_idx = first_argmax(logits, is_group)
    g_top = 1.0 / jnp.sum(jnp.where(is_group, jnp.exp(logits - g_max), 0.0), axis=-1, keepdims=True)
    in_group = (lane >= N_GROUPS) & (lane < N_GROUPS + N_EXPERTS) & ((lane - N_GROUPS) // EXP_PER_GROUP == g_idx)
    i_max, idx1 = first_argmax(logits, in_group)
    z = jnp.sum(jnp.where(in_group, jnp.exp(logits - i_max), 0.0), axis=-1, keepdims=True)
    i_max2, idx2 = first_argmax(logits, in_group & (lane != idx1))
    p1 = 1.0 / z
    p2 = jnp.exp(i_max2 - i_max) / z
    psum = p1 + p2
    gate1 = g_top * p1 / psum
    gate2 = g_top * p2 / psum
    route = jnp.where(lane == 0, (idx1 - N_GROUPS).astype(F32),
                      jnp.where(lane == 1, (idx2 - N_GROUPS).astype(F32),
                                jnp.where(lane == 2, gate1, jnp.where(lane == 3, gate2, 0.0))))
    route_ref[...] = route


def norm_router(x, g, rw, rb, tm=256):
    m, d = x.shape
    return pl.pallas_call(
        _router_kernel,
        out_shape=(jax.ShapeDtypeStruct((m, d // LANES, LANES), F32), jax.ShapeDtypeStruct((m, LANES), F32)),
        grid=(_exact_div(m, tm),),
        in_specs=[pl.BlockSpec((tm, d), lambda i: (i, 0)), pl.BlockSpec((1, d), lambda i: (0, 0)),
                  pl.BlockSpec((d, LANES), lambda i: (0, 0)), pl.BlockSpec((1, LANES), lambda i: (0, 0))],
        out_specs=(pl.BlockSpec((tm, d // LANES, LANES), lambda i: (i, 0, 0)),
                   pl.BlockSpec((tm, LANES), lambda i: (i, 0))),
        scratch_shapes=[pltpu.VMEM((tm, d), F32)],
        compiler_params=_cparams(("arbitrary",)),
        name="norm_router",
    )(x, g.reshape(1, d), rw, rb)


def _row_copy(src_hbm, buf, sem, slot, src_row, dst_row):
    return pltpu.make_async_copy(src_hbm.at[src_row], buf.at[slot, dst_row], sem.at[slot])


def _wait_slot(src_hbm, buf, sem, slot):
    pltpu.make_async_copy(src_hbm.at[pl.ds(0, buf.shape[1])], buf.at[slot], sem.at[slot]).wait()


ROW_ISSUE_UNROLL = 8


def _gather_rows_kernel(idx_ref, nused_ref, src_hbm, o_ref, buf, sem, *, rows):
    b = pl.program_id(0)
    nb = pl.num_programs(0)
    used = nused_ref[0]

    def issue(blk, slot):
        def body(r, c):
            _row_copy(src_hbm, buf, sem, slot, idx_ref[blk * rows + r], r).start()
            return c
        lax.fori_loop(0, rows, body, 0, unroll=ROW_ISSUE_UNROLL)

    @pl.when((b == 0) & (used > 0))
    def _():
        issue(0, 0)

    @pl.when((b + 1 < nb) & (b + 1 < used))
    def _():
        issue(b + 1, (b + 1) % 2)

    @pl.when(b < used)
    def _():
        slot = b % 2
        _wait_slot(src_hbm, buf, sem, slot)

        def emit(r0, j, x):
            o_ref[pl.ds(r0, BF16_SUBLANES), j * LANES:(j + 1) * LANES] = x.astype(o_ref.dtype)

        _token_tiles_to_rows(lambda r: buf[slot, r], rows, buf.shape[2], emit)

    @pl.when(b >= used)
    def _():
        o_ref[...] = jnp.zeros(o_ref.shape, o_ref.dtype)


def gather_rows(src, idx, n_used, rows, out_dtype):
    n = idx.shape[0]
    chunks = src.shape[1]
    d = chunks * LANES
    return pl.pallas_call(
        functools.partial(_gather_rows_kernel, rows=rows),
        out_shape=jax.ShapeDtypeStruct((n, d), out_dtype),
        grid_spec=pltpu.PrefetchScalarGridSpec(
            num_scalar_prefetch=2,
            grid=(n // rows,),
            in_specs=[pl.BlockSpec(memory_space=pl.ANY)],
            out_specs=pl.BlockSpec((rows, d), lambda b, idx, nu: (b, 0)),
            scratch_shapes=[pltpu.VMEM((2, rows, chunks, LANES), src.dtype), pltpu.SemaphoreType.DMA((2,))]),
        compiler_params=_cparams(("arbitrary",)),
        name="gather_rows",
    )(idx, n_used, src)


def _combine_kernel(idx_ref, src_hbm, x_ref, route_ref, o_ref, buf, sem, ybuf, *, rows):
    b = pl.program_id(0)
    nb = pl.num_programs(0)

    def issue(blk, slot):
        def body(r, c):
            base = (blk * rows + r) * TOP_K
            _row_copy(src_hbm, buf, sem, slot, idx_ref[base], r).start()
            _row_copy(src_hbm, buf, sem, slot, idx_ref[base + 1], rows + r).start()
            return c
        lax.fori_loop(0, rows, body, 0, unroll=ROW_ISSUE_UNROLL)

    @pl.when(b == 0)
    def _():
        issue(0, 0)

    @pl.when(b + 1 < nb)
    def _():
        issue(b + 1, (b + 1) % 2)

    slot = b % 2
    _wait_slot(src_hbm, buf, sem, slot)

    def emit(r0, j, x):
        ybuf[pl.ds(r0, BF16_SUBLANES), j * LANES:(j + 1) * LANES] = x

    _token_tiles_to_rows(lambda r: buf[slot, r], TOP_K * rows, buf.shape[2], emit)
    route = route_ref[...]
    o_ref[...] = x_ref[...] + (route[:, 2:3] * ybuf[pl.ds(0, rows), :] + route[:, 3:4] * ybuf[pl.ds(rows, rows), :])


def moe_combine(yb, dest, x, route, rows=128):
    m, d = x.shape
    chunks = yb.shape[1]
    return pl.pallas_call(
        functools.partial(_combine_kernel, rows=rows),
        out_shape=jax.ShapeDtypeStruct((m, d), F32),
        grid_spec=pltpu.PrefetchScalarGridSpec(
            num_scalar_prefetch=1,
            grid=(_exact_div(m, rows),),
            in_specs=[pl.BlockSpec(memory_space=pl.ANY),
                      pl.BlockSpec((rows, d), lambda b, idx: (b, 0)),
                      pl.BlockSpec((rows, LANES), lambda b, idx: (b, 0))],
            out_specs=pl.BlockSpec((rows, d), lambda b, idx: (b, 0)),
            scratch_shapes=[pltpu.VMEM((2, TOP_K * rows, chunks, LANES), F32), pltpu.SemaphoreType.DMA((2,)),
                            pltpu.VMEM((TOP_K * rows, d), F32)]),
        compiler_params=_cparams(("arbitrary",)),
        name="moe_combine",
    )(dest, yb, x, route)


def _expert_runs(block_e, n_used):
    nb = block_e.shape[0]
    idx = jnp.arange(nb, dtype=jnp.int32)
    valid = idx < n_used[0]
    first = valid & ((idx == 0) | (block_e != jnp.roll(block_e, 1)))
    upto = idx[None, :] <= idx[:, None]
    slot = (jnp.sum(jnp.where(upto & first[None, :], 1, 0), axis=1) - 1) % 2
    nxt = jnp.min(jnp.where(first[None, :] & ~upto, idx[None, :], nb), axis=1)
    next_e = jnp.where(nxt < nb, block_e[jnp.minimum(nxt, nb - 1)], -1)
    return first.astype(jnp.int32), slot.astype(jnp.int32), next_e.astype(jnp.int32)


def _stream_expert_weights(b, be_ref, first_ref, slot_ref, next_ref, copies):
    slot = slot_ref[b]

    @pl.when(b == 0)
    def _():
        for cp in copies(be_ref[0], 0):
            cp.start()

    for cp in copies(be_ref[b], slot):
        cp.wait()

    @pl.when(next_ref[b] >= 0)
    def _():
        for cp in copies(next_ref[b], 1 - slot):
            cp.start()

    return slot


WEIGHT_DMA_SPLIT = 4
EXPERT_K_CHUNK = 512


def _expert_up_kernel(be_ref, nused_ref, first_ref, slot_ref, next_ref, x_ref, wg_hbm, wu_hbm, o_ref,
                      wbuf, wgb_ref, wub_ref, sem, *, tf):
    f = pl.program_id(0)
    b = pl.program_id(1)

    def copies(e, slot):
        cols = pl.ds(pl.multiple_of(f * tf, tf), tf)
        rows_per = wbuf.shape[2] // WEIGHT_DMA_SPLIT
        return [pltpu.make_async_copy(w.at[e, pl.ds(q * rows_per, rows_per), cols],
                                      wbuf.at[slot, k, pl.ds(q * rows_per, rows_per)], sem.at[slot, k])
                for k, w in enumerate((wg_hbm, wu_hbm)) for q in range(WEIGHT_DMA_SPLIT)]

    def finish(g, u):
        o_ref[...] = (g * _sigmoid(g) * u).astype(BF16)

    live = b < nused_ref[0]

    @pl.when(live & (first_ref[b] == 1))
    def _():
        slot = _stream_expert_weights(b, be_ref, first_ref, slot_ref, next_ref, copies)
        d = x_ref.shape[1]
        g = jnp.zeros(o_ref.shape, F32)
        u = jnp.zeros(o_ref.shape, F32)
        for k0 in range(0, d, EXPERT_K_CHUNK):
            rows = pl.ds(k0, EXPERT_K_CHUNK)
            wgb_ref[rows, :] = wbuf[slot, 0, rows, :].astype(BF16)
            wub_ref[rows, :] = wbuf[slot, 1, rows, :].astype(BF16)
            xk = x_ref[:, k0:k0 + EXPERT_K_CHUNK]
            g = g + jnp.dot(xk, wgb_ref[rows, :], preferred_element_type=F32)
            u = u + jnp.dot(xk, wub_ref[rows, :], preferred_element_type=F32)
        finish(g, u)

    @pl.when(live & (first_ref[b] == 0))
    def _():
        x = x_ref[...]
        finish(jnp.dot(x, wgb_ref[...], preferred_element_type=F32),
               jnp.dot(x, wub_ref[...], preferred_element_type=F32))

    @pl.when(b >= nused_ref[0])
    def _():
        o_ref[...] = jnp.zeros(o_ref.shape, o_ref.dtype)


def expert_up(xg, block_e, n_used, runs, wg, wu, tf=512):
    n, d = xg.shape
    de = wg.shape[2]
    bm = EXPERT_ROWS
    live = lambda b, nu: jnp.minimum(b, jnp.maximum(nu[0] - 1, 0))
    return pl.pallas_call(
        functools.partial(_expert_up_kernel, tf=tf),
        out_shape=jax.ShapeDtypeStruct((n, de), BF16),
        grid_spec=pltpu.PrefetchScalarGridSpec(
            num_scalar_prefetch=5,
            grid=(de // tf, n // bm),
            in_specs=[pl.BlockSpec((bm, d), lambda f, b, be, nu, fi, sl, ne: (live(b, nu), 0)),
                      pl.BlockSpec(memory_space=pl.ANY), pl.BlockSpec(memory_space=pl.ANY)],
            out_specs=pl.BlockSpec((bm, tf), lambda f, b, be, nu, fi, sl, ne: (b, f)),
            scratch_shapes=[pltpu.VMEM((2, 2, d, tf), F32), pltpu.VMEM((d, tf), BF16), pltpu.VMEM((d, tf), BF16),
                            pltpu.SemaphoreType.DMA((2, 2))]),
        compiler_params=_cparams(("arbitrary", "arbitrary")),
        name="expert_up",
    )(block_e, n_used, *runs, xg, wg, wu)


def _expert_down_kernel(be_ref, nused_ref, first_ref, slot_ref, next_ref, h_ref, wd_hbm, o_ref, wbuf, wdb_ref, sem,
                        ybuf, *, tn):
    c = pl.program_id(0)
    b = pl.program_id(1)

    def copies(e, slot):
        cols = pl.ds(pl.multiple_of(c * tn, tn), tn)
        rows_per = wbuf.shape[1] // WEIGHT_DMA_SPLIT
        return [pltpu.make_async_copy(wd_hbm.at[e, pl.ds(q * rows_per, rows_per), cols],
                                      wbuf.at[slot, pl.ds(q * rows_per, rows_per)], sem.at[slot])
                for q in range(WEIGHT_DMA_SPLIT)]

    live = b < nused_ref[0]

    @pl.when(live & (first_ref[b] == 1))
    def _():
        slot = _stream_expert_weights(b, be_ref, first_ref, slot_ref, next_ref, copies)
        de = h_ref.shape[1]
        y = jnp.zeros(ybuf.shape, F32)
        for k0 in range(0, de, EXPERT_K_CHUNK):
            rows = pl.ds(k0, EXPERT_K_CHUNK)
            wdb_ref[rows, :] = wbuf[slot, rows, :].astype(BF16)
            y = y + jnp.dot(h_ref[:, k0:k0 + EXPERT_K_CHUNK], wdb_ref[rows, :], preferred_element_type=F32)
        ybuf[...] = y

    @pl.when(live & (first_ref[b] == 0))
    def _():
        ybuf[...] = jnp.dot(h_ref[...], wdb_ref[...], preferred_element_type=F32)

    @pl.when(live)
    def _():
        _rows_to_token_tiles(ybuf, o_ref)

    @pl.when(b >= nused_ref[0])
    def _():
        o_ref[...] = jnp.zeros(o_ref.shape, o_ref.dtype)


def expert_down(hmid, block_e, n_used, runs, wd, tn=2048):
    n, de = hmid.shape
    d = wd.shape[2]
    bm = EXPERT_ROWS
    live = lambda b, nu: jnp.minimum(b, jnp.maximum(nu[0] - 1, 0))
    return pl.pallas_call(
        functools.partial(_expert_down_kernel, tn=tn),
        out_shape=jax.ShapeDtypeStruct((n, d // LANES, LANES), F32),
        grid_spec=pltpu.PrefetchScalarGridSpec(
            num_scalar_prefetch=5,
            grid=(d // tn, n // bm),
            in_specs=[pl.BlockSpec((bm, de), lambda c, b, be, nu, fi, sl, ne: (live(b, nu), 0)),
                      pl.BlockSpec(memory_space=pl.ANY)],
            out_specs=pl.BlockSpec((bm, tn // LANES, LANES), lambda c, b, be, nu, fi, sl, ne: (b, c, 0)),
            scratch_shapes=[pltpu.VMEM((2, de, tn), F32), pltpu.VMEM((de, tn), BF16), pltpu.SemaphoreType.DMA((2,)),
                            pltpu.VMEM((bm, tn), F32)]),
        compiler_params=_cparams(("arbitrary", "arbitrary")),
        name="expert_down",
    )(block_e, n_used, *runs, hmid, wd)


def _perm_cols(x):
    pre = x.shape[:-1]
    return jnp.moveaxis(x.reshape(pre + (N_HEADS_B, K_HI, K_LO)), -3, -1).reshape(pre + (RWKV_WIDTH,))


def _unperm_cols(x):
    pre = x.shape[:-1]
    return jnp.moveaxis(x.reshape(pre + (K_HI, K_LO, N_HEADS_B)), -1, -3).reshape(pre + (RWKV_WIDTH,))


def _state_to_tiles(s):
    n = s.shape[0]
    s = s.reshape(n, N_HEADS_B, V_BLK, SUBLANES, K_HI, K_LO)
    return s.transpose(0, 2, 4, 3, 5, 1).reshape(n, V_BLK, K_HI, SUBLANES, LANES)


def _tiles_to_state(s):
    n = s.shape[0]
    s = s.reshape(n, V_BLK, K_HI, SUBLANES, K_LO, N_HEADS_B)
    return s.transpose(0, 5, 1, 3, 2, 4).reshape(n, N_HEADS_B, HEAD_B, HEAD_B)


def _pad_rows(w, row0, rows):
    return jnp.zeros((rows,) + w.shape[1:], w.dtype).at[row0:row0 + w.shape[0]].set(w)


def _trunk_layer(xp, xs, pp, ps, cache_k, cache_v, state_wkv, state_shift, t5_table, lam_init, lp, attn_blk=512):
    t, d = xp.shape
    nb, ts, _ = xs.shape
    past = cache_k.shape[1]
    ms = nb * ts
    m = t + ms
    aw = ATT_WIDTH
    c = RWKV_WIDTH
    assert ts == SCAN_TB and t % SCAN_TB == 0 and past % CHUNK == 0 and ts <= CHUNK

    x_parts = [xp, xs.reshape(ms, d)]
    pe = jnp.concatenate([pp, ps.reshape(ms, -1)], 0).astype(BF16)

    w_t = lp["w_in"].T
    rkv0 = 3 * aw
    src = jnp.arange(c, dtype=jnp.int32)
    dst = (src % HEAD_B // K_LO) * LANES + (src % K_LO) * N_HEADS_B + src // HEAD_B
    perm_t = (src[:, None] == dst[None, :]).astype(BF16)
    assert aw == c
    w_rkv_t = permute_weight_rows(w_t, perm_t, rkv0 // c, 3)
    h1 = rmsnorm_cast(x_parts, lp["norm1_g"])
    tn = 512
    proj_qkv = matmul([(h1, w_t, 0, 0, True)], 3 * aw, tn=tn)
    feat = matmul([(h1, w_rkv_t, 0, 0, True)], 3 * c, tn=tn)
    lora = matmul([(h1, w_t, 0, (rkv0 + 3 * c) // tn, True)], LORA_COLS, tn=tn)

    qn, kn, vn, k_new, v_new = qk_norm(proj_qkv, lp["q_norm_g"], lp["k_norm_g"], (t, ms))
    lams = [lp[n].reshape(1, HEAD_DIM_A) for n in ("lambda_q1", "lambda_k1", "lambda_q2", "lambda_k2")]
    blk = min(attn_blk, t)
    assert t % blk == 0 and blk >= T5_FAR
    bias_d = bias_tiles(t5_table, blk, blk, rel0=0, masked=True, key_major=True)
    bias_l = bias_tiles(t5_table, blk, blk, rel0=-blk, key_major=True)
    far_bias = t5_table[T5_BUCKETS // 2 - 1] * LOG2E
    nkb = t // blk
    ones_tile = jnp.zeros((nkb, N_HEADS_A, BF16_SUBLANES, blk), BF16).at[:, :, 0, :].set(1.0)
    vt = jnp.transpose(vn[:t].reshape(nkb, blk, N_HEADS_A, 2 * HEAD_DIM_A), (0, 2, 3, 1))
    vt = jnp.concatenate([vt, ones_tile], 2).reshape(nkb, N_HEADS_A * VT_ROWS, blk)
    ya = prompt_attention(qn, kn, vt, jnp.zeros((m, aw), BF16), t, bias_d, bias_l, far_bias, lams, lp["subln_g"], lam_init, blk)
    bias_past = bias_tiles(t5_table, ts, past, rel0=-past)
    bias_new = bias_tiles(t5_table, ts, LANES, rel0=0, n_valid=ts)
    ya = sample_attention(qn, kn, vn, ya, t, cache_k.reshape(nb, past, 2 * N_HEADS_A, HEAD_DIM_A), cache_v,
                          bias_past, bias_new, lams, lp["subln_g"], lam_init)

    shift_rkv = jnp.concatenate([_perm_cols(state_shift[:, 0, i * c:(i + 1) * c]) for i in range(3)], 1)
    lora_pad = jnp.zeros((LORA_COLS - LORA_USED,), F32)
    shift_lora = jnp.concatenate([state_shift[:, 0, 3 * c:], jnp.broadcast_to(lora_pad, (nb, lora_pad.shape[0]))], 1)
    prep_tm = 4 * SCAN_TB
    seqs_per_block = prep_tm // ts

    def block_states(rows):
        rows = rows.reshape(ms // prep_tm, seqs_per_block, rows.shape[1])
        return jnp.concatenate([rows, jnp.zeros((ms // prep_tm, SUBLANES - seqs_per_block, rows.shape[2]), F32)], 1)

    vec = lambda v: v.reshape(1, -1)
    mu = lp["rwkv_mu"]
    prm = dict(
        mu_rkv=vec(jnp.concatenate([_perm_cols(mu[i * c:(i + 1) * c]) for i in range(3)])),
        mu_lora=vec(jnp.concatenate([mu[3 * c:], lora_pad])),
        w0=vec(_perm_cols(lp["rwkv_w0"])), a0=vec(_perm_cols(lp["rwkv_a0"])),
        k_k=vec(_perm_cols(lp["rwkv_k_k"])), k_a=vec(_perm_cols(lp["rwkv_k_a"])),
        r_k=vec(_perm_cols(lp["rwkv_r_k"].reshape(-1))),
        w2=_pad_rows(_perm_cols(lp["rwkv_w2"]), 0, LANES), a2=_pad_rows(_perm_cols(lp["rwkv_a2"]), RANK_W, 2 * LANES),
        g2=_pad_rows(_perm_cols(lp["rwkv_g2"]), RANK_W + RANK_A - LANES, LORA_COLS - LANES))
    ak, wr, wdec, bvec, km, vv, gate, br, kr, bonus = rwkv_prep(
        feat, block_states(shift_rkv), lora, block_states(shift_lora), prm, t, tm=prep_tm)
    n_pstep = t // SCAN_TB
    seq_of_step = jnp.concatenate([jnp.zeros((n_pstep,), jnp.int32), 1 + jnp.arange(nb, dtype=jnp.int32)])
    first = jnp.concatenate([jnp.zeros((n_pstep,), jnp.int32).at[0].set(1), jnp.ones((nb,), jnp.int32)])
    last = jnp.concatenate([jnp.zeros((n_pstep,), jnp.int32).at[-1].set(1), jnp.ones((nb,), jnp.int32)])
    s0 = jnp.concatenate([jnp.zeros((1, V_BLK, K_HI, SUBLANES, LANES), F32),
                          _state_to_tiles(state_wkv.astype(F32))], 0)
    y_scan, s_fin = rwkv_scan((ak, wr, wdec, bvec, km, vv), br, s0, seq_of_step, first, last)
    yb = rwkv_post(y_scan, vv, gate, kr, bonus, vec(_perm_cols(lp["lnx_g"])), vec(_perm_cols(lp["lnx_b"])))
    wkv_fin = _tiles_to_state(s_fin)

    def shift_out(rows):
        return jnp.concatenate([_unperm_cols(feat[rows, i * c:(i + 1) * c]) for i in range(3)]
                               + [lora[rows, :LORA_USED]], -1)

    shift_p = shift_out(slice(t - 1, t))
    shift_s = shift_out(slice(t + ts - 1, m, ts))

    w_out_b = matmul([(perm_t, lp["w_out"], 1, 0)], d, out_dtype=BF16)
    x1 = matmul([(ya, lp["w_out"], 0, 0), (yb, w_out_b, 0, 0)], d, mode="residual", res=x_parts)

    rw = jnp.concatenate([lp["rg_w"], lp["ri_w"], jnp.zeros((d, LANES - N_GROUPS - N_EXPERTS), F32)], 1)
    rb = jnp.concatenate([lp["rg_b"], lp["ri_b"].reshape(-1), jnp.zeros((LANES - N_GROUPS - N_EXPERTS,), F32)])
    h2, route = norm_router(x1, lp["norm2_g"], rw, rb.reshape(1, LANES))
    n_assign = m * TOP_K
    flat_e = route[:, :TOP_K].astype(jnp.int32).reshape(n_assign)
    bm = EXPERT_ROWS
    seg = LANES
    onehot = (flat_e[:, None] == jnp.arange(N_EXPERTS, dtype=jnp.int32)[None, :])
    oh = onehot.astype(BF16).reshape(_exact_div(n_assign, seg), seg, N_EXPERTS)
    tri = (jnp.arange(seg)[:, None] >= jnp.arange(seg)[None, :]).astype(BF16)
    within = jnp.einsum("ij,bje->bie", tri, oh, preferred_element_type=F32)
    seg_tot = within[:, -1, :]
    seg_off = jnp.cumsum(seg_tot, axis=0) - seg_tot
    running = (within + seg_off[:, None, :]).reshape(n_assign, N_EXPERTS)
    counts = (seg_off[-1] + seg_tot[-1]).astype(jnp.int32)
    rank = jnp.sum(jnp.where(onehot, running, 0.0), axis=1).astype(jnp.int32) - 1
    pcounts = (counts + bm - 1) // bm * bm
    eid = jnp.arange(N_EXPERTS, dtype=jnp.int32)
    pend = jnp.sum(jnp.where(eid[None, :] <= eid[:, None], pcounts[None, :], 0), axis=1)
    dest = (pend - pcounts)[flat_e] + rank
    n_blocks = n_assign // bm + N_EXPERTS
    rows_tok = jnp.zeros((n_blocks * bm,), jnp.int32).at[dest].set(jnp.arange(n_assign, dtype=jnp.int32) // TOP_K)
    block_row0 = jnp.arange(n_blocks, dtype=jnp.int32) * bm
    block_e = jnp.minimum(jnp.sum((pend[None, :] <= block_row0[:, None]).astype(jnp.int32), axis=1), N_EXPERTS - 1)
    n_used = (pend[-1] // bm).astype(jnp.int32).reshape(1)
    xg = gather_rows(h2, rows_tok, n_used, bm, BF16)
    runs = _expert_runs(block_e, n_used)
    hmid = expert_up(xg, block_e, n_used, runs, lp["e_wg"], lp["e_wu"])
    yexp = expert_down(hmid, block_e, n_used, runs, lp["e_wd"])
    x2 = moe_combine(yexp, dest.astype(jnp.int32), x1, route)

    h3 = rmsnorm_cast([x2], lp["ple_norm_g"])
    yp, ys = matmul([(h3, lp["ple_gate_w"], 0, 0)], d, mode="ple", res=x2, p=pe, pw=lp["ple_proj_w"], out_rows=(t, ms))

    return (yp, ys.reshape(nb, ts, d), k_new, v_new, wkv_fin, shift_p, shift_s)


def kernel(x_prompt, x_sample, p_prompt, p_sample, cache_k, cache_v, state_wkv, state_shift, t5_table, norm1_g, w_in, q_norm_g, k_norm_g, lambda_q1, lambda_k1, lambda_q2, lambda_k2, subln_g, rwkv_mu, rwkv_w0, rwkv_w2, rwkv_a0, rwkv_a2, rwkv_g2, rwkv_k_k, rwkv_k_a, rwkv_r_k, lnx_g, lnx_b, w_out, norm2_g, router_group_w, router_group_b, router_inner_w, router_inner_b, expert_w_gate, expert_w_up, expert_w_down, ple_norm_g, ple_gate_w, ple_proj_w):
    depth = w_in.shape[0]
    bp, t, d = x_prompt.shape
    nb, ts, _ = x_sample.shape
    assert depth == 1 and bp == 1, "one layer and one prompt stream are fused with the sample batch"
    i = 0
    lp = dict(norm1_g=norm1_g[i], w_in=w_in[i], q_norm_g=q_norm_g[i], k_norm_g=k_norm_g[i],
              lambda_q1=lambda_q1[i], lambda_k1=lambda_k1[i], lambda_q2=lambda_q2[i], lambda_k2=lambda_k2[i],
              subln_g=subln_g[i], rwkv_mu=rwkv_mu[i], rwkv_w0=rwkv_w0[i], rwkv_w2=rwkv_w2[i],
              rwkv_a0=rwkv_a0[i], rwkv_a2=rwkv_a2[i], rwkv_g2=rwkv_g2[i], rwkv_k_k=rwkv_k_k[i],
              rwkv_k_a=rwkv_k_a[i], rwkv_r_k=rwkv_r_k[i], lnx_g=lnx_g[i], lnx_b=lnx_b[i], w_out=w_out[i],
              norm2_g=norm2_g[i], rg_w=router_group_w[i], rg_b=router_group_b[i], ri_w=router_inner_w[i],
              ri_b=router_inner_b[i], e_wg=expert_w_gate[i], e_wu=expert_w_up[i], e_wd=expert_w_down[i],
              ple_norm_g=ple_norm_g[i], ple_gate_w=ple_gate_w[i], ple_proj_w=ple_proj_w[i])
    lam_init = 0.8 - 0.6 * math.exp(-0.3 * i)
    yp, ys, k_new, v_new, wkv_fin, shift_p, shift_s = _trunk_layer(
        x_prompt[0], x_sample, p_prompt[i, 0], p_sample[i], cache_k[i], cache_v[i], state_wkv[i],
        state_shift[i], t5_table, lam_init, lp)
    hk = (N_HEADS_A, 2, HEAD_DIM_A)
    hv = (N_HEADS_A, 2 * HEAD_DIM_A)
    return (yp[None], ys,
            k_new[0].reshape((1, 1, t) + hk), v_new[0].reshape((1, 1, t) + hv),
            wkv_fin[:1][None], shift_p.reshape(1, 1, 1, -1),
            k_new[1].reshape((1, nb, ts) + hk), v_new[1].reshape((1, nb, ts) + hv),
            wkv_fin[1:][None], shift_s.reshape(1, nb, 1, -1))
```

```python
import functools
import math

import jax
import jax.numpy as jnp
from jax import lax
from jax.experimental import pallas as pl
from jax.experimental.pallas import tpu as pltpu

F32 = jnp.float32
BF16 = jnp.bfloat16

LANES = 128
SUBLANES = 8
VMEM_BYTES_V7X = 64 * 1024 * 1024
VMEM_LIMIT = VMEM_BYTES_V7X - 6 * 1024 * 1024

CHUNK = 64
HEAD_DIM_A = 128
N_HEADS_A = 8
ATT_WIDTH = 2 * HEAD_DIM_A * N_HEADS_A
T5_BUCKETS = 32
HEAD_B = 64
N_HEADS_B = 32
RWKV_WIDTH = HEAD_B * N_HEADS_B
RANK_W = 96
RANK_A = 96
RANK_G = 256
LORA_COLS = 512
LNX_EPS = 64e-5
N_GROUPS = 8
EXP_PER_GROUP = 8
N_EXPERTS = N_GROUPS * EXP_PER_GROUP
TOP_K = 2
RMS_EPS = 1e-6
NEG_INF = -1e30
T5_LOG_THRESHOLDS = (12, 16, 23, 32, 46, 64, 91)
T5_FAR = 128

K_LO = LANES // N_HEADS_B
K_HI = HEAD_B // K_LO
V_BLK = HEAD_B // SUBLANES
SCAN_TB = 32

EXPERT_ROWS = 256


def _cparams(sem, vmem=VMEM_LIMIT):
    return pltpu.CompilerParams(dimension_semantics=sem, vmem_limit_bytes=vmem)


def _exact_div(a, b):
    assert a % b == 0, (a, b)
    return a // b


def _sigmoid(x):
    return 1.0 / (1.0 + jnp.exp(-x))


def _part_blocks(parts, tm):
    edges = [0]
    for p in parts:
        edges.append(edges[-1] + _exact_div(p.shape[0], tm))
    return list(zip(edges[:-1], edges[1:]))


def _part_spec(block, lo, hi, row_axis_arg, const_index):
    def index_map(*grid):
        return (jnp.clip(grid[row_axis_arg] - lo, 0, hi - lo - 1),) + const_index(*grid)
    return pl.BlockSpec(block, index_map)


def _rmsnorm_kernel(*refs, ranges):
    x_refs, g_ref, o_ref = refs[:len(ranges)], refs[len(ranges)], refs[len(ranges) + 1]
    i = pl.program_id(0)
    for x_ref, (lo, hi) in zip(x_refs, ranges):
        @pl.when((i >= lo) & (i < hi))
        def _(x_ref=x_ref):
            x = x_ref[...]
            ms = jnp.mean(x * x, axis=-1, keepdims=True)
            o_ref[...] = (x * lax.rsqrt(ms + RMS_EPS) * g_ref[...]).astype(o_ref.dtype)


def rmsnorm_cast(parts, g, tm=256):
    parts = list(parts)
    d = parts[0].shape[1]
    for part in parts:
        tm = math.gcd(tm, part.shape[0])
    ranges = _part_blocks(parts, tm)
    return pl.pallas_call(
        functools.partial(_rmsnorm_kernel, ranges=ranges),
        out_shape=jax.ShapeDtypeStruct((ranges[-1][1] * tm, d), BF16),
        grid=(ranges[-1][1],),
        in_specs=[_part_spec((tm, d), lo, hi, 0, lambda i: (0,)) for lo, hi in ranges]
        + [pl.BlockSpec((1, d), lambda i: (0, 0))],
        out_specs=pl.BlockSpec((tm, d), lambda i: (i, 0)),
        compiler_params=_cparams(("arbitrary",)),
        name="rmsnorm_cast",
    )(*parts, g.reshape(1, d))


def _mm_kernel(*refs, n_pair, cast, transposed, ragged, mode, res_ranges, out_ranges):
    refs = list(refs)
    a_refs, w_refs = refs[:n_pair], refs[n_pair:2 * n_pair]
    del refs[:2 * n_pair]
    res_refs = [refs.pop(0) for _ in res_ranges]
    p_ref, pw_ref = (refs.pop(0), refs.pop(0)) if mode == "ple" else (None, None)
    o_refs = [refs.pop(0) for _ in out_ranges]
    wb_refs = [refs.pop(0) if cast[i] else w_refs[i] for i in range(n_pair)]
    i = pl.program_id(1)

    def product(k):
        dims = _NT if transposed[k] else (((1,), (0,)), ((), ()))
        return lax.dot_general(a_refs[k][...], wb_refs[k][...], dims, preferred_element_type=F32)

    @pl.when(i == 0)
    def _():
        for k in range(n_pair):
            if cast[k]:
                w = w_refs[k][...]
                if ragged[k] is not None:
                    col0, n_valid = ragged[k]
                    row = lax.broadcasted_iota(jnp.int32, w.shape, 0) + (pl.program_id(0) + col0) * w.shape[0]
                    w = jnp.where(row < n_valid, w, 0.0)
                wb_refs[k][...] = w.astype(BF16)

    acc = product(0)
    for k in range(1, n_pair):
        acc = acc + product(k)
    if mode == "ple":
        acc = jnp.dot(p_ref[...], pw_ref[...].astype(BF16), preferred_element_type=F32) * _sigmoid(acc)

    def finish(res_ref):
        val = acc if res_ref is None else res_ref[...] + acc
        for o_ref, (lo, hi) in zip(o_refs, out_ranges):
            if len(o_refs) == 1:
                o_ref[...] = val.astype(o_ref.dtype)
            else:
                @pl.when((i >= lo) & (i < hi))
                def _(o_ref=o_ref):
                    o_ref[...] = val.astype(o_ref.dtype)

    if not res_refs:
        finish(None)
    elif len(res_refs) == 1:
        finish(res_refs[0])
    else:
        for res_ref, (lo, hi) in zip(res_refs, res_ranges):
            @pl.when((i >= lo) & (i < hi))
            def _(res_ref=res_ref):
                finish(res_ref)


def matmul(pairs, n_cols, *, mode="plain", res=None, p=None, pw=None, out_dtype=F32, out_rows=None, tm=1024, tn=512):
    pairs = [tuple(pr) + (False,) * (5 - len(pr)) for pr in pairs]
    m = pairs[0][0].shape[0]
    res_parts = [] if res is None else (list(res) if isinstance(res, (list, tuple)) else [res])
    out_rows = [m] if out_rows is None else list(out_rows)
    for r in [m] + [part.shape[0] for part in res_parts] + out_rows:
        tm = math.gcd(tm, r)
    assert n_cols % tn == 0
    in_specs = [pl.BlockSpec((tm, a.shape[1]), lambda n, i: (i, 0)) for a, _, _, _, _ in pairs]
    w_blocks = [(tn, a.shape[1]) if tr else (a.shape[1], tn) for a, _, _, _, tr in pairs]
    in_specs += [pl.BlockSpec(blk, (lambda n, i, rb=rb, cb=cb: (n + cb, rb)) if tr else
                              (lambda n, i, rb=rb, cb=cb: (rb, n + cb)))
                 for blk, (_, _, rb, cb, tr) in zip(w_blocks, pairs)]
    args = [pr[0] for pr in pairs] + [pr[1] for pr in pairs]
    cast = tuple(pr[1].dtype != BF16 for pr in pairs)
    transposed = tuple(pr[4] for pr in pairs)
    ragged = tuple((cb, w.shape[0]) if tr and cast_k and (cb * tn + n_cols > w.shape[0]) else None
                   for (_, w, _, cb, tr), cast_k in zip(pairs, cast))
    res_ranges = _part_blocks(res_parts, tm)
    in_specs += [_part_spec((tm, tn), lo, hi, 1, lambda n, i: (n,)) for lo, hi in res_ranges]
    args += res_parts
    if mode == "ple":
        kp = p.shape[1]
        in_specs += [pl.BlockSpec((tm, kp), lambda n, i: (i, 0)), pl.BlockSpec((kp, tn), lambda n, i: (0, n))]
        args += [p, pw]
    out_ranges = _part_blocks([jax.ShapeDtypeStruct((r, n_cols), out_dtype) for r in out_rows], tm)
    assert out_ranges[-1][1] * tm == m
    outs = pl.pallas_call(
        functools.partial(_mm_kernel, n_pair=len(pairs), cast=cast, transposed=transposed, ragged=ragged, mode=mode,
                          res_ranges=res_ranges, out_ranges=out_ranges),
        out_shape=tuple(jax.ShapeDtypeStruct((r, n_cols), out_dtype) for r in out_rows),
        grid=(n_cols // tn, m // tm),
        in_specs=in_specs,
        out_specs=tuple(_part_spec((tm, tn), lo, hi, 1, lambda n, i: (n,)) for lo, hi in out_ranges),
        scratch_shapes=[pltpu.VMEM(blk, BF16) for blk, c in zip(w_blocks, cast) if c],
        compiler_params=_cparams(("arbitrary", "arbitrary")),
        name="matmul_" + mode,
    )(*args)
    return outs[0] if len(outs) == 1 else outs


def _perm_rows_kernel(pt_ref, w_ref, o_ref):
    o_ref[...] = jnp.dot(pt_ref[...], w_ref[...].astype(BF16), preferred_element_type=F32).astype(BF16)


def permute_weight_rows(w, perm_t, row_block0, n_sections, tn=512):
    c = RWKV_WIDTH
    n = w.shape[1]
    return pl.pallas_call(
        _perm_rows_kernel,
        out_shape=jax.ShapeDtypeStruct((n_sections * c, n), BF16),
        grid=(n_sections, _exact_div(n, tn)),
        in_specs=[pl.BlockSpec((c, c), lambda sec, j: (0, 0)),
                  pl.BlockSpec((c, tn), lambda sec, j: (row_block0 + sec, j))],
        out_specs=pl.BlockSpec((c, tn), lambda sec, j: (sec, j)),
        compiler_params=_cparams(("parallel", "parallel")),
        name="permute_weight_rows",
    )(perm_t, w)


def _qk_norm_kernel(*refs, ranges):
    q_ref, k_ref, v_ref, qg_ref, kg_ref, qo_ref, kbo_ref, vo_ref = refs[:8]
    n = len(ranges)
    k_outs, v_outs, kbuf = refs[8:8 + n], refs[8 + n:8 + 2 * n], refs[8 + 2 * n]

    def head_norm(x, g):
        ms = jnp.mean(x * x, axis=-1, keepdims=True)
        return x * lax.rsqrt(ms + RMS_EPS) * g

    qg = qg_ref[...]
    kg = kg_ref[...]
    for c in range(ATT_WIDTH // HEAD_DIM_A):
        sl = slice(c * HEAD_DIM_A, (c + 1) * HEAD_DIM_A)
        qn = head_norm(q_ref[:, sl], qg)
        qo_ref[:, sl] = (qn * (LOG2E * HEAD_DIM_A ** -0.5)).astype(BF16)
        kn = head_norm(k_ref[:, sl], kg)
        kbuf[:, sl] = kn
        kbo_ref[:, sl] = kn.astype(BF16)
    vo_ref[...] = v_ref[...].astype(BF16)
    i = pl.program_id(0)
    for k_out, v_out, (lo, hi) in zip(k_outs, v_outs, ranges):
        @pl.when((i >= lo) & (i < hi))
        def _(k_out=k_out, v_out=v_out):
            _rows_to_token_tiles(kbuf, k_out)
            _rows_to_token_tiles(v_ref, v_out, lane_tiles=2)


def qk_norm(proj_qkv, q_g, k_g, part_rows, tm=256):
    m = proj_qkv.shape[0]
    w = ATT_WIDTH
    hw = 2 * HEAD_DIM_A
    for r in part_rows:
        tm = math.gcd(tm, r)
    ranges = _part_blocks([jax.ShapeDtypeStruct((r, w), F32) for r in part_rows], tm)
    blk = lambda c: pl.BlockSpec((tm, w), lambda i, c=c: (i, c))
    vec = pl.BlockSpec((1, HEAD_DIM_A), lambda i: (0, 0))
    out_blk = pl.BlockSpec((tm, w), lambda i: (i, 0))
    rows = jax.ShapeDtypeStruct((m, w), BF16)
    outs = pl.pallas_call(
        functools.partial(_qk_norm_kernel, ranges=ranges),
        out_shape=(rows, rows, rows)
        + tuple(jax.ShapeDtypeStruct((r, w // HEAD_DIM_A, HEAD_DIM_A), F32) for r in part_rows)
        + tuple(jax.ShapeDtypeStruct((r, w // hw, hw), F32) for r in part_rows),
        grid=(_exact_div(m, tm),),
        in_specs=[blk(0), blk(1), blk(2), vec, vec],
        out_specs=(out_blk, out_blk, out_blk)
        + tuple(_part_spec((tm, w // HEAD_DIM_A, HEAD_DIM_A), lo, hi, 0, lambda i: (0, 0)) for lo, hi in ranges)
        + tuple(_part_spec((tm, w // hw, hw), lo, hi, 0, lambda i: (0, 0)) for lo, hi in ranges),
        scratch_shapes=[pltpu.VMEM((tm, w), F32)],
        compiler_params=_cparams(("arbitrary",)),
        name="qk_norm",
    )(proj_qkv, proj_qkv, proj_qkv, q_g.reshape(1, -1), k_g.reshape(1, -1))
    n = len(part_rows)
    return outs[0], outs[1], outs[2], outs[3:3 + n], outs[3 + n:]


def _bias_kernel(tab_ref, o_ref, *, rel0, masked, key_major, n_valid):
    _, nr, nc = o_ref.shape
    r = lax.broadcasted_iota(jnp.int32, (nr, nc), 0)
    c = lax.broadcasted_iota(jnp.int32, (nr, nc), 1)
    kpos, qpos = (r, c) if key_major else (c, r)
    rel = rel0 + kpos - qpos
    n = jnp.abs(rel)
    large = jnp.full((nr, nc), T5_BUCKETS // 4, jnp.int32)
    for thr in T5_LOG_THRESHOLDS:
        large = large + jnp.where(n >= thr, 1, 0)
    bucket = jnp.where(n < T5_BUCKETS // 4, n, large) + jnp.where(rel > 0, T5_BUCKETS // 2, 0)
    if masked:
        visible = (kpos // CHUNK) <= (qpos // CHUNK)
    for h in range(N_HEADS_A):
        acc = jnp.zeros((nr, nc), F32)
        for b in range(T5_BUCKETS):
            acc = jnp.where(bucket == b, tab_ref[b, h] * LOG2E, acc)
        if masked:
            acc = jnp.where(visible, acc, NEG_INF)
        if n_valid is not None:
            acc = jnp.where(c < n_valid, acc, NEG_INF)
        o_ref[h] = acc


def bias_tiles(table, nr, nc, *, rel0, masked=False, key_major=False, n_valid=None):
    return pl.pallas_call(
        functools.partial(_bias_kernel, rel0=rel0, masked=masked, key_major=key_major, n_valid=n_valid),
        out_shape=jax.ShapeDtypeStruct((N_HEADS_A, nr, nc), F32),
        in_specs=[pl.BlockSpec(memory_space=pltpu.SMEM)],
        out_specs=pl.BlockSpec(memory_space=pltpu.VMEM),
        compiler_params=_cparams(None),
        name="t5_bias",
    )(table)


def _lambda_value(lq1, lk1, lq2, lk2, lam_init):
    s1 = jnp.sum(lq1 * lk1, axis=-1, keepdims=True)
    s2 = jnp.sum(lq2 * lk2, axis=-1, keepdims=True)
    return jnp.exp(s1) - jnp.exp(s2) + lam_init


def _online_step(s, bias, v, m_prev, l_prev, acc_prev):
    nchunk = max(s.shape[1] // LANES, 1)
    width = s.shape[1] // nchunk
    sc = [s[:, c * width:(c + 1) * width] + bias(c * width, width) for c in range(nchunk)]
    m_new = jnp.maximum(m_prev, jnp.max(functools.reduce(jnp.maximum, sc), axis=-1, keepdims=True))
    alpha = jnp.exp2(m_prev - m_new)
    p = [jnp.exp2(x - m_new) for x in sc]
    l_new = alpha * l_prev + jnp.sum(functools.reduce(lambda a, b: a + b, p), axis=-1, keepdims=True)
    pb = p[0].astype(BF16) if nchunk == 1 else jnp.concatenate([x.astype(BF16) for x in p], axis=1)
    return m_new, l_new, alpha * acc_prev + jnp.dot(pb, v, preferred_element_type=F32)


def _diff_finish(acc1, l1, acc2, l2, lam, g, lam_init):
    o = acc1 / l1 - lam * (acc2 / l2)
    ms = jnp.mean(o * o, axis=-1, keepdims=True)
    return (o * lax.rsqrt(ms + RMS_EPS) * g) * (1.0 - lam_init)


_NT = (((1,), (1,)), ((), ()))
BF16_SUBLANES = 16
VT_ROWS = 2 * HEAD_DIM_A + BF16_SUBLANES
LOG2E = math.log2(math.e)


def _prompt_attn_kernel(far_ref, q_ref, k_ref, vt_ref, bd_ref, bl_ref, lq1, lk1, lq2, lk2, g_ref, ya_hbm,
                        o_ref, m_ref, acc_ref, *, blk, nsub, lam_init):
    h = pl.program_id(0)
    i = pl.program_id(1)
    m_ref[...] = jnp.full(m_ref.shape, NEG_INF, F32)
    acc_ref[...] = jnp.zeros(acc_ref.shape, F32)
    d = HEAD_DIM_A
    hw = 2 * d
    far_bias = far_ref[h]

    def chain(st, bias_tile, vt, idx):
        shift = far_bias if bias_tile is None else None
        if bias_tile is not None:
            st = st + bias_tile[0]
        col_max = jnp.max(st, axis=0, keepdims=True)
        if shift is not None:
            col_max = col_max + shift
        m_prev = m_ref[idx]
        m_new = jnp.maximum(m_prev, col_max)
        alpha = jnp.exp2(m_prev - m_new)
        p = jnp.exp2(st - (m_new if shift is None else m_new - shift))
        acc_ref[idx] = alpha * acc_ref[idx] + jnp.dot(vt, p.astype(BF16), preferred_element_type=F32)
        m_ref[idx] = m_new

    def update(j, ahead):
        kb = k_ref[pl.ds(pl.multiple_of(j * blk, blk), blk), :]
        vt = vt_ref[j]
        todo = []
        for sa in range(nsub):
            if ahead[sa] < 0:
                continue
            bias_tile = None if ahead[sa] >= 2 else (bl_ref if ahead[sa] == 1 else bd_ref)
            for mp in range(2):
                st = lax.dot_general(kb[:, mp * d:(mp + 1) * d], q_ref[sa * blk:(sa + 1) * blk, mp * d:(mp + 1) * d],
                                     _NT, preferred_element_type=F32)
                todo.append((st, bias_tile, 2 * sa + mp))
        for st, bias_tile, idx in todo:
            chain(st, bias_tile, vt, idx)

    def far_body(j, carry):
        update(j, [2] * nsub)
        return carry

    lax.fori_loop(0, jnp.maximum(nsub * i - 1, 0), far_body, 0)

    @pl.when(i >= 1)
    def _():
        update(nsub * i - 1, [sa + 1 for sa in range(nsub)])

    for o in range(nsub):
        update(nsub * i + o, [sa - o for sa in range(nsub)])
    lam = _lambda_value(lq1[...], lk1[...], lq2[...], lk2[...], lam_init)
    for sa in range(nsub):
        a1 = acc_ref[2 * sa]
        a2 = acc_ref[2 * sa + 1]
        ot = a1[:hw] / a1[hw:hw + 1] - lam * (a2[:hw] / a2[hw:hw + 1])
        ms = jnp.mean(ot * ot, axis=0, keepdims=True)
        yt = ot * lax.rsqrt(ms + RMS_EPS)
        o_ref[sa * blk:(sa + 1) * blk, :] = ((yt.T * g_ref[...]) * (1.0 - lam_init)).astype(BF16)


def prompt_attention(qn, kn, vt, ya, t, bias_d, bias_l, far_bias, lams, subln_g, lam_init, blk, nsub=2):
    nsub = min(nsub, t // blk)
    bq = nsub * blk
    hw = 2 * HEAD_DIM_A
    vec = pl.BlockSpec((1, HEAD_DIM_A), lambda h, i, far: (0, 0))
    return pl.pallas_call(
        functools.partial(_prompt_attn_kernel, blk=blk, nsub=nsub, lam_init=lam_init),
        out_shape=jax.ShapeDtypeStruct(ya.shape, BF16),
        input_output_aliases={11: 0},
        grid_spec=pltpu.PrefetchScalarGridSpec(
            num_scalar_prefetch=1,
            grid=(N_HEADS_A, _exact_div(t, bq)),
            in_specs=[pl.BlockSpec((bq, hw), lambda h, i, far: (i, h)),
                      pl.BlockSpec((t, hw), lambda h, i, far: (0, h)),
                      pl.BlockSpec((t // blk, VT_ROWS, blk), lambda h, i, far: (0, h, 0)),
                      pl.BlockSpec((1, blk, blk), lambda h, i, far: (h, 0, 0)),
                      pl.BlockSpec((1, blk, blk), lambda h, i, far: (h, 0, 0)),
                      vec, vec, vec, vec,
                      pl.BlockSpec((1, hw), lambda h, i, far: (0, 0)),
                      pl.BlockSpec(memory_space=pl.ANY)],
            out_specs=pl.BlockSpec((bq, hw), lambda h, i, far: (i, h)),
            scratch_shapes=[pltpu.VMEM((2 * nsub, 1, blk), F32), pltpu.VMEM((2 * nsub, VT_ROWS, blk), F32)]),
        compiler_params=_cparams(("arbitrary", "arbitrary")),
        name="prompt_attention",
    )(far_bias, qn, kn, vt, bias_d, bias_l, *lams, subln_g.reshape(1, hw), ya)


def _sublane_transpose8(tiles, sub):
    a = list(tiles)
    for dist in (4, 2, 1):
        keep = (sub % (2 * dist)) < dist
        nxt = list(a)
        for i in range(SUBLANES):
            if i % (2 * dist) < dist:
                x, y = a[i], a[i + dist]
                nxt[i] = jnp.where(keep, x, pltpu.roll(y, dist, axis=0))
                nxt[i + dist] = jnp.where(keep, pltpu.roll(x, SUBLANES - dist, axis=0), y)
        a = nxt
    return a


def _sample_attn_kernel(q_ref, ck_ref, cv_ref, kn_ref, vn_ref, bp_ref, bn_ref, lq1, lk1, lq2, lk2, g_ref,
                        ya_hbm, o_ref, m_ref, l_ref, acc_ref, ks_ref, vs_ref, *, lam_init):
    t = pl.program_id(1)
    last = t == pl.num_programs(1) - 1
    d = HEAD_DIM_A
    tk = ck_ref.shape[1]
    ts = q_ref.shape[0]
    n = 2 * N_HEADS_A
    sub = lax.broadcasted_iota(jnp.int32, (SUBLANES, LANES), 0)

    @pl.when(t == 0)
    def _():
        m_ref[...] = jnp.full(m_ref.shape, NEG_INF, F32)
        l_ref[...] = jnp.zeros(l_ref.shape, F32)
        acc_ref[...] = jnp.zeros(acc_ref.shape, F32)
        ks_ref[:, pl.ds(tk, LANES), :] = jnp.zeros((n, LANES, d), BF16)
        vs_ref[:, pl.ds(tk, LANES), :] = jnp.zeros((N_HEADS_A, LANES, 2 * d), BF16)
        for c in range(n):
            ks_ref[c, pl.ds(tk, ts), :] = kn_ref[:, c * d:(c + 1) * d]
        for h in range(N_HEADS_A):
            vs_ref[h, pl.ds(tk, ts), :] = vn_ref[:, 2 * h * d:2 * (h + 1) * d]

    def to_head_major(src, dst, p0, lanes):
        halves = [_sublane_transpose8(src[SUBLANES * a:SUBLANES * (a + 1)], sub) for a in range(2)]
        for r in range(SUBLANES):
            dst(r)[pl.ds(p0, BF16_SUBLANES), lanes] = jnp.concatenate([halves[0][r], halves[1][r]], 0).astype(BF16)

    def relayout(g, carry):
        p0 = pl.multiple_of(g * BF16_SUBLANES, BF16_SUBLANES)
        kt = ck_ref[0, pl.ds(p0, BF16_SUBLANES), :, :]
        vt = cv_ref[0, pl.ds(p0, BF16_SUBLANES), :, :]
        for a in range(2):
            rows = slice(a * SUBLANES, (a + 1) * SUBLANES)
            to_head_major([kt[p, rows, :] for p in range(BF16_SUBLANES)], lambda r, a=a: ks_ref.at[a * SUBLANES + r],
                          p0, slice(None))
            lanes = slice(a * LANES, (a + 1) * LANES)
            to_head_major([vt[p, :, lanes] for p in range(BF16_SUBLANES)], lambda r: vs_ref.at[r], p0, lanes)
        return carry

    lax.fori_loop(0, tk // BF16_SUBLANES, relayout, 0)

    state = [(m_ref[c], l_ref[c], acc_ref[c]) for c in range(n)]
    logits = [lax.dot_general(q_ref[:, c * d:(c + 1) * d], ks_ref[c], _NT, preferred_element_type=F32)
              for c in range(n)]
    for c in range(n):
        h = c // 2

        def bias(k0, w, h=h):
            if k0 < tk:
                return bp_ref[h, :, k0:k0 + w]
            return jnp.where(last, bn_ref[h], NEG_INF)

        state[c] = _online_step(logits[c], bias, vs_ref[h], *state[c])
    for c in range(n):
        m_ref[c], l_ref[c], acc_ref[c] = state[c]

    @pl.when(last)
    def _():
        lam = _lambda_value(lq1[...], lk1[...], lq2[...], lk2[...], lam_init)
        for h in range(N_HEADS_A):
            y = _diff_finish(acc_ref[2 * h], l_ref[2 * h], acc_ref[2 * h + 1], l_ref[2 * h + 1], lam,
                             g_ref[...], lam_init)
            o_ref[:, 2 * h * d:2 * (h + 1) * d] = y.astype(BF16)


def sample_attention(qn, kn, vn, ya, row0, cache_k, cache_v, bias_past, bias_new, lams, subln_g, lam_init, tk=1024):
    nb, past = cache_k.shape[:2]
    w = ATT_WIDTH
    ts = bias_new.shape[1]
    tk = min(tk, past)
    blk0 = row0 // ts
    hw = 2 * HEAD_DIM_A
    vec = pl.BlockSpec((1, HEAD_DIM_A), lambda b, t: (0, 0))
    new_rows = pl.BlockSpec((ts, w), lambda b, t: (blk0 + b, 0))
    return pl.pallas_call(
        functools.partial(_sample_attn_kernel, lam_init=lam_init),
        out_shape=jax.ShapeDtypeStruct(ya.shape, BF16),
        grid=(nb, _exact_div(past, tk)),
        in_specs=[new_rows,
                  pl.BlockSpec((1, tk, 2 * N_HEADS_A, HEAD_DIM_A), lambda b, t: (b, t, 0, 0)),
                  pl.BlockSpec((1, tk, N_HEADS_A, hw), lambda b, t: (b, t, 0, 0)),
                  new_rows, new_rows,
                  pl.BlockSpec((N_HEADS_A, ts, tk), lambda b, t: (0, 0, t)),
                  pl.BlockSpec((N_HEADS_A, ts, LANES), lambda b, t: (0, 0, 0)),
                  vec, vec, vec, vec,
                  pl.BlockSpec((1, hw), lambda b, t: (0, 0)),
                  pl.BlockSpec(memory_space=pl.ANY)],
        out_specs=pl.BlockSpec((ts, w), lambda b, t: (blk0 + b, 0)),
        input_output_aliases={12: 0},
        scratch_shapes=[pltpu.VMEM((2 * N_HEADS_A, ts, 1), F32), pltpu.VMEM((2 * N_HEADS_A, ts, 1), F32),
                        pltpu.VMEM((2 * N_HEADS_A, ts, hw), F32),
                        pltpu.VMEM((2 * N_HEADS_A, tk + LANES, HEAD_DIM_A), BF16),
                        pltpu.VMEM((N_HEADS_A, tk + LANES, hw), BF16)],
        compiler_params=_cparams(("arbitrary", "arbitrary")),
        name="sample_attention",
    )(qn, cache_k, cache_v, kn, vn, bias_past, bias_new, *lams, subln_g.reshape(1, hw), ya)


def _group_allreduce(x):
    r1 = pltpu.roll(x, N_HEADS_B, axis=1)
    r2 = pltpu.roll(x, 2 * N_HEADS_B, axis=1)
    r3 = pltpu.roll(x, 3 * N_HEADS_B, axis=1)
    return (x + r1) + (r2 + r3)


def _head_sum128(x):
    acc = x[:, 0:LANES]
    for c in range(1, K_HI):
        acc = acc + x[:, c * LANES:(c + 1) * LANES]
    return _group_allreduce(acc)


def _tile16(x128):
    return jnp.concatenate([x128] * K_HI, axis=1)


LORA_USED = RANK_W + RANK_A + RANK_G


def _token_shift(x_ref, pv_ref, st_ref, buf_ref, mu_ref, is_first, is_sample):
    tm = x_ref.shape[0]
    x = x_ref[...]
    buf_ref[pl.ds(0, SUBLANES), :] = jnp.where(is_first, 0.0, pv_ref[...])
    buf_ref[pl.ds(SUBLANES, tm), :] = x
    shifted = buf_ref[pl.ds(SUBLANES - 1, tm), :]
    starts = is_sample & (lax.broadcasted_iota(jnp.int32, (SCAN_TB, 1), 0) == 0)
    prev = jnp.concatenate([jnp.where(starts, st_ref[0, g:g + 1, :], shifted[g * SCAN_TB:(g + 1) * SCAN_TB])
                            for g in range(tm // SCAN_TB)], 0)
    return x + (prev - x) * mu_ref[...]


def _rwkv_prep_kernel(f_ref, pf_ref, sf_ref, lo_ref, plo_ref, slo_ref, mu_ref, mul_ref, w0_ref, a0_ref, kk_ref,
                      ka_ref, rk_ref, w2_ref, a2_ref, g2_ref,
                      ak_o, wr_o, w_o, b_o, km_o, v_o, g_o, br_o, kr_o, bo_o, fbuf, lbuf, *, n_prompt_blocks):
    c = RWKV_WIDTH
    i = pl.program_id(0)
    xm = _token_shift(f_ref, pf_ref, sf_ref, fbuf, mu_ref, i == 0, i >= n_prompt_blocks)
    xl = _token_shift(lo_ref, plo_ref, slo_ref, lbuf, mul_ref, i == 0, i >= n_prompt_blocks)
    r, k, v = xm[:, :c], xm[:, c:2 * c], xm[:, 2 * c:]
    wd, ad, gd = xl[:, :LANES], xl[:, :2 * LANES], xl[:, LANES:]
    lw = w0_ref[...] + jnp.dot(jnp.tanh(wd).astype(BF16), w2_ref[...].astype(BF16), preferred_element_type=F32)
    z = -lw
    softplus = jnp.maximum(z, 0.0) + jnp.log(1.0 + jnp.exp(-jnp.abs(z)))
    log_decay = -jnp.exp(-softplus - 0.5)
    tm = lw.shape[0]
    ri = lax.broadcasted_iota(jnp.int32, (tm, tm), 0)
    ci = lax.broadcasted_iota(jnp.int32, (tm, tm), 1)
    same_run_upto = ((ri // SCAN_TB) == (ci // SCAN_TB)) & (ci <= ri)
    csum = jnp.dot(jnp.where(same_run_upto, 1.0, 0.0), log_decay, precision=lax.Precision.HIGHEST,
                   preferred_element_type=F32)
    p_incl = jnp.exp(csum)
    p_excl = jnp.exp(csum - log_decay)
    inv_incl = jnp.exp(-csum)
    a = _sigmoid(a0_ref[...] + jnp.dot(ad.astype(BF16), a2_ref[...].astype(BF16), preferred_element_type=F32))
    g_o[...] = jnp.dot(_sigmoid(gd).astype(BF16), g2_ref[...].astype(BF16), preferred_element_type=F32)
    kk = k * kk_ref[...]
    norm = jnp.maximum(jnp.sqrt(_tile16(_head_sum128(kk * kk))), 1e-12)
    kk = kk / norm
    kmod = k * (1.0 + (a - 1.0) * ka_ref[...])
    bvec = kk * a
    ak_o[...] = -kk * p_excl
    wr_o[...] = p_incl * r
    w_o[...] = p_incl
    b_o[...] = bvec * inv_incl
    km_o[...] = kmod * inv_incl
    v_o[...] = v
    br_o[...] = _head_sum128(bvec * r)
    kr_o[...] = _head_sum128(kmod * r)
    bo_o[...] = _head_sum128(r * kmod * rk_ref[...])


def rwkv_prep(feat, state, lora, state_lora, prm, n_prompt_rows, tm=128):
    m = feat.shape[0]
    c = RWKV_WIDTH
    row = lambda w: pl.BlockSpec((tm, w), lambda i: (i, 0))
    vec = lambda w: pl.BlockSpec((1, w), lambda i: (0, 0))
    mat = lambda r: pl.BlockSpec((r, c), lambda i: (0, 0))
    big = jax.ShapeDtypeStruct((m, c), F32)
    small = jax.ShapeDtypeStruct((m, LANES), F32)
    assert tm // SCAN_TB <= SUBLANES
    npb = _exact_div(n_prompt_rows, tm)
    prev8 = lambda w: pl.BlockSpec((SUBLANES, w), lambda i: (jnp.maximum(i * (tm // SUBLANES) - 1, 0), 0))
    st = lambda w: pl.BlockSpec((1, SUBLANES, w), lambda i: (jnp.maximum(i - npb, 0), 0, 0))
    return pl.pallas_call(
        functools.partial(_rwkv_prep_kernel, n_prompt_blocks=npb),
        out_shape=(big,) * 7 + (small,) * 3,
        grid=(_exact_div(m, tm),),
        in_specs=[row(3 * c), prev8(3 * c), st(3 * c), row(LORA_COLS), prev8(LORA_COLS), st(LORA_COLS),
                  vec(3 * c), vec(LORA_COLS), vec(c), vec(c), vec(c), vec(c), vec(c),
                  mat(LANES), mat(2 * LANES), mat(LORA_COLS - LANES)],
        out_specs=(row(c),) * 7 + (row(LANES),) * 3,
        scratch_shapes=[pltpu.VMEM((tm + SUBLANES, 3 * c), F32), pltpu.VMEM((tm + SUBLANES, LORA_COLS), F32)],
        compiler_params=_cparams(("arbitrary",)),
        name="rwkv_prep",
    )(feat, feat, state, lora, lora, state_lora, prm["mu_rkv"], prm["mu_lora"], prm["w0"], prm["a0"], prm["k_k"], prm["k_a"],
      prm["r_k"], prm["w2"], prm["a2"], prm["g2"])


def _scan_kernel(seq_ref, first_ref, last_ref, ak_ref, wr_ref, w_ref, b_ref, km_ref, v_ref, br_ref, s0_ref,
                 y_ref, sout_ref, s_ref):
    step = pl.program_id(0)

    @pl.when(first_ref[step] == 1)
    def _():
        s_ref[...] = s0_ref[0]

    sub = lax.broadcasted_iota(jnp.int32, (SUBLANES, LANES), 0)
    grp = lax.broadcasted_iota(jnp.int32, (SUBLANES, LANES), 1) // N_HEADS_B
    own_group = grp == (sub % K_LO)
    low_half = sub < K_LO

    def row(ref, t8, s, c):
        tile = ref[pl.ds(t8, SUBLANES), c * LANES:(c + 1) * LANES]
        return jnp.broadcast_to(tile[s:s + 1], (SUBLANES, LANES))

    def time_step(t8, s, y_lo, y_hi):
        vt = []
        for vb in range(V_BLK):
            tile = jnp.where(low_half, row(v_ref, t8, s, 2 * vb), row(v_ref, t8, s, 2 * vb + 1))
            vt.append(_group_allreduce(jnp.where(own_group, tile, 0.0)))
        acc_u = [[None, None] for _ in range(V_BLK)]
        acc_y = [[None, None] for _ in range(V_BLK)]
        for kh in range(K_HI):
            a_row = row(ak_ref, t8, s, kh)
            wr_row = row(wr_ref, t8, s, kh)
            for vb in range(V_BLK):
                st = s_ref[vb, kh]
                pu, py = st * a_row, st * wr_row
                acc_u[vb][kh % 2] = pu if acc_u[vb][kh % 2] is None else acc_u[vb][kh % 2] + pu
                acc_y[vb][kh % 2] = py if acc_y[vb][kh % 2] is None else acc_y[vb][kh % 2] + py
        acc_u = [a + b for a, b in acc_u]
        acc_y = [a + b for a, b in acc_y]
        br_row = row(br_ref, t8, s, 0)
        u = [_group_allreduce(x) for x in acc_u]
        for vb in range(V_BLK):
            y = _group_allreduce(acc_y[vb]) + u[vb] * br_row
            ym = jnp.where(own_group, y, 0.0)
            ym = ym + pltpu.roll(ym, 1, axis=0)
            ym = ym + pltpu.roll(ym, 2, axis=0)
            y_lo[vb] = jnp.where(sub == s, pltpu.roll(ym, (s - (K_LO - 1)) % SUBLANES, axis=0), y_lo[vb])
            y_hi[vb] = jnp.where(sub == s, pltpu.roll(ym, (s - (2 * K_LO - 1)) % SUBLANES, axis=0), y_hi[vb])
        for kh in range(K_HI):
            b_row = row(b_ref, t8, s, kh)
            km_row = row(km_ref, t8, s, kh)
            for vb in range(V_BLK):
                s_ref[vb, kh] = s_ref[vb, kh] + (b_row * u[vb] + km_row * vt[vb])

    def eight_steps(gi, carry):
        t8 = pl.multiple_of(gi * SUBLANES, SUBLANES)
        y_lo = [jnp.zeros((SUBLANES, LANES), F32) for _ in range(V_BLK)]
        y_hi = [jnp.zeros((SUBLANES, LANES), F32) for _ in range(V_BLK)]
        for s in range(SUBLANES):
            time_step(t8, s, y_lo, y_hi)
        for vb in range(V_BLK):
            y_ref[pl.ds(t8, SUBLANES), (2 * vb) * LANES:(2 * vb + 1) * LANES] = y_lo[vb]
            y_ref[pl.ds(t8, SUBLANES), (2 * vb + 1) * LANES:(2 * vb + 2) * LANES] = y_hi[vb]
        return carry

    lax.fori_loop(0, y_ref.shape[0] // SUBLANES, eight_steps, 0)
    last8 = y_ref.shape[0] - SUBLANES
    for kh in range(K_HI):
        p_end = row(w_ref, last8, SUBLANES - 1, kh)
        for vb in range(V_BLK):
            s_ref[vb, kh] = s_ref[vb, kh] * p_end

    @pl.when(last_ref[step] == 1)
    def _():
        sout_ref[0] = s_ref[...]


def rwkv_scan(ops, br, s0, seq_of_step, first, last):
    m, c = ops[0].shape
    nseq = s0.shape[0]
    tb = SCAN_TB
    row = lambda w: pl.BlockSpec((tb, w), lambda i, sq, fi, la: (i, 0))
    st = pl.BlockSpec((1, V_BLK, K_HI, SUBLANES, LANES), lambda i, sq, fi, la: (sq[i], 0, 0, 0, 0))
    return pl.pallas_call(
        _scan_kernel,
        out_shape=(jax.ShapeDtypeStruct((m, c), F32), jax.ShapeDtypeStruct(s0.shape, F32)),
        grid_spec=pltpu.PrefetchScalarGridSpec(
            num_scalar_prefetch=3,
            grid=(m // tb,),
            in_specs=[row(c)] * 6 + [row(LANES), st],
            out_specs=(row(c), st),
            scratch_shapes=[pltpu.VMEM((V_BLK, K_HI, SUBLANES, LANES), F32)]),
        compiler_params=_cparams(("arbitrary",)),
        name="rwkv_scan",
    )(seq_of_step, first, last, *ops, br, s0)


def _rwkv_post_kernel(y_ref, v_ref, g_ref, kr_ref, bo_ref, lg_ref, lb_ref, o_ref):
    v = v_ref[...]
    y = y_ref[...] + v * _tile16(kr_ref[...])
    mean = _tile16(_head_sum128(y)) * (1.0 / HEAD_B)
    yc = y - mean
    var = _tile16(_head_sum128(yc * yc)) * (1.0 / HEAD_B)
    yn = yc * lax.rsqrt(var + LNX_EPS) * lg_ref[...] + lb_ref[...]
    o_ref[...] = ((yn + _tile16(bo_ref[...]) * v) * g_ref[...]).astype(BF16)


def rwkv_post(y, v, g, kr, bo, lnx_g, lnx_b, tm=256):
    m, c = y.shape
    row = lambda w: pl.BlockSpec((tm, w), lambda i: (i, 0))
    vec = pl.BlockSpec((1, c), lambda i: (0, 0))
    return pl.pallas_call(
        _rwkv_post_kernel,
        out_shape=jax.ShapeDtypeStruct((m, c), BF16),
        grid=(_exact_div(m, tm),),
        in_specs=[row(c), row(c), row(c), row(LANES), row(LANES), vec, vec],
        out_specs=row(c),
        compiler_params=_cparams(("parallel",)),
        name="rwkv_post",
    )(y, v, g, kr, bo, lnx_g, lnx_b)


def _rows_to_token_tiles(src_ref, dst_ref, lane_tiles=1):
    rows, cols = src_ref.shape
    sub = lax.broadcasted_iota(jnp.int32, (SUBLANES, LANES), 0)

    def body(g, carry):
        r0 = pl.multiple_of(g * SUBLANES, SUBLANES)
        for a in range(cols // (SUBLANES * LANES * lane_tiles)):
            for lt in range(lane_tiles):
                chunk = lambda i: (SUBLANES * a + i) * lane_tiles + lt
                tiles = [src_ref[pl.ds(r0, SUBLANES), chunk(i) * LANES:(chunk(i) + 1) * LANES]
                         for i in range(SUBLANES)]
                out = _sublane_transpose8(tiles, sub)
                for p in range(SUBLANES):
                    dst_ref[r0 + p, SUBLANES * a:SUBLANES * (a + 1), lt * LANES:(lt + 1) * LANES] = out[p]
        return carry

    lax.fori_loop(0, rows // SUBLANES, body, 0)


def _token_tiles_to_rows(src, n_rows, chunks, emit):
    sub = lax.broadcasted_iota(jnp.int32, (SUBLANES, LANES), 0)

    def body(g, carry):
        r0 = pl.multiple_of(g * BF16_SUBLANES, BF16_SUBLANES)
        for a in range(chunks // SUBLANES):
            halves = [_sublane_transpose8([src(r0 + SUBLANES * b + p)[SUBLANES * a:SUBLANES * (a + 1), :]
                                           for p in range(SUBLANES)], sub) for b in range(2)]
            for i in range(SUBLANES):
                emit(r0, SUBLANES * a + i, jnp.concatenate([halves[0][i], halves[1][i]], 0))
        return carry

    lax.fori_loop(0, n_rows // BF16_SUBLANES, body, 0)


def _router_kernel(x_ref, g_ref, rw_ref, rb_ref, h_ref, route_ref, hbuf):
    x = x_ref[...]
    ms = jnp.mean(x * x, axis=-1, keepdims=True)
    h = x * lax.rsqrt(ms + RMS_EPS) * g_ref[...]
    hbuf[...] = h
    _rows_to_token_tiles(hbuf, h_ref)
    logits = jnp.dot(h, rw_ref[...], precision=lax.Precision.HIGHEST, preferred_element_type=F32) + rb_ref[...]
    lane = lax.broadcasted_iota(jnp.int32, logits.shape, 1)
    big = jnp.int32(LANES)

    def first_argmax(vals, valid):
        masked = jnp.where(valid, vals, -jnp.inf)
        mx = jnp.max(masked, axis=-1, keepdims=True)
        idx = jnp.min(jnp.where(valid & (masked == mx), lane, big), axis=-1, keepdims=True)
        return mx, idx

    is_group = lane < N_GROUPS
    g_max, g_idx = first_argmax(logits, is_group)
    g_top = 1.0 / jnp.sum(jnp.where(is_group, jnp.exp(logits - g_max), 0.0), axis=-1, keepdims=True)
    in_group = (lane >= N_GROUPS) & (lane < N_GROUPS + N_EXPERTS) & ((lane - N_GROUPS) // EXP_PER_GROUP == g_idx)
    i_max, idx1 = first_argmax(logits, in_group)
    z = jnp.sum(jnp.where(in_group, jnp.exp(logits - i_max), 0.0), axis=-1, keepdims=True)
    i_max2, idx2 = first_argmax(logits, in_group & (lane != idx1))
    p1 = 1.0 / z
    p2 = jnp.exp(i_max2 - i_max) / z
    psum = p1 + p2
    gate1 = g_top * p1 / psum
    gate2 = g_top * p2 / psum
    route = jnp.where(lane == 0, (idx1 - N_GROUPS).astype(F32),
                      jnp.where(lane == 1, (idx2 - N_GROUPS).astype(F32),
                                jnp.where(lane == 2, gate1, jnp.where(lane == 3, gate2, 0.0))))
    route_ref[...] = route


def norm_router(x, g, rw, rb, tm=256):
    m, d = x.shape
    return pl.pallas_call(
        _router_kernel,
        out_shape=(jax.ShapeDtypeStruct((m, d // LANES, LANES), F32), jax.ShapeDtypeStruct((m, LANES), F32)),
        grid=(_exact_div(m, tm),),
        in_specs=[pl.BlockSpec((tm, d), lambda i: (i, 0)), pl.BlockSpec((1, d), lambda i: (0, 0)),
                  pl.BlockSpec((d, LANES), lambda i: (0, 0)), pl.BlockSpec((1, LANES), lambda i: (0, 0))],
        out_specs=(pl.BlockSpec((tm, d // LANES, LANES), lambda i: (i, 0, 0)),
                   pl.BlockSpec((tm, LANES), lambda i: (i, 0))),
        scratch_shapes=[pltpu.VMEM((tm, d), F32)],
        compiler_params=_cparams(("arbitrary",)),
        name="norm_router",
    )(x, g.reshape(1, d), rw, rb)


def _row_copy(src_hbm, buf, sem, slot, src_row, dst_row):
    return pltpu.make_async_copy(src_hbm.at[src_row], buf.at[slot, dst_row], sem.at[slot])


def _wait_slot(src_hbm, buf, sem, slot):
    pltpu.make_async_copy(src_hbm.at[pl.ds(0, buf.shape[1])], buf.at[slot], sem.at[slot]).wait()


ROW_ISSUE_UNROLL = 8
GATHER_SLOTS = 3


def _gather_rows_kernel(idx_ref, nused_ref, src_hbm, o_ref, buf, sem, *, rows):
    b = pl.program_id(0)
    nb = pl.num_programs(0)
    used = nused_ref[0]

    def issue(blk, slot):
        def body(r, c):
            _row_copy(src_hbm, buf, sem, slot, idx_ref[blk * rows + r], r).start()
            return c
        lax.fori_loop(0, rows, body, 0, unroll=ROW_ISSUE_UNROLL)

    ahead = GATHER_SLOTS - 1
    for first in range(ahead):
        @pl.when((b == 0) & (first < used) & (first < nb))
        def _(first=first):
            issue(first, first)

    @pl.when((b + ahead < nb) & (b + ahead < used))
    def _():
        issue(b + ahead, (b + ahead) % GATHER_SLOTS)

    @pl.when(b < used)
    def _():
        slot = b % GATHER_SLOTS
        _wait_slot(src_hbm, buf, sem, slot)

        def emit(r0, j, x):
            o_ref[pl.ds(r0, BF16_SUBLANES), j * LANES:(j + 1) * LANES] = x.astype(o_ref.dtype)

        _token_tiles_to_rows(lambda r: buf[slot, r], rows, buf.shape[2], emit)

    @pl.when(b >= used)
    def _():
        o_ref[...] = jnp.zeros(o_ref.shape, o_ref.dtype)


def gather_rows(src, idx, n_used, rows, out_dtype):
    n = idx.shape[0]
    chunks = src.shape[1]
    d = chunks * LANES
    return pl.pallas_call(
        functools.partial(_gather_rows_kernel, rows=rows),
        out_shape=jax.ShapeDtypeStruct((n, d), out_dtype),
        grid_spec=pltpu.PrefetchScalarGridSpec(
            num_scalar_prefetch=2,
            grid=(n // rows,),
            in_specs=[pl.BlockSpec(memory_space=pl.ANY)],
            out_specs=pl.BlockSpec((rows, d), lambda b, idx, nu: (b, 0)),
            scratch_shapes=[pltpu.VMEM((GATHER_SLOTS, rows, chunks, LANES), src.dtype),
                            pltpu.SemaphoreType.DMA((GATHER_SLOTS,))]),
        compiler_params=_cparams(("arbitrary",)),
        name="gather_rows",
    )(idx, n_used, src)


def _combine_kernel(idx_ref, src_hbm, x_ref, route_ref, o_ref, buf, sem, ybuf, *, rows):
    b = pl.program_id(0)
    nb = pl.num_programs(0)

    def issue(blk, slot):
        def body(r, c):
            base = (blk * rows + r) * TOP_K
            _row_copy(src_hbm, buf, sem, slot, idx_ref[base], r).start()
            _row_copy(src_hbm, buf, sem, slot, idx_ref[base + 1], rows + r).start()
            return c
        lax.fori_loop(0, rows, body, 0, unroll=ROW_ISSUE_UNROLL)

    @pl.when(b == 0)
    def _():
        issue(0, 0)

    @pl.when(b + 1 < nb)
    def _():
        issue(b + 1, (b + 1) % 2)

    slot = b % 2
    _wait_slot(src_hbm, buf, sem, slot)

    def emit(r0, j, x):
        ybuf[pl.ds(r0, BF16_SUBLANES), j * LANES:(j + 1) * LANES] = x

    _token_tiles_to_rows(lambda r: buf[slot, r], TOP_K * rows, buf.shape[2], emit)
    route = route_ref[...]
    o_ref[...] = x_ref[...] + (route[:, 2:3] * ybuf[pl.ds(0, rows), :] + route[:, 3:4] * ybuf[pl.ds(rows, rows), :])


def moe_combine(yb, dest, x, route, rows=128):
    m, d = x.shape
    chunks = yb.shape[1]
    return pl.pallas_call(
        functools.partial(_combine_kernel, rows=rows),
        out_shape=jax.ShapeDtypeStruct((m, d), F32),
        grid_spec=pltpu.PrefetchScalarGridSpec(
            num_scalar_prefetch=1,
            grid=(_exact_div(m, rows),),
            in_specs=[pl.BlockSpec(memory_space=pl.ANY),
                      pl.BlockSpec((rows, d), lambda b, idx: (b, 0)),
                      pl.BlockSpec((rows, LANES), lambda b, idx: (b, 0))],
            out_specs=pl.BlockSpec((rows, d), lambda b, idx: (b, 0)),
            scratch_shapes=[pltpu.VMEM((2, TOP_K * rows, chunks, LANES), F32), pltpu.SemaphoreType.DMA((2,)),
                            pltpu.VMEM((TOP_K * rows, d), F32)]),
        compiler_params=_cparams(("arbitrary",)),
        name="moe_combine",
    )(dest, yb, x, route)


def _expert_runs(block_e, n_used):
    nb = block_e.shape[0]
    idx = jnp.arange(nb, dtype=jnp.int32)
    valid = idx < n_used[0]
    first = valid & ((idx == 0) | (block_e != jnp.roll(block_e, 1)))
    upto = idx[None, :] <= idx[:, None]
    slot = (jnp.sum(jnp.where(upto & first[None, :], 1, 0), axis=1) - 1) % 2
    nxt = jnp.min(jnp.where(first[None, :] & ~upto, idx[None, :], nb), axis=1)
    next_e = jnp.where(nxt < nb, block_e[jnp.minimum(nxt, nb - 1)], -1)
    return first.astype(jnp.int32), slot.astype(jnp.int32), next_e.astype(jnp.int32)


def _stream_expert_weights(b, be_ref, first_ref, slot_ref, next_ref, copies):
    slot = slot_ref[b]

    @pl.when(b == 0)
    def _():
        for cp in copies(be_ref[0], 0):
            cp.start()

    for cp in copies(be_ref[b], slot):
        cp.wait()

    @pl.when(next_ref[b] >= 0)
    def _():
        for cp in copies(next_ref[b], 1 - slot):
            cp.start()

    return slot


WEIGHT_DMA_SPLIT = 4
EXPERT_K_CHUNK = 512


def _expert_up_kernel(be_ref, nused_ref, first_ref, slot_ref, next_ref, x_ref, wg_hbm, wu_hbm, o_ref,
                      wbuf, wgb_ref, wub_ref, sem, *, tf):
    f = pl.program_id(0)
    b = pl.program_id(1)

    def copies(e, slot):
        cols = pl.ds(pl.multiple_of(f * tf, tf), tf)
        rows_per = wbuf.shape[2] // WEIGHT_DMA_SPLIT
        return [pltpu.make_async_copy(w.at[e, pl.ds(q * rows_per, rows_per), cols],
                                      wbuf.at[slot, k, pl.ds(q * rows_per, rows_per)], sem.at[slot, k])
                for k, w in enumerate((wg_hbm, wu_hbm)) for q in range(WEIGHT_DMA_SPLIT)]

    def finish(g, u):
        o_ref[...] = (g * _sigmoid(g) * u).astype(BF16)

    live = b < nused_ref[0]

    @pl.when(live & (first_ref[b] == 1))
    def _():
        slot = _stream_expert_weights(b, be_ref, first_ref, slot_ref, next_ref, copies)
        d = x_ref.shape[1]
        g = jnp.zeros(o_ref.shape, F32)
        u = jnp.zeros(o_ref.shape, F32)
        for k0 in range(0, d, EXPERT_K_CHUNK):
            rows = pl.ds(k0, EXPERT_K_CHUNK)
            wgb_ref[rows, :] = wbuf[slot, 0, rows, :].astype(BF16)
            wub_ref[rows, :] = wbuf[slot, 1, rows, :].astype(BF16)
            xk = x_ref[:, k0:k0 + EXPERT_K_CHUNK]
            g = g + jnp.dot(xk, wgb_ref[rows, :], preferred_element_type=F32)
            u = u + jnp.dot(xk, wub_ref[rows, :], preferred_element_type=F32)
        finish(g, u)

    @pl.when(live & (first_ref[b] == 0))
    def _():
        x = x_ref[...]
        finish(jnp.dot(x, wgb_ref[...], preferred_element_type=F32),
               jnp.dot(x, wub_ref[...], preferred_element_type=F32))

    @pl.when(b >= nused_ref[0])
    def _():
        o_ref[...] = jnp.zeros(o_ref.shape, o_ref.dtype)


def expert_up(xg, block_e, n_used, runs, wg, wu, tf=512):
    n, d = xg.shape
    de = wg.shape[2]
    bm = EXPERT_ROWS
    live = lambda b, nu: jnp.minimum(b, jnp.maximum(nu[0] - 1, 0))
    return pl.pallas_call(
        functools.partial(_expert_up_kernel, tf=tf),
        out_shape=jax.ShapeDtypeStruct((n, de), BF16),
        grid_spec=pltpu.PrefetchScalarGridSpec(
            num_scalar_prefetch=5,
            grid=(de // tf, n // bm),
            in_specs=[pl.BlockSpec((bm, d), lambda f, b, be, nu, fi, sl, ne: (live(b, nu), 0)),
                      pl.BlockSpec(memory_space=pl.ANY), pl.BlockSpec(memory_space=pl.ANY)],
            out_specs=pl.BlockSpec((bm, tf), lambda f, b, be, nu, fi, sl, ne: (b, f)),
            scratch_shapes=[pltpu.VMEM((2, 2, d, tf), F32), pltpu.VMEM((d, tf), BF16), pltpu.VMEM((d, tf), BF16),
                            pltpu.SemaphoreType.DMA((2, 2))]),
        compiler_params=_cparams(("arbitrary", "arbitrary")),
        name="expert_up",
    )(block_e, n_used, *runs, xg, wg, wu)


def _expert_down_kernel(be_ref, nused_ref, first_ref, slot_ref, next_ref, h_ref, wd_hbm, o_ref, wbuf, wdb_ref, sem,
                        ybuf, *, tn):
    c = pl.program_id(0)
    b = pl.program_id(1)

    def copies(e, slot):
        cols = pl.ds(pl.multiple_of(c * tn, tn), tn)
        rows_per = wbuf.shape[1] // WEIGHT_DMA_SPLIT
        return [pltpu.make_async_copy(wd_hbm.at[e, pl.ds(q * rows_per, rows_per), cols],
                                      wbuf.at[slot, pl.ds(q * rows_per, rows_per)], sem.at[slot])
                for q in range(WEIGHT_DMA_SPLIT)]

    live = b < nused_ref[0]

    @pl.when(live & (first_ref[b] == 1))
    def _():
        slot = _stream_expert_weights(b, be_ref, first_ref, slot_ref, next_ref, copies)
        de = h_ref.shape[1]
        y = jnp.zeros(ybuf.shape, F32)
        for k0 in range(0, de, EXPERT_K_CHUNK):
            rows = pl.ds(k0, EXPERT_K_CHUNK)
            wdb_ref[rows, :] = wbuf[slot, rows, :].astype(BF16)
            y = y + jnp.dot(h_ref[:, k0:k0 + EXPERT_K_CHUNK], wdb_ref[rows, :], preferred_element_type=F32)
        ybuf[...] = y

    @pl.when(live & (first_ref[b] == 0))
    def _():
        ybuf[...] = jnp.dot(h_ref[...], wdb_ref[...], preferred_element_type=F32)

    @pl.when(live)
    def _():
        _rows_to_token_tiles(ybuf, o_ref)

    @pl.when(b >= nused_ref[0])
    def _():
        o_ref[...] = jnp.zeros(o_ref.shape, o_ref.dtype)


def expert_down(hmid, block_e, n_used, runs, wd, tn=2048):
    n, de = hmid.shape
    d = wd.shape[2]
    bm = EXPERT_ROWS
    live = lambda b, nu: jnp.minimum(b, jnp.maximum(nu[0] - 1, 0))
    return pl.pallas_call(
        functools.partial(_expert_down_kernel, tn=tn),
        out_shape=jax.ShapeDtypeStruct((n, d // LANES, LANES), F32),
        grid_spec=pltpu.PrefetchScalarGridSpec(
            num_scalar_prefetch=5,
            grid=(d // tn, n // bm),
            in_specs=[pl.BlockSpec((bm, de), lambda c, b, be, nu, fi, sl, ne: (live(b, nu), 0)),
                      pl.BlockSpec(memory_space=pl.ANY)],
            out_specs=pl.BlockSpec((bm, tn // LANES, LANES), lambda c, b, be, nu, fi, sl, ne: (b, c, 0)),
            scratch_shapes=[pltpu.VMEM((2, de, tn), F32), pltpu.VMEM((de, tn), BF16), pltpu.SemaphoreType.DMA((2,)),
                            pltpu.VMEM((bm, tn), F32)]),
        compiler_params=_cparams(("arbitrary", "arbitrary")),
        name="expert_down",
    )(block_e, n_used, *runs, hmid, wd)


def _perm_cols(x):
    pre = x.shape[:-1]
    return jnp.moveaxis(x.reshape(pre + (N_HEADS_B, K_HI, K_LO)), -3, -1).reshape(pre + (RWKV_WIDTH,))


def _unperm_cols(x):
    pre = x.shape[:-1]
    return jnp.moveaxis(x.reshape(pre + (K_HI, K_LO, N_HEADS_B)), -1, -3).reshape(pre + (RWKV_WIDTH,))


def _state_to_tiles(s):
    n = s.shape[0]
    s = s.reshape(n, N_HEADS_B, V_BLK, SUBLANES, K_HI, K_LO)
    return s.transpose(0, 2, 4, 3, 5, 1).reshape(n, V_BLK, K_HI, SUBLANES, LANES)


def _tiles_to_state(s):
    n = s.shape[0]
    s = s.reshape(n, V_BLK, K_HI, SUBLANES, K_LO, N_HEADS_B)
    return s.transpose(0, 5, 1, 3, 2, 4).reshape(n, N_HEADS_B, HEAD_B, HEAD_B)


def _pad_rows(w, row0, rows):
    return jnp.zeros((rows,) + w.shape[1:], w.dtype).at[row0:row0 + w.shape[0]].set(w)


def _trunk_layer(xp, xs, pp, ps, cache_k, cache_v, state_wkv, state_shift, t5_table, lam_init, lp, attn_blk=512):
    t, d = xp.shape
    nb, ts, _ = xs.shape
    past = cache_k.shape[1]
    ms = nb * ts
    m = t + ms
    aw = ATT_WIDTH
    c = RWKV_WIDTH
    assert ts == SCAN_TB and t % SCAN_TB == 0 and past % CHUNK == 0 and ts <= CHUNK

    x_parts = [xp, xs.reshape(ms, d)]
    pe = jnp.concatenate([pp, ps.reshape(ms, -1)], 0).astype(BF16)

    w_t = lp["w_in"].T
    rkv0 = 3 * aw
    src = jnp.arange(c, dtype=jnp.int32)
    dst = (src % HEAD_B // K_LO) * LANES + (src % K_LO) * N_HEADS_B + src // HEAD_B
    perm_t = (src[:, None] == dst[None, :]).astype(BF16)
    assert aw == c
    w_rkv_t = permute_weight_rows(w_t, perm_t, rkv0 // c, 3)
    h1 = rmsnorm_cast(x_parts, lp["norm1_g"])
    tn = 512
    proj_qkv = matmul([(h1, w_t, 0, 0, True)], 3 * aw, tn=tn)
    feat = matmul([(h1, w_rkv_t, 0, 0, True)], 3 * c, tn=tn)
    lora = matmul([(h1, w_t, 0, (rkv0 + 3 * c) // tn, True)], LORA_COLS, tn=tn)

    qn, kn, vn, k_new, v_new = qk_norm(proj_qkv, lp["q_norm_g"], lp["k_norm_g"], (t, ms))
    lams = [lp[n].reshape(1, HEAD_DIM_A) for n in ("lambda_q1", "lambda_k1", "lambda_q2", "lambda_k2")]
    blk = min(attn_blk, t)
    assert t % blk == 0 and blk >= T5_FAR
    bias_d = bias_tiles(t5_table, blk, blk, rel0=0, masked=True, key_major=True)
    bias_l = bias_tiles(t5_table, blk, blk, rel0=-blk, key_major=True)
    far_bias = t5_table[T5_BUCKETS // 2 - 1] * LOG2E
    nkb = t // blk
    ones_tile = jnp.zeros((nkb, N_HEADS_A, BF16_SUBLANES, blk), BF16).at[:, :, 0, :].set(1.0)
    vt = jnp.transpose(vn[:t].reshape(nkb, blk, N_HEADS_A, 2 * HEAD_DIM_A), (0, 2, 3, 1))
    vt = jnp.concatenate([vt, ones_tile], 2).reshape(nkb, N_HEADS_A * VT_ROWS, blk)
    ya = prompt_attention(qn, kn, vt, jnp.zeros((m, aw), BF16), t, bias_d, bias_l, far_bias, lams, lp["subln_g"], lam_init, blk)
    bias_past = bias_tiles(t5_table, ts, past, rel0=-past)
    bias_new = bias_tiles(t5_table, ts, LANES, rel0=0, n_valid=ts)
    ya = sample_attention(qn, kn, vn, ya, t, cache_k.reshape(nb, past, 2 * N_HEADS_A, HEAD_DIM_A), cache_v,
                          bias_past, bias_new, lams, lp["subln_g"], lam_init)

    shift_rkv = jnp.concatenate([_perm_cols(state_shift[:, 0, i * c:(i + 1) * c]) for i in range(3)], 1)
    lora_pad = jnp.zeros((LORA_COLS - LORA_USED,), F32)
    shift_lora = jnp.concatenate([state_shift[:, 0, 3 * c:], jnp.broadcast_to(lora_pad, (nb, lora_pad.shape[0]))], 1)
    prep_tm = 4 * SCAN_TB
    seqs_per_block = prep_tm // ts

    def block_states(rows):
        rows = rows.reshape(ms // prep_tm, seqs_per_block, rows.shape[1])
        return jnp.concatenate([rows, jnp.zeros((ms // prep_tm, SUBLANES - seqs_per_block, rows.shape[2]), F32)], 1)

    vec = lambda v: v.reshape(1, -1)
    mu = lp["rwkv_mu"]
    prm = dict(
        mu_rkv=vec(jnp.concatenate([_perm_cols(mu[i * c:(i + 1) * c]) for i in range(3)])),
        mu_lora=vec(jnp.concatenate([mu[3 * c:], lora_pad])),
        w0=vec(_perm_cols(lp["rwkv_w0"])), a0=vec(_perm_cols(lp["rwkv_a0"])),
        k_k=vec(_perm_cols(lp["rwkv_k_k"])), k_a=vec(_perm_cols(lp["rwkv_k_a"])),
        r_k=vec(_perm_cols(lp["rwkv_r_k"].reshape(-1))),
        w2=_pad_rows(_perm_cols(lp["rwkv_w2"]), 0, LANES), a2=_pad_rows(_perm_cols(lp["rwkv_a2"]), RANK_W, 2 * LANES),
        g2=_pad_rows(_perm_cols(lp["rwkv_g2"]), RANK_W + RANK_A - LANES, LORA_COLS - LANES))
    ak, wr, wdec, bvec, km, vv, gate, br, kr, bonus = rwkv_prep(
        feat, block_states(shift_rkv), lora, block_states(shift_lora), prm, t, tm=prep_tm)
    n_pstep = t // SCAN_TB
    seq_of_step = jnp.concatenate([jnp.zeros((n_pstep,), jnp.int32), 1 + jnp.arange(nb, dtype=jnp.int32)])
    first = jnp.concatenate([jnp.zeros((n_pstep,), jnp.int32).at[0].set(1), jnp.ones((nb,), jnp.int32)])
    last = jnp.concatenate([jnp.zeros((n_pstep,), jnp.int32).at[-1].set(1), jnp.ones((nb,), jnp.int32)])
    s0 = jnp.concatenate([jnp.zeros((1, V_BLK, K_HI, SUBLANES, LANES), F32),
                          _state_to_tiles(state_wkv.astype(F32))], 0)
    y_scan, s_fin = rwkv_scan((ak, wr, wdec, bvec, km, vv), br, s0, seq_of_step, first, last)
    yb = rwkv_post(y_scan, vv, gate, kr, bonus, vec(_perm_cols(lp["lnx_g"])), vec(_perm_cols(lp["lnx_b"])))
    wkv_fin = _tiles_to_state(s_fin)

    def shift_out(rows):
        return jnp.concatenate([_unperm_cols(feat[rows, i * c:(i + 1) * c]) for i in range(3)]
                               + [lora[rows, :LORA_USED]], -1)

    shift_p = shift_out(slice(t - 1, t))
    shift_s = shift_out(slice(t + ts - 1, m, ts))

    w_out_b = matmul([(perm_t, lp["w_out"], 1, 0)], d, out_dtype=BF16)
    x1 = matmul([(ya, lp["w_out"], 0, 0), (yb, w_out_b, 0, 0)], d, mode="residual", res=x_parts)

    rw = jnp.concatenate([lp["rg_w"], lp["ri_w"], jnp.zeros((d, LANES - N_GROUPS - N_EXPERTS), F32)], 1)
    rb = jnp.concatenate([lp["rg_b"], lp["ri_b"].reshape(-1), jnp.zeros((LANES - N_GROUPS - N_EXPERTS,), F32)])
    h2, route = norm_router(x1, lp["norm2_g"], rw, rb.reshape(1, LANES))
    n_assign = m * TOP_K
    flat_e = route[:, :TOP_K].astype(jnp.int32).reshape(n_assign)
    bm = EXPERT_ROWS
    seg = LANES
    onehot = (flat_e[:, None] == jnp.arange(N_EXPERTS, dtype=jnp.int32)[None, :])
    oh = onehot.astype(BF16).reshape(_exact_div(n_assign, seg), seg, N_EXPERTS)
    tri = (jnp.arange(seg)[:, None] >= jnp.arange(seg)[None, :]).astype(BF16)
    within = jnp.einsum("ij,bje->bie", tri, oh, preferred_element_type=F32)
    seg_tot = within[:, -1, :]
    seg_off = jnp.cumsum(seg_tot, axis=0) - seg_tot
    running = (within + seg_off[:, None, :]).reshape(n_assign, N_EXPERTS)
    counts = (seg_off[-1] + seg_tot[-1]).astype(jnp.int32)
    rank = jnp.sum(jnp.where(onehot, running, 0.0), axis=1).astype(jnp.int32) - 1
    pcounts = (counts + bm - 1) // bm * bm
    eid = jnp.arange(N_EXPERTS, dtype=jnp.int32)
    pend = jnp.sum(jnp.where(eid[None, :] <= eid[:, None], pcounts[None, :], 0), axis=1)
    dest = (pend - pcounts)[flat_e] + rank
    n_blocks = n_assign // bm + N_EXPERTS
    rows_tok = jnp.zeros((n_blocks * bm,), jnp.int32).at[dest].set(jnp.arange(n_assign, dtype=jnp.int32) // TOP_K)
    block_row0 = jnp.arange(n_blocks, dtype=jnp.int32) * bm
    block_e = jnp.minimum(jnp.sum((pend[None, :] <= block_row0[:, None]).astype(jnp.int32), axis=1), N_EXPERTS - 1)
    n_used = (pend[-1] // bm).astype(jnp.int32).reshape(1)
    xg = gather_rows(h2, rows_tok, n_used, bm, BF16)
    runs = _expert_runs(block_e, n_used)
    hmid = expert_up(xg, block_e, n_used, runs, lp["e_wg"], lp["e_wu"])
    yexp = expert_down(hmid, block_e, n_used, runs, lp["e_wd"])
    x2 = moe_combine(yexp, dest.astype(jnp.int32), x1, route)

    h3 = rmsnorm_cast([x2], lp["ple_norm_g"])
    yp, ys = matmul([(h3, lp["ple_gate_w"], 0, 0)], d, mode="ple", res=x2, p=pe, pw=lp["ple_proj_w"], out_rows=(t, ms))

    return (yp, ys.reshape(nb, ts, d), k_new, v_new, wkv_fin, shift_p, shift_s)


def kernel(x_prompt, x_sample, p_prompt, p_sample, cache_k, cache_v, state_wkv, state_shift, t5_table, norm1_g, w_in, q_norm_g, k_norm_g, lambda_q1, lambda_k1, lambda_q2, lambda_k2, subln_g, rwkv_mu, rwkv_w0, rwkv_w2, rwkv_a0, rwkv_a2, rwkv_g2, rwkv_k_k, rwkv_k_a, rwkv_r_k, lnx_g, lnx_b, w_out, norm2_g, router_group_w, router_group_b, router_inner_w, router_inner_b, expert_w_gate, expert_w_up, expert_w_down, ple_norm_g, ple_gate_w, ple_proj_w):
    depth = w_in.shape[0]
    bp, t, d = x_prompt.shape
    nb, ts, _ = x_sample.shape
    assert depth == 1 and bp == 1, "one layer and one prompt stream are fused with the sample batch"
    i = 0
    lp = dict(norm1_g=norm1_g[i], w_in=w_in[i], q_norm_g=q_norm_g[i], k_norm_g=k_norm_g[i],
              lambda_q1=lambda_q1[i], lambda_k1=lambda_k1[i], lambda_q2=lambda_q2[i], lambda_k2=lambda_k2[i],
              subln_g=subln_g[i], rwkv_mu=rwkv_mu[i], rwkv_w0=rwkv_w0[i], rwkv_w2=rwkv_w2[i],
              rwkv_a0=rwkv_a0[i], rwkv_a2=rwkv_a2[i], rwkv_g2=rwkv_g2[i], rwkv_k_k=rwkv_k_k[i],
              rwkv_k_a=rwkv_k_a[i], rwkv_r_k=rwkv_r_k[i], lnx_g=lnx_g[i], lnx_b=lnx_b[i], w_out=w_out[i],
              norm2_g=norm2_g[i], rg_w=router_group_w[i], rg_b=router_group_b[i], ri_w=router_inner_w[i],
              ri_b=router_inner_b[i], e_wg=expert_w_gate[i], e_wu=expert_w_up[i], e_wd=expert_w_down[i],
              ple_norm_g=ple_norm_g[i], ple_gate_w=ple_gate_w[i], ple_proj_w=ple_proj_w[i])
    lam_init = 0.8 - 0.6 * math.exp(-0.3 * i)
    yp, ys, k_new, v_new, wkv_fin, shift_p, shift_s = _trunk_layer(
        x_prompt[0], x_sample, p_prompt[i, 0], p_sample[i], cache_k[i], cache_v[i], state_wkv[i],
        state_shift[i], t5_table, lam_init, lp)
    hk = (N_HEADS_A, 2, HEAD_DIM_A)
    hv = (N_HEADS_A, 2 * HEAD_DIM_A)
    return (yp[None], ys,
            k_new[0].reshape((1, 1, t) + hk), v_new[0].reshape((1, 1, t) + hv),
            wkv_fin[:1][None], shift_p.reshape(1, 1, 1, -1),
            k_new[1].reshape((1, nb, ts) + hk), v_new[1].reshape((1, nb, ts) + hv),
            wkv_fin[1:][None], shift_s.reshape(1, nb, 1, -1))
```

```python
import functools
import math

import jax
import jax.numpy as jnp
from jax import lax
from jax.experimental import pallas as pl
from jax.experimental.pallas import tpu as pltpu

F32 = jnp.float32
BF16 = jnp.bfloat16

LANES = 128
SUBLANES = 8
VMEM_BYTES_V7X = 64 * 1024 * 1024
VMEM_LIMIT = VMEM_BYTES_V7X - 6 * 1024 * 1024

CHUNK = 64
HEAD_DIM_A = 128
N_HEADS_A = 8
ATT_WIDTH = 2 * HEAD_DIM_A * N_HEADS_A
T5_BUCKETS = 32
HEAD_B = 64
N_HEADS_B = 32
RWKV_WIDTH = HEAD_B * N_HEADS_B
RANK_W = 96
RANK_A = 96
RANK_G = 256
LORA_COLS = 512
LNX_EPS = 64e-5
N_GROUPS = 8
EXP_PER_GROUP = 8
N_EXPERTS = N_GROUPS * EXP_PER_GROUP
TOP_K = 2
RMS_EPS = 1e-6
NEG_INF = -1e30
T5_LOG_THRESHOLDS = (12, 16, 23, 32, 46, 64, 91)
T5_FAR = 128

K_LO = LANES // N_HEADS_B
K_HI = HEAD_B // K_LO
V_BLK = HEAD_B // SUBLANES
SCAN_TB = 32

EXPERT_ROWS = 256


def _cparams(sem, vmem=VMEM_LIMIT):
    return pltpu.CompilerParams(dimension_semantics=sem, vmem_limit_bytes=vmem)


def _exact_div(a, b):
    assert a % b == 0, (a, b)
    return a // b


def _sigmoid(x):
    return 1.0 / (1.0 + jnp.exp(-x))


def _part_blocks(parts, tm):
    edges = [0]
    for p in parts:
        edges.append(edges[-1] + _exact_div(p.shape[0], tm))
    return list(zip(edges[:-1], edges[1:]))


def _part_spec(block, lo, hi, row_axis_arg, const_index):
    def index_map(*grid):
        return (jnp.clip(grid[row_axis_arg] - lo, 0, hi - lo - 1),) + const_index(*grid)
    return pl.BlockSpec(block, index_map)


def _rmsnorm_kernel(*refs, ranges):
    x_refs, g_ref, o_ref = refs[:len(ranges)], refs[len(ranges)], refs[len(ranges) + 1]
    i = pl.program_id(0)
    for x_ref, (lo, hi) in zip(x_refs, ranges):
        @pl.when((i >= lo) & (i < hi))
        def _(x_ref=x_ref):
            x = x_ref[...]
            ms = jnp.mean(x * x, axis=-1, keepdims=True)
            o_ref[...] = (x * lax.rsqrt(ms + RMS_EPS) * g_ref[...]).astype(o_ref.dtype)


def rmsnorm_cast(parts, g, tm=256):
    parts = list(parts)
    d = parts[0].shape[1]
    for part in parts:
        tm = math.gcd(tm, part.shape[0])
    ranges = _part_blocks(parts, tm)
    return pl.pallas_call(
        functools.partial(_rmsnorm_kernel, ranges=ranges),
        out_shape=jax.ShapeDtypeStruct((ranges[-1][1] * tm, d), BF16),
        grid=(ranges[-1][1],),
        in_specs=[_part_spec((tm, d), lo, hi, 0, lambda i: (0,)) for lo, hi in ranges]
        + [pl.BlockSpec((1, d), lambda i: (0, 0))],
        out_specs=pl.BlockSpec((tm, d), lambda i: (i, 0)),
        compiler_params=_cparams(("arbitrary",)),
        name="rmsnorm_cast",
    )(*parts, g.reshape(1, d))


def _mm_kernel(*refs, n_pair, cast, transposed, ragged, mode, res_ranges, out_ranges):
    refs = list(refs)
    a_refs, w_refs = refs[:n_pair], refs[n_pair:2 * n_pair]
    del refs[:2 * n_pair]
    res_refs = [refs.pop(0) for _ in res_ranges]
    p_ref, pw_ref = (refs.pop(0), refs.pop(0)) if mode == "ple" else (None, None)
    o_refs = [refs.pop(0) for _ in out_ranges]
    wb_refs = [refs.pop(0) if cast[i] else w_refs[i] for i in range(n_pair)]
    i = pl.program_id(1)

    def product(k):
        dims = _NT if transposed[k] else (((1,), (0,)), ((), ()))
        return lax.dot_general(a_refs[k][...], wb_refs[k][...], dims, preferred_element_type=F32)

    @pl.when(i == 0)
    def _():
        for k in range(n_pair):
            if cast[k]:
                w = w_refs[k][...]
                if ragged[k] is not None:
                    col0, n_valid = ragged[k]
                    row = lax.broadcasted_iota(jnp.int32, w.shape, 0) + (pl.program_id(0) + col0) * w.shape[0]
                    w = jnp.where(row < n_valid, w, 0.0)
                wb_refs[k][...] = w.astype(BF16)

    acc = product(0)
    for k in range(1, n_pair):
        acc = acc + product(k)
    if mode == "ple":
        acc = jnp.dot(p_ref[...], pw_ref[...].astype(BF16), preferred_element_type=F32) * _sigmoid(acc)

    def finish(res_ref):
        val = acc if res_ref is None else res_ref[...] + acc
        for o_ref, (lo, hi) in zip(o_refs, out_ranges):
            if len(o_refs) == 1:
                o_ref[...] = val.astype(o_ref.dtype)
            else:
                @pl.when((i >= lo) & (i < hi))
                def _(o_ref=o_ref):
                    o_ref[...] = val.astype(o_ref.dtype)

    if not res_refs:
        finish(None)
    elif len(res_refs) == 1:
        finish(res_refs[0])
    else:
        for res_ref, (lo, hi) in zip(res_refs, res_ranges):
            @pl.when((i >= lo) & (i < hi))
            def _(res_ref=res_ref):
                finish(res_ref)


def matmul(pairs, n_cols, *, mode="plain", res=None, p=None, pw=None, out_dtype=F32, out_rows=None, tm=1024, tn=512):
    pairs = [tuple(pr) + (False,) * (5 - len(pr)) for pr in pairs]
    m = pairs[0][0].shape[0]
    res_parts = [] if res is None else (list(res) if isinstance(res, (list, tuple)) else [res])
    out_rows = [m] if out_rows is None else list(out_rows)
    for r in [m] + [part.shape[0] for part in res_parts] + out_rows:
        tm = math.gcd(tm, r)
    assert n_cols % tn == 0
    in_specs = [pl.BlockSpec((tm, a.shape[1]), lambda n, i: (i, 0)) for a, _, _, _, _ in pairs]
    w_blocks = [(tn, a.shape[1]) if tr else (a.shape[1], tn) for a, _, _, _, tr in pairs]
    in_specs += [pl.BlockSpec(blk, (lambda n, i, rb=rb, cb=cb: (n + cb, rb)) if tr else
                              (lambda n, i, rb=rb, cb=cb: (rb, n + cb)))
                 for blk, (_, _, rb, cb, tr) in zip(w_blocks, pairs)]
    args = [pr[0] for pr in pairs] + [pr[1] for pr in pairs]
    cast = tuple(pr[1].dtype != BF16 for pr in pairs)
    transposed = tuple(pr[4] for pr in pairs)
    ragged = tuple((cb, w.shape[0]) if tr and cast_k and (cb * tn + n_cols > w.shape[0]) else None
                   for (_, w, _, cb, tr), cast_k in zip(pairs, cast))
    res_ranges = _part_blocks(res_parts, tm)
    in_specs += [_part_spec((tm, tn), lo, hi, 1, lambda n, i: (n,)) for lo, hi in res_ranges]
    args += res_parts
    if mode == "ple":
        kp = p.shape[1]
        in_specs += [pl.BlockSpec((tm, kp), lambda n, i: (i, 0)), pl.BlockSpec((kp, tn), lambda n, i: (0, n))]
        args += [p, pw]
    out_ranges = _part_blocks([jax.ShapeDtypeStruct((r, n_cols), out_dtype) for r in out_rows], tm)
    assert out_ranges[-1][1] * tm == m
    outs = pl.pallas_call(
        functools.partial(_mm_kernel, n_pair=len(pairs), cast=cast, transposed=transposed, ragged=ragged, mode=mode,
                          res_ranges=res_ranges, out_ranges=out_ranges),
        out_shape=tuple(jax.ShapeDtypeStruct((r, n_cols), out_dtype) for r in out_rows),
        grid=(n_cols // tn, m // tm),
        in_specs=in_specs,
        out_specs=tuple(_part_spec((tm, tn), lo, hi, 1, lambda n, i: (n,)) for lo, hi in out_ranges),
        scratch_shapes=[pltpu.VMEM(blk, BF16) for blk, c in zip(w_blocks, cast) if c],
        compiler_params=_cparams(("arbitrary", "arbitrary")),
        name="matmul_" + mode,
    )(*args)
    return outs[0] if len(outs) == 1 else outs


def _perm_rows_kernel(pt_ref, w_ref, o_ref):
    o_ref[...] = jnp.dot(pt_ref[...], w_ref[...].astype(BF16), preferred_element_type=F32).astype(BF16)


def permute_weight_rows(w, perm_t, row_block0, n_sections, tn=512):
    c = RWKV_WIDTH
    n = w.shape[1]
    return pl.pallas_call(
        _perm_rows_kernel,
        out_shape=jax.ShapeDtypeStruct((n_sections * c, n), BF16),
        grid=(n_sections, _exact_div(n, tn)),
        in_specs=[pl.BlockSpec((c, c), lambda sec, j: (0, 0)),
                  pl.BlockSpec((c, tn), lambda sec, j: (row_block0 + sec, j))],
        out_specs=pl.BlockSpec((c, tn), lambda sec, j: (sec, j)),
        compiler_params=_cparams(("parallel", "parallel")),
        name="permute_weight_rows",
    )(perm_t, w)


def _qk_norm_kernel(*refs, ranges):
    q_ref, k_ref, v_ref, qg_ref, kg_ref, qo_ref, kbo_ref, vo_ref = refs[:8]
    n = len(ranges)
    k_outs, v_outs, kbuf = refs[8:8 + n], refs[8 + n:8 + 2 * n], refs[8 + 2 * n]

    def head_norm(x, g):
        ms = jnp.mean(x * x, axis=-1, keepdims=True)
        return x * lax.rsqrt(ms + RMS_EPS) * g

    qg = qg_ref[...]
    kg = kg_ref[...]
    for c in range(ATT_WIDTH // HEAD_DIM_A):
        sl = slice(c * HEAD_DIM_A, (c + 1) * HEAD_DIM_A)
        qn = head_norm(q_ref[:, sl], qg)
        qo_ref[:, sl] = (qn * (LOG2E * HEAD_DIM_A ** -0.5)).astype(BF16)
        kn = head_norm(k_ref[:, sl], kg)
        kbuf[:, sl] = kn
        kbo_ref[:, sl] = kn.astype(BF16)
    vo_ref[...] = v_ref[...].astype(BF16)
    i = pl.program_id(0)
    for k_out, v_out, (lo, hi) in zip(k_outs, v_outs, ranges):
        @pl.when((i >= lo) & (i < hi))
        def _(k_out=k_out, v_out=v_out):
            _rows_to_token_tiles(kbuf, k_out)
            _rows_to_token_tiles(v_ref, v_out, lane_tiles=2)


def qk_norm(proj_qkv, q_g, k_g, part_rows, tm=256):
    m = proj_qkv.shape[0]
    w = ATT_WIDTH
    hw = 2 * HEAD_DIM_A
    for r in part_rows:
        tm = math.gcd(tm, r)
    ranges = _part_blocks([jax.ShapeDtypeStruct((r, w), F32) for r in part_rows], tm)
    blk = lambda c: pl.BlockSpec((tm, w), lambda i, c=c: (i, c))
    vec = pl.BlockSpec((1, HEAD_DIM_A), lambda i: (0, 0))
    out_blk = pl.BlockSpec((tm, w), lambda i: (i, 0))
    rows = jax.ShapeDtypeStruct((m, w), BF16)
    outs = pl.pallas_call(
        functools.partial(_qk_norm_kernel, ranges=ranges),
        out_shape=(rows, rows, rows)
        + tuple(jax.ShapeDtypeStruct((r, w // HEAD_DIM_A, HEAD_DIM_A), F32) for r in part_rows)
        + tuple(jax.ShapeDtypeStruct((r, w // hw, hw), F32) for r in part_rows),
        grid=(_exact_div(m, tm),),
        in_specs=[blk(0), blk(1), blk(2), vec, vec],
        out_specs=(out_blk, out_blk, out_blk)
        + tuple(_part_spec((tm, w // HEAD_DIM_A, HEAD_DIM_A), lo, hi, 0, lambda i: (0, 0)) for lo, hi in ranges)
        + tuple(_part_spec((tm, w // hw, hw), lo, hi, 0, lambda i: (0, 0)) for lo, hi in ranges),
        scratch_shapes=[pltpu.VMEM((tm, w), F32)],
        compiler_params=_cparams(("arbitrary",)),
        name="qk_norm",
    )(proj_qkv, proj_qkv, proj_qkv, q_g.reshape(1, -1), k_g.reshape(1, -1))
    n = len(part_rows)
    return outs[0], outs[1], outs[2], outs[3:3 + n], outs[3 + n:]


def _bias_kernel(tab_ref, o_ref, *, rel0, masked, key_major, n_valid):
    _, nr, nc = o_ref.shape
    r = lax.broadcasted_iota(jnp.int32, (nr, nc), 0)
    c = lax.broadcasted_iota(jnp.int32, (nr, nc), 1)
    kpos, qpos = (r, c) if key_major else (c, r)
    rel = rel0 + kpos - qpos
    n = jnp.abs(rel)
    large = jnp.full((nr, nc), T5_BUCKETS // 4, jnp.int32)
    for thr in T5_LOG_THRESHOLDS:
        large = large + jnp.where(n >= thr, 1, 0)
    bucket = jnp.where(n < T5_BUCKETS // 4, n, large) + jnp.where(rel > 0, T5_BUCKETS // 2, 0)
    if masked:
        visible = (kpos // CHUNK) <= (qpos // CHUNK)
    for h in range(N_HEADS_A):
        acc = jnp.zeros((nr, nc), F32)
        for b in range(T5_BUCKETS):
            acc = jnp.where(bucket == b, tab_ref[b, h] * LOG2E, acc)
        if masked:
            acc = jnp.where(visible, acc, NEG_INF)
        if n_valid is not None:
            acc = jnp.where(c < n_valid, acc, NEG_INF)
        o_ref[h] = acc


def bias_tiles(table, nr, nc, *, rel0, masked=False, key_major=False, n_valid=None):
    return pl.pallas_call(
        functools.partial(_bias_kernel, rel0=rel0, masked=masked, key_major=key_major, n_valid=n_valid),
        out_shape=jax.ShapeDtypeStruct((N_HEADS_A, nr, nc), F32),
        in_specs=[pl.BlockSpec(memory_space=pltpu.SMEM)],
        out_specs=pl.BlockSpec(memory_space=pltpu.VMEM),
        compiler_params=_cparams(None),
        name="t5_bias",
    )(table)


def _lambda_value(lq1, lk1, lq2, lk2, lam_init):
    s1 = jnp.sum(lq1 * lk1, axis=-1, keepdims=True)
    s2 = jnp.sum(lq2 * lk2, axis=-1, keepdims=True)
    return jnp.exp(s1) - jnp.exp(s2) + lam_init


def _online_step(s, bias, v, m_prev, l_prev, acc_prev):
    nchunk = max(s.shape[1] // LANES, 1)
    width = s.shape[1] // nchunk
    sc = [s[:, c * width:(c + 1) * width] + bias(c * width, width) for c in range(nchunk)]
    m_new = jnp.maximum(m_prev, jnp.max(functools.reduce(jnp.maximum, sc), axis=-1, keepdims=True))
    alpha = jnp.exp2(m_prev - m_new)
    p = [jnp.exp2(x - m_new) for x in sc]
    l_new = alpha * l_prev + jnp.sum(functools.reduce(lambda a, b: a + b, p), axis=-1, keepdims=True)
    pb = p[0].astype(BF16) if nchunk == 1 else jnp.concatenate([x.astype(BF16) for x in p], axis=1)
    return m_new, l_new, alpha * acc_prev + jnp.dot(pb, v, preferred_element_type=F32)


def _diff_finish(acc1, l1, acc2, l2, lam, g, lam_init):
    o = acc1 / l1 - lam * (acc2 / l2)
    ms = jnp.mean(o * o, axis=-1, keepdims=True)
    return (o * lax.rsqrt(ms + RMS_EPS) * g) * (1.0 - lam_init)


_NT = (((1,), (1,)), ((), ()))
BF16_SUBLANES = 16
VT_ROWS = 2 * HEAD_DIM_A + BF16_SUBLANES
LOG2E = math.log2(math.e)


def _prompt_attn_kernel(far_ref, q_ref, k_ref, vt_ref, bd_ref, bl_ref, lq1, lk1, lq2, lk2, g_ref, ya_hbm,
                        o_ref, m_ref, acc_ref, *, blk, nsub, lam_init):
    h = pl.program_id(0)
    i = pl.program_id(1)
    m_ref[...] = jnp.full(m_ref.shape, NEG_INF, F32)
    acc_ref[...] = jnp.zeros(acc_ref.shape, F32)
    d = HEAD_DIM_A
    hw = 2 * d
    far_bias = far_ref[h]

    def chain(st, bias_tile, vt, idx):
        shift = far_bias if bias_tile is None else None
        if bias_tile is not None:
            st = st + bias_tile[0]
        col_max = jnp.max(st, axis=0, keepdims=True)
        if shift is not None:
            col_max = col_max + shift
        m_prev = m_ref[idx]
        m_new = jnp.maximum(m_prev, col_max)
        alpha = jnp.exp2(m_prev - m_new)
        p = jnp.exp2(st - (m_new if shift is None else m_new - shift))
        acc_ref[idx] = alpha * acc_ref[idx] + jnp.dot(vt, p.astype(BF16), preferred_element_type=F32)
        m_ref[idx] = m_new

    def update(j, ahead):
        kb = k_ref[pl.ds(pl.multiple_of(j * blk, blk), blk), :]
        vt = vt_ref[j]
        todo = []
        for sa in range(nsub):
            if ahead[sa] < 0:
                continue
            bias_tile = None if ahead[sa] >= 2 else (bl_ref if ahead[sa] == 1 else bd_ref)
            for mp in range(2):
                st = lax.dot_general(kb[:, mp * d:(mp + 1) * d], q_ref[sa * blk:(sa + 1) * blk, mp * d:(mp + 1) * d],
                                     _NT, preferred_element_type=F32)
                todo.append((st, bias_tile, 2 * sa + mp))
        for st, bias_tile, idx in todo:
            chain(st, bias_tile, vt, idx)

    def far_body(j, carry):
        update(j, [2] * nsub)
        return carry

    lax.fori_loop(0, jnp.maximum(nsub * i - 1, 0), far_body, 0)

    @pl.when(i >= 1)
    def _():
        update(nsub * i - 1, [sa + 1 for sa in range(nsub)])

    for o in range(nsub):
        update(nsub * i + o, [sa - o for sa in range(nsub)])
    lam = _lambda_value(lq1[...], lk1[...], lq2[...], lk2[...], lam_init)
    for sa in range(nsub):
        a1 = acc_ref[2 * sa]
        a2 = acc_ref[2 * sa + 1]
        ot = a1[:hw] / a1[hw:hw + 1] - lam * (a2[:hw] / a2[hw:hw + 1])
        ms = jnp.mean(ot * ot, axis=0, keepdims=True)
        yt = ot * lax.rsqrt(ms + RMS_EPS)
        o_ref[sa * blk:(sa + 1) * blk, :] = ((yt.T * g_ref[...]) * (1.0 - lam_init)).astype(BF16)


def prompt_attention(qn, kn, vt, ya, t, bias_d, bias_l, far_bias, lams, subln_g, lam_init, blk, nsub=4):
    nsub = min(nsub, t // blk)
    bq = nsub * blk
    hw = 2 * HEAD_DIM_A
    vec = pl.BlockSpec((1, HEAD_DIM_A), lambda h, i, far: (0, 0))
    return pl.pallas_call(
        functools.partial(_prompt_attn_kernel, blk=blk, nsub=nsub, lam_init=lam_init),
        out_shape=jax.ShapeDtypeStruct(ya.shape, BF16),
        input_output_aliases={11: 0},
        grid_spec=pltpu.PrefetchScalarGridSpec(
            num_scalar_prefetch=1,
            grid=(N_HEADS_A, _exact_div(t, bq)),
            in_specs=[pl.BlockSpec((bq, hw), lambda h, i, far: (i, h)),
                      pl.BlockSpec((t, hw), lambda h, i, far: (0, h)),
                      pl.BlockSpec((t // blk, VT_ROWS, blk), lambda h, i, far: (0, h, 0)),
                      pl.BlockSpec((1, blk, blk), lambda h, i, far: (h, 0, 0)),
                      pl.BlockSpec((1, blk, blk), lambda h, i, far: (h, 0, 0)),
                      vec, vec, vec, vec,
                      pl.BlockSpec((1, hw), lambda h, i, far: (0, 0)),
                      pl.BlockSpec(memory_space=pl.ANY)],
            out_specs=pl.BlockSpec((bq, hw), lambda h, i, far: (i, h)),
            scratch_shapes=[pltpu.VMEM((2 * nsub, 1, blk), F32), pltpu.VMEM((2 * nsub, VT_ROWS, blk), F32)]),
        compiler_params=_cparams(("arbitrary", "arbitrary")),
        name="prompt_attention",
    )(far_bias, qn, kn, vt, bias_d, bias_l, *lams, subln_g.reshape(1, hw), ya)


def _sublane_transpose8(tiles, sub):
    a = list(tiles)
    for dist in (4, 2, 1):
        keep = (sub % (2 * dist)) < dist
        nxt = list(a)
        for i in range(SUBLANES):
            if i % (2 * dist) < dist:
                x, y = a[i], a[i + dist]
                nxt[i] = jnp.where(keep, x, pltpu.roll(y, dist, axis=0))
                nxt[i + dist] = jnp.where(keep, pltpu.roll(x, SUBLANES - dist, axis=0), y)
        a = nxt
    return a


def _sample_attn_kernel(q_ref, ck_ref, cv_ref, kn_ref, vn_ref, bp_ref, bn_ref, lq1, lk1, lq2, lk2, g_ref,
                        ya_hbm, o_ref, m_ref, l_ref, acc_ref, ks_ref, vs_ref, *, lam_init):
    t = pl.program_id(1)
    last = t == pl.num_programs(1) - 1
    d = HEAD_DIM_A
    tk = ck_ref.shape[1]
    ts = q_ref.shape[0]
    n = 2 * N_HEADS_A
    sub = lax.broadcasted_iota(jnp.int32, (SUBLANES, LANES), 0)

    @pl.when(t == 0)
    def _():
        m_ref[...] = jnp.full(m_ref.shape, NEG_INF, F32)
        l_ref[...] = jnp.zeros(l_ref.shape, F32)
        acc_ref[...] = jnp.zeros(acc_ref.shape, F32)
        ks_ref[:, pl.ds(tk, LANES), :] = jnp.zeros((n, LANES, d), BF16)
        vs_ref[:, pl.ds(tk, LANES), :] = jnp.zeros((N_HEADS_A, LANES, 2 * d), BF16)
        for c in range(n):
            ks_ref[c, pl.ds(tk, ts), :] = kn_ref[:, c * d:(c + 1) * d]
        for h in range(N_HEADS_A):
            vs_ref[h, pl.ds(tk, ts), :] = vn_ref[:, 2 * h * d:2 * (h + 1) * d]

    def to_head_major(src, dst, p0, lanes):
        halves = [_sublane_transpose8(src[SUBLANES * a:SUBLANES * (a + 1)], sub) for a in range(2)]
        for r in range(SUBLANES):
            dst(r)[pl.ds(p0, BF16_SUBLANES), lanes] = jnp.concatenate([halves[0][r], halves[1][r]], 0).astype(BF16)

    def relayout(g, carry):
        p0 = pl.multiple_of(g * BF16_SUBLANES, BF16_SUBLANES)
        kt = ck_ref[0, pl.ds(p0, BF16_SUBLANES), :, :]
        vt = cv_ref[0, pl.ds(p0, BF16_SUBLANES), :, :]
        for a in range(2):
            rows = slice(a * SUBLANES, (a + 1) * SUBLANES)
            to_head_major([kt[p, rows, :] for p in range(BF16_SUBLANES)], lambda r, a=a: ks_ref.at[a * SUBLANES + r],
                          p0, slice(None))
            lanes = slice(a * LANES, (a + 1) * LANES)
            to_head_major([vt[p, :, lanes] for p in range(BF16_SUBLANES)], lambda r: vs_ref.at[r], p0, lanes)
        return carry

    lax.fori_loop(0, tk // BF16_SUBLANES, relayout, 0)

    state = [(m_ref[c], l_ref[c], acc_ref[c]) for c in range(n)]
    logits = [lax.dot_general(q_ref[:, c * d:(c + 1) * d], ks_ref[c], _NT, preferred_element_type=F32)
              for c in range(n)]
    for c in range(n):
        h = c // 2

        def bias(k0, w, h=h):
            if k0 < tk:
                return bp_ref[h, :, k0:k0 + w]
            return jnp.where(last, bn_ref[h], NEG_INF)

        state[c] = _online_step(logits[c], bias, vs_ref[h], *state[c])
    for c in range(n):
        m_ref[c], l_ref[c], acc_ref[c] = state[c]

    @pl.when(last)
    def _():
        lam = _lambda_value(lq1[...], lk1[...], lq2[...], lk2[...], lam_init)
        for h in range(N_HEADS_A):
            y = _diff_finish(acc_ref[2 * h], l_ref[2 * h], acc_ref[2 * h + 1], l_ref[2 * h + 1], lam,
                             g_ref[...], lam_init)
            o_ref[:, 2 * h * d:2 * (h + 1) * d] = y.astype(BF16)


def sample_attention(qn, kn, vn, ya, row0, cache_k, cache_v, bias_past, bias_new, lams, subln_g, lam_init, tk=1024):
    nb, past = cache_k.shape[:2]
    w = ATT_WIDTH
    ts = bias_new.shape[1]
    tk = min(tk, past)
    blk0 = row0 // ts
    hw = 2 * HEAD_DIM_A
    vec = pl.BlockSpec((1, HEAD_DIM_A), lambda b, t: (0, 0))
    new_rows = pl.BlockSpec((ts, w), lambda b, t: (blk0 + b, 0))
    return pl.pallas_call(
        functools.partial(_sample_attn_kernel, lam_init=lam_init),
        out_shape=jax.ShapeDtypeStruct(ya.shape, BF16),
        grid=(nb, _exact_div(past, tk)),
        in_specs=[new_rows,
                  pl.BlockSpec((1, tk, 2 * N_HEADS_A, HEAD_DIM_A), lambda b, t: (b, t, 0, 0)),
                  pl.BlockSpec((1, tk, N_HEADS_A, hw), lambda b, t: (b, t, 0, 0)),
                  new_rows, new_rows,
                  pl.BlockSpec((N_HEADS_A, ts, tk), lambda b, t: (0, 0, t)),
                  pl.BlockSpec((N_HEADS_A, ts, LANES), lambda b, t: (0, 0, 0)),
                  vec, vec, vec, vec,
                  pl.BlockSpec((1, hw), lambda b, t: (0, 0)),
                  pl.BlockSpec(memory_space=pl.ANY)],
        out_specs=pl.BlockSpec((ts, w), lambda b, t: (blk0 + b, 0)),
        input_output_aliases={12: 0},
        scratch_shapes=[pltpu.VMEM((2 * N_HEADS_A, ts, 1), F32), pltpu.VMEM((2 * N_HEADS_A, ts, 1), F32),
                        pltpu.VMEM((2 * N_HEADS_A, ts, hw), F32),
                        pltpu.VMEM((2 * N_HEADS_A, tk + LANES, HEAD_DIM_A), BF16),
                        pltpu.VMEM((N_HEADS_A, tk + LANES, hw), BF16)],
        compiler_params=_cparams(("arbitrary", "arbitrary")),
        name="sample_attention",
    )(qn, cache_k, cache_v, kn, vn, bias_past, bias_new, *lams, subln_g.reshape(1, hw), ya)


def _group_allreduce(x):
    r1 = pltpu.roll(x, N_HEADS_B, axis=1)
    r2 = pltpu.roll(x, 2 * N_HEADS_B, axis=1)
    r3 = pltpu.roll(x, 3 * N_HEADS_B, axis=1)
    return (x + r1) + (r2 + r3)


def _head_sum128(x):
    acc = x[:, 0:LANES]
    for c in range(1, K_HI):
        acc = acc + x[:, c * LANES:(c + 1) * LANES]
    return _group_allreduce(acc)


def _tile16(x128):
    return jnp.concatenate([x128] * K_HI, axis=1)


LORA_USED = RANK_W + RANK_A + RANK_G


def _token_shift(x_ref, pv_ref, st_ref, buf_ref, mu_ref, is_first, is_sample):
    tm = x_ref.shape[0]
    x = x_ref[...]
    buf_ref[pl.ds(0, SUBLANES), :] = jnp.where(is_first, 0.0, pv_ref[...])
    buf_ref[pl.ds(SUBLANES, tm), :] = x
    shifted = buf_ref[pl.ds(SUBLANES - 1, tm), :]
    starts = is_sample & (lax.broadcasted_iota(jnp.int32, (SCAN_TB, 1), 0) == 0)
    prev = jnp.concatenate([jnp.where(starts, st_ref[0, g:g + 1, :], shifted[g * SCAN_TB:(g + 1) * SCAN_TB])
                            for g in range(tm // SCAN_TB)], 0)
    return x + (prev - x) * mu_ref[...]


def _rwkv_prep_kernel(f_ref, pf_ref, sf_ref, lo_ref, plo_ref, slo_ref, mu_ref, mul_ref, w0_ref, a0_ref, kk_ref,
                      ka_ref, rk_ref, w2_ref, a2_ref, g2_ref,
                      ak_o, wr_o, w_o, b_o, km_o, v_o, g_o, br_o, kr_o, bo_o, fbuf, lbuf, *, n_prompt_blocks):
    c = RWKV_WIDTH
    i = pl.program_id(0)
    xm = _token_shift(f_ref, pf_ref, sf_ref, fbuf, mu_ref, i == 0, i >= n_prompt_blocks)
    xl = _token_shift(lo_ref, plo_ref, slo_ref, lbuf, mul_ref, i == 0, i >= n_prompt_blocks)
    r, k, v = xm[:, :c], xm[:, c:2 * c], xm[:, 2 * c:]
    wd, ad, gd = xl[:, :LANES], xl[:, :2 * LANES], xl[:, LANES:]
    lw = w0_ref[...] + jnp.dot(jnp.tanh(wd).astype(BF16), w2_ref[...].astype(BF16), preferred_element_type=F32)
    z = -lw
    softplus = jnp.maximum(z, 0.0) + jnp.log(1.0 + jnp.exp(-jnp.abs(z)))
    log_decay = -jnp.exp(-softplus - 0.5)
    tm = lw.shape[0]
    ri = lax.broadcasted_iota(jnp.int32, (tm, tm), 0)
    ci = lax.broadcasted_iota(jnp.int32, (tm, tm), 1)
    same_run_upto = ((ri // SCAN_TB) == (ci // SCAN_TB)) & (ci <= ri)
    csum = jnp.dot(jnp.where(same_run_upto, 1.0, 0.0), log_decay, precision=lax.Precision.HIGHEST,
                   preferred_element_type=F32)
    p_incl = jnp.exp(csum)
    p_excl = jnp.exp(csum - log_decay)
    inv_incl = jnp.exp(-csum)
    a = _sigmoid(a0_ref[...] + jnp.dot(ad.astype(BF16), a2_ref[...].astype(BF16), preferred_element_type=F32))
    g_o[...] = jnp.dot(_sigmoid(gd).astype(BF16), g2_ref[...].astype(BF16), preferred_element_type=F32)
    kk = k * kk_ref[...]
    norm = jnp.maximum(jnp.sqrt(_tile16(_head_sum128(kk * kk))), 1e-12)
    kk = kk / norm
    kmod = k * (1.0 + (a - 1.0) * ka_ref[...])
    bvec = kk * a
    ak_o[...] = -kk * p_excl
    wr_o[...] = p_incl * r
    w_o[...] = p_incl
    b_o[...] = bvec * inv_incl
    km_o[...] = kmod * inv_incl
    v_o[...] = v
    br_o[...] = _head_sum128(bvec * r)
    kr_o[...] = _head_sum128(kmod * r)
    bo_o[...] = _head_sum128(r * kmod * rk_ref[...])


def rwkv_prep(feat, state, lora, state_lora, prm, n_prompt_rows, tm=128):
    m = feat.shape[0]
    c = RWKV_WIDTH
    row = lambda w: pl.BlockSpec((tm, w), lambda i: (i, 0))
    vec = lambda w: pl.BlockSpec((1, w), lambda i: (0, 0))
    mat = lambda r: pl.BlockSpec((r, c), lambda i: (0, 0))
    big = jax.ShapeDtypeStruct((m, c), F32)
    small = jax.ShapeDtypeStruct((m, LANES), F32)
    assert tm // SCAN_TB <= SUBLANES
    npb = _exact_div(n_prompt_rows, tm)
    prev8 = lambda w: pl.BlockSpec((SUBLANES, w), lambda i: (jnp.maximum(i * (tm // SUBLANES) - 1, 0), 0))
    st = lambda w: pl.BlockSpec((1, SUBLANES, w), lambda i: (jnp.maximum(i - npb, 0), 0, 0))
    return pl.pallas_call(
        functools.partial(_rwkv_prep_kernel, n_prompt_blocks=npb),
        out_shape=(big,) * 7 + (small,) * 3,
        grid=(_exact_div(m, tm),),
        in_specs=[row(3 * c), prev8(3 * c), st(3 * c), row(LORA_COLS), prev8(LORA_COLS), st(LORA_COLS),
                  vec(3 * c), vec(LORA_COLS), vec(c), vec(c), vec(c), vec(c), vec(c),
                  mat(LANES), mat(2 * LANES), mat(LORA_COLS - LANES)],
        out_specs=(row(c),) * 7 + (row(LANES),) * 3,
        scratch_shapes=[pltpu.VMEM((tm + SUBLANES, 3 * c), F32), pltpu.VMEM((tm + SUBLANES, LORA_COLS), F32)],
        compiler_params=_cparams(("arbitrary",)),
        name="rwkv_prep",
    )(feat, feat, state, lora, lora, state_lora, prm["mu_rkv"], prm["mu_lora"], prm["w0"], prm["a0"], prm["k_k"], prm["k_a"],
      prm["r_k"], prm["w2"], prm["a2"], prm["g2"])


def _scan_kernel(seq_ref, first_ref, last_ref, ak_ref, wr_ref, w_ref, b_ref, km_ref, v_ref, br_ref, s0_ref,
                 y_ref, sout_ref, s_ref):
    step = pl.program_id(0)

    @pl.when(first_ref[step] == 1)
    def _():
        s_ref[...] = s0_ref[0]

    sub = lax.broadcasted_iota(jnp.int32, (SUBLANES, LANES), 0)
    grp = lax.broadcasted_iota(jnp.int32, (SUBLANES, LANES), 1) // N_HEADS_B
    own_group = grp == (sub % K_LO)
    low_half = sub < K_LO

    def row(ref, t8, s, c):
        tile = ref[pl.ds(t8, SUBLANES), c * LANES:(c + 1) * LANES]
        return jnp.broadcast_to(tile[s:s + 1], (SUBLANES, LANES))

    def time_step(t8, s, y_lo, y_hi):
        vt = []
        for vb in range(V_BLK):
            tile = jnp.where(low_half, row(v_ref, t8, s, 2 * vb), row(v_ref, t8, s, 2 * vb + 1))
            vt.append(_group_allreduce(jnp.where(own_group, tile, 0.0)))
        acc_u = [[None, None] for _ in range(V_BLK)]
        acc_y = [[None, None] for _ in range(V_BLK)]
        for kh in range(K_HI):
            a_row = row(ak_ref, t8, s, kh)
            wr_row = row(wr_ref, t8, s, kh)
            for vb in range(V_BLK):
                st = s_ref[vb, kh]
                pu, py = st * a_row, st * wr_row
                acc_u[vb][kh % 2] = pu if acc_u[vb][kh % 2] is None else acc_u[vb][kh % 2] + pu
                acc_y[vb][kh % 2] = py if acc_y[vb][kh % 2] is None else acc_y[vb][kh % 2] + py
        acc_u = [a + b for a, b in acc_u]
        acc_y = [a + b for a, b in acc_y]
        br_row = row(br_ref, t8, s, 0)
        u = [_group_allreduce(x) for x in acc_u]
        for vb in range(V_BLK):
            y = _group_allreduce(acc_y[vb]) + u[vb] * br_row
            ym = jnp.where(own_group, y, 0.0)
            ym = ym + pltpu.roll(ym, 1, axis=0)
            ym = ym + pltpu.roll(ym, 2, axis=0)
            y_lo[vb] = jnp.where(sub == s, pltpu.roll(ym, (s - (K_LO - 1)) % SUBLANES, axis=0), y_lo[vb])
            y_hi[vb] = jnp.where(sub == s, pltpu.roll(ym, (s - (2 * K_LO - 1)) % SUBLANES, axis=0), y_hi[vb])
        for kh in range(K_HI):
            b_row = row(b_ref, t8, s, kh)
            km_row = row(km_ref, t8, s, kh)
            for vb in range(V_BLK):
                s_ref[vb, kh] = s_ref[vb, kh] + (b_row * u[vb] + km_row * vt[vb])

    def eight_steps(gi, carry):
        t8 = pl.multiple_of(gi * SUBLANES, SUBLANES)
        y_lo = [jnp.zeros((SUBLANES, LANES), F32) for _ in range(V_BLK)]
        y_hi = [jnp.zeros((SUBLANES, LANES), F32) for _ in range(V_BLK)]
        for s in range(SUBLANES):
            time_step(t8, s, y_lo, y_hi)
        for vb in range(V_BLK):
            y_ref[pl.ds(t8, SUBLANES), (2 * vb) * LANES:(2 * vb + 1) * LANES] = y_lo[vb]
            y_ref[pl.ds(t8, SUBLANES), (2 * vb + 1) * LANES:(2 * vb + 2) * LANES] = y_hi[vb]
        return carry

    lax.fori_loop(0, y_ref.shape[0] // SUBLANES, eight_steps, 0)
    last8 = y_ref.shape[0] - SUBLANES
    for kh in range(K_HI):
        p_end = row(w_ref, last8, SUBLANES - 1, kh)
        for vb in range(V_BLK):
            s_ref[vb, kh] = s_ref[vb, kh] * p_end

    @pl.when(last_ref[step] == 1)
    def _():
        sout_ref[0] = s_ref[...]


def rwkv_scan(ops, br, s0, seq_of_step, first, last):
    m, c = ops[0].shape
    nseq = s0.shape[0]
    tb = SCAN_TB
    row = lambda w: pl.BlockSpec((tb, w), lambda i, sq, fi, la: (i, 0))
    st = pl.BlockSpec((1, V_BLK, K_HI, SUBLANES, LANES), lambda i, sq, fi, la: (sq[i], 0, 0, 0, 0))
    return pl.pallas_call(
        _scan_kernel,
        out_shape=(jax.ShapeDtypeStruct((m, c), F32), jax.ShapeDtypeStruct(s0.shape, F32)),
        grid_spec=pltpu.PrefetchScalarGridSpec(
            num_scalar_prefetch=3,
            grid=(m // tb,),
            in_specs=[row(c)] * 6 + [row(LANES), st],
            out_specs=(row(c), st),
            scratch_shapes=[pltpu.VMEM((V_BLK, K_HI, SUBLANES, LANES), F32)]),
        compiler_params=_cparams(("arbitrary",)),
        name="rwkv_scan",
    )(seq_of_step, first, last, *ops, br, s0)


def _rwkv_post_kernel(y_ref, v_ref, g_ref, kr_ref, bo_ref, lg_ref, lb_ref, o_ref):
    v = v_ref[...]
    y = y_ref[...] + v * _tile16(kr_ref[...])
    mean = _tile16(_head_sum128(y)) * (1.0 / HEAD_B)
    yc = y - mean
    var = _tile16(_head_sum128(yc * yc)) * (1.0 / HEAD_B)
    yn = yc * lax.rsqrt(var + LNX_EPS) * lg_ref[...] + lb_ref[...]
    o_ref[...] = ((yn + _tile16(bo_ref[...]) * v) * g_ref[...]).astype(BF16)


def rwkv_post(y, v, g, kr, bo, lnx_g, lnx_b, tm=256):
    m, c = y.shape
    row = lambda w: pl.BlockSpec((tm, w), lambda i: (i, 0))
    vec = pl.BlockSpec((1, c), lambda i: (0, 0))
    return pl.pallas_call(
        _rwkv_post_kernel,
        out_shape=jax.ShapeDtypeStruct((m, c), BF16),
        grid=(_exact_div(m, tm),),
        in_specs=[row(c), row(c), row(c), row(LANES), row(LANES), vec, vec],
        out_specs=row(c),
        compiler_params=_cparams(("parallel",)),
        name="rwkv_post",
    )(y, v, g, kr, bo, lnx_g, lnx_b)


def _rows_to_token_tiles(src_ref, dst_ref, lane_tiles=1):
    rows, cols = src_ref.shape
    sub = lax.broadcasted_iota(jnp.int32, (SUBLANES, LANES), 0)

    def body(g, carry):
        r0 = pl.multiple_of(g * SUBLANES, SUBLANES)
        for a in range(cols // (SUBLANES * LANES * lane_tiles)):
            for lt in range(lane_tiles):
                chunk = lambda i: (SUBLANES * a + i) * lane_tiles + lt
                tiles = [src_ref[pl.ds(r0, SUBLANES), chunk(i) * LANES:(chunk(i) + 1) * LANES]
                         for i in range(SUBLANES)]
                out = _sublane_transpose8(tiles, sub)
                for p in range(SUBLANES):
                    dst_ref[r0 + p, SUBLANES * a:SUBLANES * (a + 1), lt * LANES:(lt + 1) * LANES] = out[p]
        return carry

    lax.fori_loop(0, rows // SUBLANES, body, 0)


def _token_tiles_to_rows(src, n_rows, chunks, emit):
    sub = lax.broadcasted_iota(jnp.int32, (SUBLANES, LANES), 0)

    def body(g, carry):
        r0 = pl.multiple_of(g * BF16_SUBLANES, BF16_SUBLANES)
        for a in range(chunks // SUBLANES):
            halves = [_sublane_transpose8([src(r0 + SUBLANES * b + p)[SUBLANES * a:SUBLANES * (a + 1), :]
                                           for p in range(SUBLANES)], sub) for b in range(2)]
            for i in range(SUBLANES):
                emit(r0, SUBLANES * a + i, jnp.concatenate([halves[0][i], halves[1][i]], 0))
        return carry

    lax.fori_loop(0, n_rows // BF16_SUBLANES, body, 0)


def _router_kernel(x_ref, g_ref, rw_ref, rb_ref, h_ref, route_ref, hbuf):
    x = x_ref[...]
    ms = jnp.mean(x * x, axis=-1, keepdims=True)
    h = x * lax.rsqrt(ms + RMS_EPS) * g_ref[...]
    hbuf[...] = h
    _rows_to_token_tiles(hbuf, h_ref)
    logits = jnp.dot(h, rw_ref[...], precision=lax.Precision.HIGHEST, preferred_element_type=F32) + rb_ref[...]
    lane = lax.broadcasted_iota(jnp.int32, logits.shape, 1)
    big = jnp.int32(LANES)

    def first_argmax(vals, valid):
        masked = jnp.where(valid, vals, -jnp.inf)
        mx = jnp.max(masked, axis=-1, keepdims=True)
        idx = jnp.min(jnp.where(valid & (masked == mx), lane, big), axis=-1, keepdims=True)
        return mx, idx

    is_group = lane < N_GROUPS
    g_max, g_idx = first_argmax(logits, is_group)
    g_top = 1.0 / jnp.sum(jnp.where(is_group, jnp.exp(logits - g_max), 0.0), axis=-1, keepdims=True)
    in_group = (lane >= N_GROUPS) & (lane < N_GROUPS + N_EXPERTS) & ((lane - N_GROUPS) // EXP_PER_GROUP == g_idx)
    i_max, idx1 = first_argmax(logits, in_group)
    z = jnp.sum(jnp.where(in_group, jnp.exp(logits - i_max), 0.0), axis=-1, keepdims=True)
    i_max2, idx2 = first_argmax(logits, in_group & (lane != idx1))
    p1 = 1.0 / z
    p2 = jnp.exp(i_max2 - i_max) / z
    psum = p1 + p2
    gate1 = g_top * p1 / psum
    gate2 = g_top * p2 / psum
    route = jnp.where(lane == 0, (idx1 - N_GROUPS).astype(F32),
                      jnp.where(lane == 1, (idx2 - N_GROUPS).astype(F32),
                                jnp.where(lane == 2, gate1, jnp.where(lane == 3, gate2, 0.0))))
    route_ref[...] = route


def norm_router(x, g, rw, rb, tm=256):
    m, d = x.shape
    return pl.pallas_call(
        _router_kernel,
        out_shape=(jax.ShapeDtypeStruct((m, d // LANES, LANES), F32), jax.ShapeDtypeStruct((m, LANES), F32)),
        grid=(_exact_div(m, tm),),
        in_specs=[pl.BlockSpec((tm, d), lambda i: (i, 0)), pl.BlockSpec((1, d), lambda i: (0, 0)),
                  pl.BlockSpec((d, LANES), lambda i: (0, 0)), pl.BlockSpec((1, LANES), lambda i: (0, 0))],
        out_specs=(pl.BlockSpec((tm, d // LANES, LANES), lambda i: (i, 0, 0)),
                   pl.BlockSpec((tm, LANES), lambda i: (i, 0))),
        scratch_shapes=[pltpu.VMEM((tm, d), F32)],
        compiler_params=_cparams(("arbitrary",)),
        name="norm_router",
    )(x, g.reshape(1, d), rw, rb)


def _row_copy(src_hbm, buf, sem, slot, src_row, dst_row):
    return pltpu.make_async_copy(src_hbm.at[src_row], buf.at[slot, dst_row], sem.at[slot])


def _wait_slot(src_hbm, buf, sem, slot):
    pltpu.make_async_copy(src_hbm.at[pl.ds(0, buf.shape[1])], buf.at[slot], sem.at[slot]).wait()


ROW_ISSUE_UNROLL = 8
GATHER_SLOTS = 3


def _gather_rows_kernel(idx_ref, nused_ref, src_hbm, o_ref, buf, sem, *, rows):
    b = pl.program_id(0)
    nb = pl.num_programs(0)
    used = nused_ref[0]

    def issue(blk, slot):
        def body(r, c):
            _row_copy(src_hbm, buf, sem, slot, idx_ref[blk * rows + r], r).start()
            return c
        lax.fori_loop(0, rows, body, 0, unroll=ROW_ISSUE_UNROLL)

    ahead = GATHER_SLOTS - 1
    for first in range(ahead):
        @pl.when((b == 0) & (first < used) & (first < nb))
        def _(first=first):
            issue(first, first)

    @pl.when((b + ahead < nb) & (b + ahead < used))
    def _():
        issue(b + ahead, (b + ahead) % GATHER_SLOTS)

    @pl.when(b < used)
    def _():
        slot = b % GATHER_SLOTS
        _wait_slot(src_hbm, buf, sem, slot)

        def emit(r0, j, x):
            o_ref[pl.ds(r0, BF16_SUBLANES), j * LANES:(j + 1) * LANES] = x.astype(o_ref.dtype)

        _token_tiles_to_rows(lambda r: buf[slot, r], rows, buf.shape[2], emit)

    @pl.when(b >= used)
    def _():
        o_ref[...] = jnp.zeros(o_ref.shape, o_ref.dtype)


def gather_rows(src, idx, n_used, rows, out_dtype):
    n = idx.shape[0]
    chunks = src.shape[1]
    d = chunks * LANES
    return pl.pallas_call(
        functools.partial(_gather_rows_kernel, rows=rows),
        out_shape=jax.ShapeDtypeStruct((n, d), out_dtype),
        grid_spec=pltpu.PrefetchScalarGridSpec(
            num_scalar_prefetch=2,
            grid=(n // rows,),
            in_specs=[pl.BlockSpec(memory_space=pl.ANY)],
            out_specs=pl.BlockSpec((rows, d), lambda b, idx, nu: (b, 0)),
            scratch_shapes=[pltpu.VMEM((GATHER_SLOTS, rows, chunks, LANES), src.dtype),
                            pltpu.SemaphoreType.DMA((GATHER_SLOTS,))]),
        compiler_params=_cparams(("arbitrary",)),
        name="gather_rows",
    )(idx, n_used, src)


def _combine_kernel(idx_ref, src_hbm, x_ref, route_ref, o_ref, buf, sem, ybuf, *, rows):
    b = pl.program_id(0)
    nb = pl.num_programs(0)

    def issue(blk, slot):
        def body(r, c):
            base = (blk * rows + r) * TOP_K
            _row_copy(src_hbm, buf, sem, slot, idx_ref[base], r).start()
            _row_copy(src_hbm, buf, sem, slot, idx_ref[base + 1], rows + r).start()
            return c
        lax.fori_loop(0, rows, body, 0, unroll=ROW_ISSUE_UNROLL)

    @pl.when(b == 0)
    def _():
        issue(0, 0)

    @pl.when(b + 1 < nb)
    def _():
        issue(b + 1, (b + 1) % 2)

    slot = b % 2
    _wait_slot(src_hbm, buf, sem, slot)

    def emit(r0, j, x):
        ybuf[pl.ds(r0, BF16_SUBLANES), j * LANES:(j + 1) * LANES] = x

    _token_tiles_to_rows(lambda r: buf[slot, r], TOP_K * rows, buf.shape[2], emit)
    route = route_ref[...]
    o_ref[...] = x_ref[...] + (route[:, 2:3] * ybuf[pl.ds(0, rows), :] + route[:, 3:4] * ybuf[pl.ds(rows, rows), :])


def moe_combine(yb, dest, x, route, rows=128):
    m, d = x.shape
    chunks = yb.shape[1]
    return pl.pallas_call(
        functools.partial(_combine_kernel, rows=rows),
        out_shape=jax.ShapeDtypeStruct((m, d), F32),
        grid_spec=pltpu.PrefetchScalarGridSpec(
            num_scalar_prefetch=1,
            grid=(_exact_div(m, rows),),
            in_specs=[pl.BlockSpec(memory_space=pl.ANY),
                      pl.BlockSpec((rows, d), lambda b, idx: (b, 0)),
                      pl.BlockSpec((rows, LANES), lambda b, idx: (b, 0))],
            out_specs=pl.BlockSpec((rows, d), lambda b, idx: (b, 0)),
            scratch_shapes=[pltpu.VMEM((2, TOP_K * rows, chunks, LANES), F32), pltpu.SemaphoreType.DMA((2,)),
                            pltpu.VMEM((TOP_K * rows, d), F32)]),
        compiler_params=_cparams(("arbitrary",)),
        name="moe_combine",
    )(dest, yb, x, route)


def _expert_runs(block_e, n_used):
    nb = block_e.shape[0]
    idx = jnp.arange(nb, dtype=jnp.int32)
    valid = idx < n_used[0]
    first = valid & ((idx == 0) | (block_e != jnp.roll(block_e, 1)))
    upto = idx[None, :] <= idx[:, None]
    slot = (jnp.sum(jnp.where(upto & first[None, :], 1, 0), axis=1) - 1) % 2
    nxt = jnp.min(jnp.where(first[None, :] & ~upto, idx[None, :], nb), axis=1)
    next_e = jnp.where(nxt < nb, block_e[jnp.minimum(nxt, nb - 1)], -1)
    return first.astype(jnp.int32), slot.astype(jnp.int32), next_e.astype(jnp.int32)


def _stream_expert_weights(b, be_ref, first_ref, slot_ref, next_ref, copies):
    slot = slot_ref[b]

    @pl.when(b == 0)
    def _():
        for cp in copies(be_ref[0], 0):
            cp.start()

    for cp in copies(be_ref[b], slot):
        cp.wait()

    @pl.when(next_ref[b] >= 0)
    def _():
        for cp in copies(next_ref[b], 1 - slot):
            cp.start()

    return slot


WEIGHT_DMA_SPLIT = 4
EXPERT_K_CHUNK = 512


def _expert_up_kernel(be_ref, nused_ref, first_ref, slot_ref, next_ref, x_ref, wg_hbm, wu_hbm, o_ref,
                      wbuf, wgb_ref, wub_ref, sem, *, tf):
    f = pl.program_id(0)
    b = pl.program_id(1)

    def copies(e, slot):
        cols = pl.ds(pl.multiple_of(f * tf, tf), tf)
        rows_per = wbuf.shape[2] // WEIGHT_DMA_SPLIT
        return [pltpu.make_async_copy(w.at[e, pl.ds(q * rows_per, rows_per), cols],
                                      wbuf.at[slot, k, pl.ds(q * rows_per, rows_per)], sem.at[slot, k])
                for k, w in enumerate((wg_hbm, wu_hbm)) for q in range(WEIGHT_DMA_SPLIT)]

    def finish(g, u):
        o_ref[...] = (g * _sigmoid(g) * u).astype(BF16)

    live = b < nused_ref[0]

    @pl.when(live & (first_ref[b] == 1))
    def _():
        slot = _stream_expert_weights(b, be_ref, first_ref, slot_ref, next_ref, copies)
        d = x_ref.shape[1]
        g = jnp.zeros(o_ref.shape, F32)
        u = jnp.zeros(o_ref.shape, F32)
        for k0 in range(0, d, EXPERT_K_CHUNK):
            rows = pl.ds(k0, EXPERT_K_CHUNK)
            wgb_ref[rows, :] = wbuf[slot, 0, rows, :].astype(BF16)
            wub_ref[rows, :] = wbuf[slot, 1, rows, :].astype(BF16)
            xk = x_ref[:, k0:k0 + EXPERT_K_CHUNK]
            g = g + jnp.dot(xk, wgb_ref[rows, :], preferred_element_type=F32)
            u = u + jnp.dot(xk, wub_ref[rows, :], preferred_element_type=F32)
        finish(g, u)

    @pl.when(live & (first_ref[b] == 0))
    def _():
        x = x_ref[...]
        finish(jnp.dot(x, wgb_ref[...], preferred_element_type=F32),
               jnp.dot(x, wub_ref[...], preferred_element_type=F32))

    @pl.when(b >= nused_ref[0])
    def _():
        o_ref[...] = jnp.zeros(o_ref.shape, o_ref.dtype)


def expert_up(xg, block_e, n_used, runs, wg, wu, tf=512):
    n, d = xg.shape
    de = wg.shape[2]
    bm = EXPERT_ROWS
    live = lambda b, nu: jnp.minimum(b, jnp.maximum(nu[0] - 1, 0))
    return pl.pallas_call(
        functools.partial(_expert_up_kernel, tf=tf),
        out_shape=jax.ShapeDtypeStruct((n, de), BF16),
        grid_spec=pltpu.PrefetchScalarGridSpec(
            num_scalar_prefetch=5,
            grid=(de // tf, n // bm),
            in_specs=[pl.BlockSpec((bm, d), lambda f, b, be, nu, fi, sl, ne: (live(b, nu), 0)),
                      pl.BlockSpec(memory_space=pl.ANY), pl.BlockSpec(memory_space=pl.ANY)],
            out_specs=pl.BlockSpec((bm, tf), lambda f, b, be, nu, fi, sl, ne: (b, f)),
            scratch_shapes=[pltpu.VMEM((2, 2, d, tf), F32), pltpu.VMEM((d, tf), BF16), pltpu.VMEM((d, tf), BF16),
                            pltpu.SemaphoreType.DMA((2, 2))]),
        compiler_params=_cparams(("arbitrary", "arbitrary")),
        name="expert_up",
    )(block_e, n_used, *runs, xg, wg, wu)


def _expert_down_kernel(be_ref, nused_ref, first_ref, slot_ref, next_ref, h_ref, wd_hbm, o_ref, wbuf, wdb_ref, sem,
                        ybuf, *, tn):
    c = pl.program_id(0)
    b = pl.program_id(1)

    def copies(e, slot):
        cols = pl.ds(pl.multiple_of(c * tn, tn), tn)
        rows_per = wbuf.shape[1] // WEIGHT_DMA_SPLIT
        return [pltpu.make_async_copy(wd_hbm.at[e, pl.ds(q * rows_per, rows_per), cols],
                                      wbuf.at[slot, pl.ds(q * rows_per, rows_per)], sem.at[slot])
                for q in range(WEIGHT_DMA_SPLIT)]

    live = b < nused_ref[0]

    @pl.when(live & (first_ref[b] == 1))
    def _():
        slot = _stream_expert_weights(b, be_ref, first_ref, slot_ref, next_ref, copies)
        de = h_ref.shape[1]
        y = jnp.zeros(ybuf.shape, F32)
        for k0 in range(0, de, EXPERT_K_CHUNK):
            rows = pl.ds(k0, EXPERT_K_CHUNK)
            wdb_ref[rows, :] = wbuf[slot, rows, :].astype(BF16)
            y = y + jnp.dot(h_ref[:, k0:k0 + EXPERT_K_CHUNK], wdb_ref[rows, :], preferred_element_type=F32)
        ybuf[...] = y

    @pl.when(live & (first_ref[b] == 0))
    def _():
        ybuf[...] = jnp.dot(h_ref[...], wdb_ref[...], preferred_element_type=F32)

    @pl.when(live)
    def _():
        _rows_to_token_tiles(ybuf, o_ref)

    @pl.when(b >= nused_ref[0])
    def _():
        o_ref[...] = jnp.zeros(o_ref.shape, o_ref.dtype)


def expert_down(hmid, block_e, n_used, runs, wd, tn=2048):
    n, de = hmid.shape
    d = wd.shape[2]
    bm = EXPERT_ROWS
    live = lambda b, nu: jnp.minimum(b, jnp.maximum(nu[0] - 1, 0))
    return pl.pallas_call(
        functools.partial(_expert_down_kernel, tn=tn),
        out_shape=jax.ShapeDtypeStruct((n, d // LANES, LANES), F32),
        grid_spec=pltpu.PrefetchScalarGridSpec(
            num_scalar_prefetch=5,
            grid=(d // tn, n // bm),
            in_specs=[pl.BlockSpec((bm, de), lambda c, b, be, nu, fi, sl, ne: (live(b, nu), 0)),
                      pl.BlockSpec(memory_space=pl.ANY)],
            out_specs=pl.BlockSpec((bm, tn // LANES, LANES), lambda c, b, be, nu, fi, sl, ne: (b, c, 0)),
            scratch_shapes=[pltpu.VMEM((2, de, tn), F32), pltpu.VMEM((de, tn), BF16), pltpu.SemaphoreType.DMA((2,)),
                            pltpu.VMEM((bm, tn), F32)]),
        compiler_params=_cparams(("arbitrary", "arbitrary")),
        name="expert_down",
    )(block_e, n_used, *runs, hmid, wd)


def _perm_cols(x):
    pre = x.shape[:-1]
    return jnp.moveaxis(x.reshape(pre + (N_HEADS_B, K_HI, K_LO)), -3, -1).reshape(pre + (RWKV_WIDTH,))


def _unperm_cols(x):
    pre = x.shape[:-1]
    return jnp.moveaxis(x.reshape(pre + (K_HI, K_LO, N_HEADS_B)), -1, -3).reshape(pre + (RWKV_WIDTH,))


def _state_to_tiles(s):
    n = s.shape[0]
    s = s.reshape(n, N_HEADS_B, V_BLK, SUBLANES, K_HI, K_LO)
    return s.transpose(0, 2, 4, 3, 5, 1).reshape(n, V_BLK, K_HI, SUBLANES, LANES)


def _tiles_to_state(s):
    n = s.shape[0]
    s = s.reshape(n, V_BLK, K_HI, SUBLANES, K_LO, N_HEADS_B)
    return s.transpose(0, 5, 1, 3, 2, 4).reshape(n, N_HEADS_B, HEAD_B, HEAD_B)


def _pad_rows(w, row0, rows):
    return jnp.zeros((rows,) + w.shape[1:], w.dtype).at[row0:row0 + w.shape[0]].set(w)


def _trunk_layer(xp, xs, pp, ps, cache_k, cache_v, state_wkv, state_shift, t5_table, lam_init, lp, attn_blk=512):
    t, d = xp.shape
    nb, ts, _ = xs.shape
    past = cache_k.shape[1]
    ms = nb * ts
    m = t + ms
    aw = ATT_WIDTH
    c = RWKV_WIDTH
    assert ts == SCAN_TB and t % SCAN_TB == 0 and past % CHUNK == 0 and ts <= CHUNK

    x_parts = [xp, xs.reshape(ms, d)]
    pe = jnp.concatenate([pp, ps.reshape(ms, -1)], 0).astype(BF16)

    w_t = lp["w_in"].T
    rkv0 = 3 * aw
    src = jnp.arange(c, dtype=jnp.int32)
    dst = (src % HEAD_B // K_LO) * LANES + (src % K_LO) * N_HEADS_B + src // HEAD_B
    perm_t = (src[:, None] == dst[None, :]).astype(BF16)
    assert aw == c
    w_rkv_t = permute_weight_rows(w_t, perm_t, rkv0 // c, 3)
    h1 = rmsnorm_cast(x_parts, lp["norm1_g"])
    tn = 512
    proj_qkv = matmul([(h1, w_t, 0, 0, True)], 3 * aw, tn=tn)
    feat = matmul([(h1, w_rkv_t, 0, 0, True)], 3 * c, tn=tn)
    lora = matmul([(h1, w_t, 0, (rkv0 + 3 * c) // tn, True)], LORA_COLS, tn=tn)

    qn, kn, vn, k_new, v_new = qk_norm(proj_qkv, lp["q_norm_g"], lp["k_norm_g"], (t, ms))
    lams = [lp[n].reshape(1, HEAD_DIM_A) for n in ("lambda_q1", "lambda_k1", "lambda_q2", "lambda_k2")]
    blk = min(attn_blk, t)
    assert t % blk == 0 and blk >= T5_FAR
    bias_d = bias_tiles(t5_table, blk, blk, rel0=0, masked=True, key_major=True)
    bias_l = bias_tiles(t5_table, blk, blk, rel0=-blk, key_major=True)
    far_bias = t5_table[T5_BUCKETS // 2 - 1] * LOG2E
    nkb = t // blk
    ones_tile = jnp.zeros((nkb, N_HEADS_A, BF16_SUBLANES, blk), BF16).at[:, :, 0, :].set(1.0)
    vt = jnp.transpose(vn[:t].reshape(nkb, blk, N_HEADS_A, 2 * HEAD_DIM_A), (0, 2, 3, 1))
    vt = jnp.concatenate([vt, ones_tile], 2).reshape(nkb, N_HEADS_A * VT_ROWS, blk)
    ya = prompt_attention(qn, kn, vt, jnp.zeros((m, aw), BF16), t, bias_d, bias_l, far_bias, lams, lp["subln_g"], lam_init, blk)
    bias_past = bias_tiles(t5_table, ts, past, rel0=-past)
    bias_new = bias_tiles(t5_table, ts, LANES, rel0=0, n_valid=ts)
    ya = sample_attention(qn, kn, vn, ya, t, cache_k.reshape(nb, past, 2 * N_HEADS_A, HEAD_DIM_A), cache_v,
                          bias_past, bias_new, lams, lp["subln_g"], lam_init)

    shift_rkv = jnp.concatenate([_perm_cols(state_shift[:, 0, i * c:(i + 1) * c]) for i in range(3)], 1)
    lora_pad = jnp.zeros((LORA_COLS - LORA_USED,), F32)
    shift_lora = jnp.concatenate([state_shift[:, 0, 3 * c:], jnp.broadcast_to(lora_pad, (nb, lora_pad.shape[0]))], 1)
    prep_tm = 4 * SCAN_TB
    seqs_per_block = prep_tm // ts

    def block_states(rows):
        rows = rows.reshape(ms // prep_tm, seqs_per_block, rows.shape[1])
        return jnp.concatenate([rows, jnp.zeros((ms // prep_tm, SUBLANES - seqs_per_block, rows.shape[2]), F32)], 1)

    vec = lambda v: v.reshape(1, -1)
    mu = lp["rwkv_mu"]
    prm = dict(
        mu_rkv=vec(jnp.concatenate([_perm_cols(mu[i * c:(i + 1) * c]) for i in range(3)])),
        mu_lora=vec(jnp.concatenate([mu[3 * c:], lora_pad])),
        w0=vec(_perm_cols(lp["rwkv_w0"])), a0=vec(_perm_cols(lp["rwkv_a0"])),
        k_k=vec(_perm_cols(lp["rwkv_k_k"])), k_a=vec(_perm_cols(lp["rwkv_k_a"])),
        r_k=vec(_perm_cols(lp["rwkv_r_k"].reshape(-1))),
        w2=_pad_rows(_perm_cols(lp["rwkv_w2"]), 0, LANES), a2=_pad_rows(_perm_cols(lp["rwkv_a2"]), RANK_W, 2 * LANES),
        g2=_pad_rows(_perm_cols(lp["rwkv_g2"]), RANK_W + RANK_A - LANES, LORA_COLS - LANES))
    ak, wr, wdec, bvec, km, vv, gate, br, kr, bonus = rwkv_prep(
        feat, block_states(shift_rkv), lora, block_states(shift_lora), prm, t, tm=prep_tm)
    n_pstep = t // SCAN_TB
    seq_of_step = jnp.concatenate([jnp.zeros((n_pstep,), jnp.int32), 1 + jnp.arange(nb, dtype=jnp.int32)])
    first = jnp.concatenate([jnp.zeros((n_pstep,), jnp.int32).at[0].set(1), jnp.ones((nb,), jnp.int32)])
    last = jnp.concatenate([jnp.zeros((n_pstep,), jnp.int32).at[-1].set(1), jnp.ones((nb,), jnp.int32)])
    s0 = jnp.concatenate([jnp.zeros((1, V_BLK, K_HI, SUBLANES, LANES), F32),
                          _state_to_tiles(state_wkv.astype(F32))], 0)
    y_scan, s_fin = rwkv_scan((ak, wr, wdec, bvec, km, vv), br, s0, seq_of_step, first, last)
    yb = rwkv_post(y_scan, vv, gate, kr, bonus, vec(_perm_cols(lp["lnx_g"])), vec(_perm_cols(lp["lnx_b"])))
    wkv_fin = _tiles_to_state(s_fin)

    def shift_out(rows):
        return jnp.concatenate([_unperm_cols(feat[rows, i * c:(i + 1) * c]) for i in range(3)]
                               + [lora[rows, :LORA_USED]], -1)

    shift_p = shift_out(slice(t - 1, t))
    shift_s = shift_out(slice(t + ts - 1, m, ts))

    w_out_b = matmul([(perm_t, lp["w_out"], 1, 0)], d, out_dtype=BF16)
    x1 = matmul([(ya, lp["w_out"], 0, 0), (yb, w_out_b, 0, 0)], d, mode="residual", res=x_parts)

    rw = jnp.concatenate([lp["rg_w"], lp["ri_w"], jnp.zeros((d, LANES - N_GROUPS - N_EXPERTS), F32)], 1)
    rb = jnp.concatenate([lp["rg_b"], lp["ri_b"].reshape(-1), jnp.zeros((LANES - N_GROUPS - N_EXPERTS,), F32)])
    h2, route = norm_router(x1, lp["norm2_g"], rw, rb.reshape(1, LANES))
    n_assign = m * TOP_K
    flat_e = route[:, :TOP_K].astype(jnp.int32).reshape(n_assign)
    bm = EXPERT_ROWS
    seg = LANES
    onehot = (flat_e[:, None] == jnp.arange(N_EXPERTS, dtype=jnp.int32)[None, :])
    oh = onehot.astype(BF16).reshape(_exact_div(n_assign, seg), seg, N_EXPERTS)
    tri = (jnp.arange(seg)[:, None] >= jnp.arange(seg)[None, :]).astype(BF16)
    within = jnp.einsum("ij,bje->bie", tri, oh, preferred_element_type=F32)
    seg_tot = within[:, -1, :]
    seg_off = jnp.cumsum(seg_tot, axis=0) - seg_tot
    running = (within + seg_off[:, None, :]).reshape(n_assign, N_EXPERTS)
    counts = (seg_off[-1] + seg_tot[-1]).astype(jnp.int32)
    rank = jnp.sum(jnp.where(onehot, running, 0.0), axis=1).astype(jnp.int32) - 1
    pcounts = (counts + bm - 1) // bm * bm
    eid = jnp.arange(N_EXPERTS, dtype=jnp.int32)
    pend = jnp.sum(jnp.where(eid[None, :] <= eid[:, None], pcounts[None, :], 0), axis=1)
    dest = (pend - pcounts)[flat_e] + rank
    n_blocks = n_assign // bm + N_EXPERTS
    rows_tok = jnp.zeros((n_blocks * bm,), jnp.int32).at[dest].set(jnp.arange(n_assign, dtype=jnp.int32) // TOP_K)
    block_row0 = jnp.arange(n_blocks, dtype=jnp.int32) * bm
    block_e = jnp.minimum(jnp.sum((pend[None, :] <= block_row0[:, None]).astype(jnp.int32), axis=1), N_EXPERTS - 1)
    n_used = (pend[-1] // bm).astype(jnp.int32).reshape(1)
    xg = gather_rows(h2, rows_tok, n_used, bm, BF16)
    runs = _expert_runs(block_e, n_used)
    hmid = expert_up(xg, block_e, n_used, runs, lp["e_wg"], lp["e_wu"])
    yexp = expert_down(hmid, block_e, n_used, runs, lp["e_wd"])
    x2 = moe_combine(yexp, dest.astype(jnp.int32), x1, route)

    h3 = rmsnorm_cast([x2], lp["ple_norm_g"])
    yp, ys = matmul([(h3, lp["ple_gate_w"], 0, 0)], d, mode="ple", res=x2, p=pe, pw=lp["ple_proj_w"], out_rows=(t, ms))

    return (yp, ys.reshape(nb, ts, d), k_new, v_new, wkv_fin, shift_p, shift_s)


def kernel(x_prompt, x_sample, p_prompt, p_sample, cache_k, cache_v, state_wkv, state_shift, t5_table, norm1_g, w_in, q_norm_g, k_norm_g, lambda_q1, lambda_k1, lambda_q2, lambda_k2, subln_g, rwkv_mu, rwkv_w0, rwkv_w2, rwkv_a0, rwkv_a2, rwkv_g2, rwkv_k_k, rwkv_k_a, rwkv_r_k, lnx_g, lnx_b, w_out, norm2_g, router_group_w, router_group_b, router_inner_w, router_inner_b, expert_w_gate, expert_w_up, expert_w_down, ple_norm_g, ple_gate_w, ple_proj_w):
    depth = w_in.shape[0]
    bp, t, d = x_prompt.shape
    nb, ts, _ = x_sample.shape
    assert depth == 1 and bp == 1, "one layer and one prompt stream are fused with the sample batch"
    i = 0
    lp = dict(norm1_g=norm1_g[i], w_in=w_in[i], q_norm_g=q_norm_g[i], k_norm_g=k_norm_g[i],
              lambda_q1=lambda_q1[i], lambda_k1=lambda_k1[i], lambda_q2=lambda_q2[i], lambda_k2=lambda_k2[i],
              subln_g=subln_g[i], rwkv_mu=rwkv_mu[i], rwkv_w0=rwkv_w0[i], rwkv_w2=rwkv_w2[i],
              rwkv_a0=rwkv_a0[i], rwkv_a2=rwkv_a2[i], rwkv_g2=rwkv_g2[i], rwkv_k_k=rwkv_k_k[i],
              rwkv_k_a=rwkv_k_a[i], rwkv_r_k=rwkv_r_k[i], lnx_g=lnx_g[i], lnx_b=lnx_b[i], w_out=w_out[i],
              norm2_g=norm2_g[i], rg_w=router_group_w[i], rg_b=router_group_b[i], ri_w=router_inner_w[i],
              ri_b=router_inner_b[i], e_wg=expert_w_gate[i], e_wu=expert_w_up[i], e_wd=expert_w_down[i],
              ple_norm_g=ple_norm_g[i], ple_gate_w=ple_gate_w[i], ple_proj_w=ple_proj_w[i])
    lam_init = 0.8 - 0.6 * math.exp(-0.3 * i)
    yp, ys, k_new, v_new, wkv_fin, shift_p, shift_s = _trunk_layer(
        x_prompt[0], x_sample, p_prompt[i, 0], p_sample[i], cache_k[i], cache_v[i], state_wkv[i],
        state_shift[i], t5_table, lam_init, lp)
    hk = (N_HEADS_A, 2, HEAD_DIM_A)
    hv = (N_HEADS_A, 2 * HEAD_DIM_A)
    return (yp[None], ys,
            k_new[0].reshape((1, 1, t) + hk), v_new[0].reshape((1, 1, t) + hv),
            wkv_fin[:1][None], shift_p.reshape(1, 1, 1, -1),
            k_new[1].reshape((1, nb, ts) + hk), v_new[1].reshape((1, nb, ts) + hv),
            wkv_fin[1:][None], shift_s.reshape(1, nb, 1, -1))
```
